```python
import math
import jax
import jax.numpy as jnp
from jax import lax
import numpy as np

D_MODEL = 1024
BATCH = 4
SEQ = 8192
DEPTH = 1

CHUNK = 64
N_BACK = 8
BAND = (N_BACK + 1) * CHUNK
N_HEADS = 8
HEAD_DIM = 64
ATTN_WIDTH = N_HEADS * HEAD_DIM
REL_CLIP = 256
SG_BLOCK = 128
SG_GROUPS = 4
SG_WIDTH = 512
SG_GROUP_DIM = SG_WIDTH // SG_GROUPS
IN_WIDTH = 3 * ATTN_WIDTH + 2 * SG_WIDTH + 2 * D_MODEL
N_EXPERTS = 32
TOP_K = 4
D_FF = D_MODEL
SWIGLU_LIMIT = 7.0
SWIGLU_ALPHA = 1.702
MOE_BLOCK = 128
EPS = 1e-6

kernel_name = "chunked_attn_gmlp_gated_moe_block"


def rms_norm(x, g):
    xf = x.astype(jnp.float32)
    y = xf * lax.rsqrt(jnp.mean(xf * xf, axis=-1, keepdims=True) + EPS)
    return (y * g.astype(jnp.float32)).astype(x.dtype)


def layer_norm(x, g, b):
    xf = x.astype(jnp.float32)
    mu = jnp.mean(xf, axis=-1, keepdims=True)
    xc = xf - mu
    y = xc * lax.rsqrt(jnp.mean(xc * xc, axis=-1, keepdims=True) + EPS)
    return (y * g.astype(jnp.float32) + b.astype(jnp.float32)).astype(x.dtype)


def rel_index():
    qi = np.arange(CHUNK)[:, None] + N_BACK * CHUNK
    kk = np.arange(BAND)[None, :]
    return np.clip(qi - kk, -REL_CLIP, REL_CLIP) + REL_CLIP


def chunked_attention(q, k, v, rel_bias):
    B, S, H, Dh = q.shape
    n_chunks = S // CHUNK
    pad = N_BACK * CHUNK
    kp = jnp.pad(k, ((0, 0), (pad, 0), (0, 0), (0, 0)))
    vp = jnp.pad(v, ((0, 0), (pad, 0), (0, 0), (0, 0)))
    bias = rel_bias.astype(jnp.float32)[:, rel_index()]
    kk = jnp.arange(BAND)
    scale = 1.0 / math.sqrt(Dh)

    def one_chunk(n):
        qs = lax.dynamic_slice_in_dim(q, n * CHUNK, CHUNK, axis=1)
        ks = lax.dynamic_slice_in_dim(kp, n * CHUNK, BAND, axis=1)
        vs = lax.dynamic_slice_in_dim(vp, n * CHUNK, BAND, axis=1)
        s = jnp.einsum('bqhd,bkhd->bhqk', qs, ks).astype(jnp.float32) * scale + bias
        valid = kk >= (N_BACK - n) * CHUNK
        s = jnp.where(valid, s, -1e30)
        p = jax.nn.softmax(s, axis=-1).astype(v.dtype)
        return jnp.einsum('bhqk,bkhd->bqhd', p, vs)

    out = lax.map(one_chunk, jnp.arange(n_chunks))
    return out.transpose(1, 0, 2, 3, 4).reshape(B, S, H * Dh)


def spatial_gating(u, v, ln_g, ln_b, w_s, b_s):
    B, S, _ = u.shape
    vg = v.reshape(B, S // SG_BLOCK, SG_BLOCK, SG_GROUPS, SG_GROUP_DIM)
    vn = layer_norm(vg, ln_g.reshape(SG_GROUPS, SG_GROUP_DIM), ln_b.reshape(SG_GROUPS, SG_GROUP_DIM))
    mask = jnp.tril(jnp.ones((SG_BLOCK, SG_BLOCK), w_s.dtype))
    mixed = jnp.einsum('gts,bnsgc->bntgc', w_s * mask, vn) + b_s.T[:, :, None]
    return u * mixed.reshape(B, S, SG_WIDTH)


def clamped_swiglu(gu):
    gate, up = jnp.split(gu, 2, axis=-1)
    gate = jnp.minimum(gate, SWIGLU_LIMIT)
    up = jnp.clip(up, -SWIGLU_LIMIT, SWIGLU_LIMIT)
    return (up + 1.0) * (gate * jax.nn.sigmoid(gate * SWIGLU_ALPHA))


def moe(h, router_w, router_b, w_gate_up, b_gate_up, w_down, b_down):
    T, D = h.shape
    logits = (h @ router_w + router_b).astype(jnp.float32)
    top_v, top_i = lax.top_k(logits, TOP_K)
    probs = jax.nn.softmax(top_v, axis=-1)
    n_assign = T * TOP_K
    e_flat = top_i.reshape(n_assign)
    tok_flat = jnp.arange(n_assign, dtype=jnp.int32) // TOP_K
    order = jnp.argsort(e_flat)
    e_sorted = e_flat[order]
    tok_sorted = tok_flat[order]
    w_sorted = probs.reshape(n_assign)[order]
    counts = jnp.zeros((N_EXPERTS,), jnp.int32).at[e_flat].add(1)
    starts = jnp.cumsum(counts) - counts
    padded = (counts + MOE_BLOCK - 1) // MOE_BLOCK * MOE_BLOCK
    pends = jnp.cumsum(padded)
    pstarts = pends - padded
    rank = jnp.arange(n_assign, dtype=jnp.int32) - starts[e_sorted]
    dest = pstarts[e_sorted] + rank
    n_slots = n_assign + N_EXPERTS * MOE_BLOCK
    n_blocks = n_slots // MOE_BLOCK
    slot_tok = jnp.full((n_slots,), T, jnp.int32).at[dest].set(tok_sorted)
    h_pad = jnp.concatenate([h, jnp.zeros((1, D), h.dtype)], axis=0)
    x_slots = h_pad[slot_tok].reshape(n_blocks, MOE_BLOCK, D)
    blk_start = jnp.arange(n_blocks, dtype=jnp.int32) * MOE_BLOCK
    blk_expert = jnp.minimum(jnp.searchsorted(pends, blk_start, side='right'), N_EXPERTS - 1)

    def expert_block(args):
        xb, e = args
        gu = xb @ w_gate_up[e] + b_gate_up[e]
        return clamped_swiglu(gu) @ w_down[e] + b_down[e]

    y_slots = lax.map(expert_block, (x_slots, blk_expert)).reshape(n_slots, D)
    y_assign = y_slots[dest] * w_sorted[:, None].astype(h.dtype)
    return jnp.zeros((T, D), h.dtype).at[tok_sorted].add(y_assign)


def setup_inputs(seed: int = 0) -> dict:
    key = jax.random.key(seed)
    ks = jax.random.split(key, 24)
    f32 = jnp.float32

    def nrm(k, shape, scale):
        return jax.random.normal(k, shape, f32) * scale

    return {
        "x": nrm(ks[0], (BATCH, SEQ, D_MODEL), 1.0),
        "norm1_g": 1.0 + nrm(ks[1], (DEPTH, D_MODEL), 0.02),
        "w_in": nrm(ks[2], (DEPTH, D_MODEL, IN_WIDTH), D_MODEL ** -0.5),
        "q_norm_g": 1.0 + nrm(ks[3], (DEPTH, HEAD_DIM), 0.02),
        "k_norm_g": 1.0 + nrm(ks[4], (DEPTH, HEAD_DIM), 0.02),
        "rel_bias": nrm(ks[5], (DEPTH, N_HEADS, 2 * REL_CLIP + 1), 0.5),
        "sg_ln_g": 1.0 + nrm(ks[6], (DEPTH, SG_WIDTH), 0.02),
        "sg_ln_b": nrm(ks[7], (DEPTH, SG_WIDTH), 0.02),
        "sg_w": nrm(ks[8], (DEPTH, SG_GROUPS, SG_BLOCK, SG_BLOCK), SG_BLOCK ** -0.5),
        "sg_b": 1.0 + nrm(ks[9], (DEPTH, SG_GROUPS, SG_BLOCK), 0.02),
        "w_branch_a": nrm(ks[10], (DEPTH, ATTN_WIDTH, D_MODEL), ATTN_WIDTH ** -0.5),
        "w_branch_b": nrm(ks[11], (DEPTH, SG_WIDTH, D_MODEL), SG_WIDTH ** -0.5),
        "w_out": nrm(ks[12], (DEPTH, D_MODEL, D_MODEL), D_MODEL ** -0.5),
        "norm2_g": 1.0 + nrm(ks[13], (DEPTH, D_MODEL), 0.02),
        "router_w": nrm(ks[14], (DEPTH, D_MODEL, N_EXPERTS), D_MODEL ** -0.5),
        "router_b": nrm(ks[15], (DEPTH, N_EXPERTS), 0.01),
        "w_gate_up": nrm(ks[16], (DEPTH, N_EXPERTS, D_MODEL, 2 * D_FF), D_MODEL ** -0.5),
        "b_gate_up": nrm(ks[17], (DEPTH, N_EXPERTS, 2 * D_FF), 0.01),
        "w_down": nrm(ks[18], (DEPTH, N_EXPERTS, D_FF, D_MODEL), D_FF ** -0.5),
        "b_down": nrm(ks[19], (DEPTH, N_EXPERTS, D_MODEL), 0.01),
    }


def reference(x, norm1_g, w_in, q_norm_g, k_norm_g, rel_bias, sg_ln_g, sg_ln_b, sg_w, sg_b,
              w_branch_a, w_branch_b, w_out, norm2_g, router_w, router_b,
              w_gate_up, b_gate_up, w_down, b_down):
    B, S, D = x.shape
    cuts = np.cumsum([ATTN_WIDTH, ATTN_WIDTH, ATTN_WIDTH, SG_WIDTH, SG_WIDTH, D_MODEL]).tolist()
    for l in range(DEPTH):
        h = rms_norm(x, norm1_g[l])
        proj = jnp.einsum('bsd,de->bse', h, w_in[l])
        q, k, v, u_sg, v_sg, g_a, g_b = jnp.split(proj, cuts, axis=-1)
        q = rms_norm(q.reshape(B, S, N_HEADS, HEAD_DIM), q_norm_g[l])
        k = rms_norm(k.reshape(B, S, N_HEADS, HEAD_DIM), k_norm_g[l])
        v = v.reshape(B, S, N_HEADS, HEAD_DIM)
        y_attn = chunked_attention(q, k, v, rel_bias[l])
        y_sg = spatial_gating(jax.nn.gelu(u_sg), jax.nn.gelu(v_sg),
                              sg_ln_g[l], sg_ln_b[l], sg_w[l], sg_b[l])
        merged = (jax.nn.sigmoid(g_a) * (y_attn @ w_branch_a[l])
                  + jax.nn.sigmoid(g_b) * (y_sg @ w_branch_b[l]))
        x = x + merged @ w_out[l]
        h2 = rms_norm(x, norm2_g[l]).reshape(B * S, D)
        x = x + moe(h2, router_w[l], router_b[l], w_gate_up[l], b_gate_up[l],
                    w_down[l], b_down[l]).reshape(B, S, D)
    return x
```

```python
import functools
import math

import numpy as np
import jax
import jax.numpy as jnp
from jax import lax
from jax.experimental import pallas as pl
from jax.experimental.pallas import tpu as pltpu

F32 = jnp.float32
BF16 = jnp.bfloat16

D_MODEL = 1024
CHUNK = 64
N_BACK = 8
BAND = (N_BACK + 1) * CHUNK
N_HEADS = 8
HEAD_DIM = 64
ATTN_WIDTH = N_HEADS * HEAD_DIM
REL_CLIP = 256
SG_BLOCK = 128
SG_GROUPS = 4
SG_WIDTH = 512
IN_WIDTH = 3 * ATTN_WIDTH + 2 * SG_WIDTH + 2 * D_MODEL
N_EXPERTS = 32
TOP_K = 4
D_FF = D_MODEL
SWIGLU_LIMIT = 7.0
SWIGLU_ALPHA = 1.702
EPS = 1e-6
NEG = -1e30

LANES = 128
VMEM_BYTES_V7X = 64 * 1024 * 1024

SEQ_TILE = 512
Q_BLOCK = 2 * CHUNK
PREV = N_BACK * CHUNK
WIN = PREV + Q_BLOCK
RANK_TILE = 512
ROW_TILE = 512
SCATTER_TILE = 256
COMBINE_TILE = 128

_C_Q = 0
_C_K = ATTN_WIDTH
_C_V = 2 * ATTN_WIDTH
_C_U = 3 * ATTN_WIDTH
_C_VS = _C_U + SG_WIDTH
_C_GA = _C_VS + SG_WIDTH
_C_GB = _C_GA + D_MODEL


def _dot(a, b):
    return jnp.dot(a, b, preferred_element_type=F32)


def _dot_nt(a, b, precision=None):
    return lax.dot_general(a, b, (((1,), (1,)), ((), ())),
                           preferred_element_type=F32, precision=precision)


def _mixer_kernel(x_ref, g1_ref, win_ref, qg_ref, kg_ref, hsum_ref, bias_ref, lng_ref, lnb_ref,
                  sgw_ref, sgb_ref, wa_ref, wb_ref, wo_ref, g2_ref, rwt_ref, rb_ref,
                  x1_ref, h2_ref, ti_ref, pr_ref,
                  h_scr, qe_scr, qo_scr, k_win, v_win, ya_scr, ysg_scr):
    ts = x_ref.shape[1]
    s_idx = pl.program_id(1)

    @pl.when(s_idx == 0)
    def _():
        k_win[0:PREV, :] = jnp.zeros((PREV, ATTN_WIDTH), BF16)
        v_win[0:PREV, :] = jnp.zeros((PREV, ATTN_WIDTH), BF16)

    @pl.when(s_idx > 0)
    def _():
        k_win[0:PREV, :] = k_win[ts:ts + PREV, :]
        v_win[0:PREV, :] = v_win[ts:ts + PREV, :]

    x = x_ref[0]
    ms = jnp.mean(x * x, axis=-1, keepdims=True)
    h_scr[...] = (x * lax.rsqrt(ms + EPS) * g1_ref[...]).astype(BF16)

    def proj(lo, width):
        return _dot(h_scr[...], win_ref[:, lo:lo + width])

    def head_rms(t, g_ref):
        ssq = _dot((t * t).astype(BF16), hsum_ref[...])
        return t * lax.rsqrt(ssq * (1.0 / HEAD_DIM) + EPS) * g_ref[...]

    lane = lax.broadcasted_iota(jnp.int32, (1, ATTN_WIDTH), 1)
    even_head = (lane % (2 * HEAD_DIM)) < HEAD_DIM
    qn = head_rms(proj(_C_Q, ATTN_WIDTH), qg_ref)
    qe_scr[...] = jnp.where(even_head, qn, 0.0).astype(BF16)
    qo_scr[...] = jnp.where(even_head, 0.0, qn).astype(BF16)
    k_win[PREV:PREV + ts, :] = head_rms(proj(_C_K, ATTN_WIDTH), kg_ref).astype(BF16)
    v_win[PREV:PREV + ts, :] = proj(_C_V, ATTN_WIDTH).astype(BF16)

    u = jax.nn.gelu(proj(_C_U, SG_WIDTH))
    vs = jax.nn.gelu(proj(_C_VS, SG_WIDTH))
    r_i = lax.broadcasted_iota(jnp.int32, (SG_BLOCK, SG_BLOCK), 0)
    c_i = lax.broadcasted_iota(jnp.int32, (SG_BLOCK, SG_BLOCK), 1)
    tril = r_i >= c_i
    gdim = SG_WIDTH // SG_GROUPS
    for g in range(SG_GROUPS):
        cs = slice(g * gdim, (g + 1) * gdim)
        vg = vs[:, cs]
        mu = jnp.mean(vg, axis=-1, keepdims=True)
        xc = vg - mu
        var = jnp.mean(xc * xc, axis=-1, keepdims=True)
        vn = (xc * lax.rsqrt(var + EPS) * lng_ref[:, cs] + lnb_ref[:, cs]).astype(BF16)
        wm = jnp.where(tril, sgw_ref[g], 0.0).astype(BF16)
        for j in range(ts // SG_BLOCK):
            rs = slice(j * SG_BLOCK, (j + 1) * SG_BLOCK)
            mixed = _dot(wm, vn[rs]) + sgb_ref[:, cs]
            ysg_scr[rs, cs] = (u[rs, cs] * mixed).astype(BF16)

    col = lax.broadcasted_iota(jnp.int32, (1, WIN), 1)
    out_lane = lax.broadcasted_iota(jnp.int32, (1, 2 * HEAD_DIM), 1)
    first_half = out_lane < HEAD_DIM

    def attn_block(qb, carry):
        r0 = pl.multiple_of(qb * Q_BLOCK, Q_BLOCK)
        valid = col >= (PREV - (s_idx * ts + r0))
        for pair in range(N_HEADS // 2):
            cs = slice(pair * 2 * HEAD_DIM, (pair + 1) * 2 * HEAD_DIM)
            kb = k_win[pl.ds(r0, WIN), cs]
            vb = v_win[pl.ds(r0, WIN), cs]
            outs = []
            for hh, q_scr in enumerate((qe_scr, qo_scr)):
                s = _dot_nt(q_scr[pl.ds(r0, Q_BLOCK), cs], kb) + bias_ref[2 * pair + hh]
                s = jnp.where(valid, s, NEG)
                m = jnp.max(s, axis=-1, keepdims=True)
                p = jnp.exp(s - m)
                l = jnp.sum(p, axis=-1, keepdims=True)
                outs.append(_dot(p.astype(BF16), vb) / l)
            ya_scr[pl.ds(r0, Q_BLOCK), cs] = jnp.where(first_half, outs[0], outs[1]).astype(BF16)
        return carry

    lax.fori_loop(0, ts // Q_BLOCK, attn_block, 0)

    merged = jax.nn.sigmoid(proj(_C_GA, D_MODEL)) * _dot(ya_scr[...], wa_ref[...])
    merged = merged + jax.nn.sigmoid(proj(_C_GB, D_MODEL)) * _dot(ysg_scr[...], wb_ref[...])
    x1 = x_ref[0] + _dot(merged.astype(BF16), wo_ref[...])
    x1_ref[0] = x1

    ms2 = jnp.mean(x1 * x1, axis=-1, keepdims=True)
    h2 = x1 * lax.rsqrt(ms2 + EPS) * g2_ref[...]
    h2_ref[0] = h2
    lt = _dot_nt(rwt_ref[...], h2, precision=lax.Precision.HIGHEST) + rb_ref[...]
    e_iota = lax.broadcasted_iota(jnp.int32, lt.shape, 0)
    vals = []
    for k in range(TOP_K):
        m = jnp.max(lt, axis=0, keepdims=True)
        idx = jnp.min(jnp.where(lt == m, e_iota, N_EXPERTS), axis=0, keepdims=True)
        vals.append(m)
        ti_ref[k:k + 1, :] = idx
        lt = jnp.where(e_iota == idx, -jnp.inf, lt)
    exps = [jnp.exp(v - vals[0]) for v in vals]
    denom = exps[0] + exps[1] + exps[2] + exps[3]
    for k in range(TOP_K):
        pr_ref[k:k + 1, :] = exps[k] / denom


def _const_spec(shape):
    zeros = (0,) * len(shape)
    return pl.BlockSpec(shape, lambda b, s: zeros, pipeline_mode=pl.Buffered(1))


def _mixer(x, g1, win, qg, kg, hsum, bias, lng, lnb, sgw, sgb, wa, wb, wo, g2, rwt, rb):
    bsz, seq, d = x.shape
    ts = SEQ_TILE
    ns = seq // ts
    t = bsz * seq
    consts = (g1, win, qg, kg, hsum, bias, lng, lnb, sgw, sgb, wa, wb, wo, g2, rwt, rb)
    tok_spec = pl.BlockSpec((1, ts, d), lambda b, s: (b, s, 0))
    idx_spec = pl.BlockSpec((TOP_K, ts), lambda b, s: (0, b * ns + s))
    return pl.pallas_call(
        _mixer_kernel,
        grid=(bsz, ns),
        in_specs=[tok_spec] + [_const_spec(c.shape) for c in consts],
        out_specs=[tok_spec, tok_spec, idx_spec, idx_spec],
        out_shape=[
            jax.ShapeDtypeStruct((bsz, seq, d), F32),
            jax.ShapeDtypeStruct((bsz, seq, d), F32),
            jax.ShapeDtypeStruct((TOP_K, t), jnp.int32),
            jax.ShapeDtypeStruct((TOP_K, t), F32),
        ],
        scratch_shapes=[
            pltpu.VMEM((ts, d), BF16),
            pltpu.VMEM((ts, ATTN_WIDTH), BF16),
            pltpu.VMEM((ts, ATTN_WIDTH), BF16),
            pltpu.VMEM((PREV + ts, ATTN_WIDTH), BF16),
            pltpu.VMEM((PREV + ts, ATTN_WIDTH), BF16),
            pltpu.VMEM((ts, ATTN_WIDTH), BF16),
            pltpu.VMEM((ts, SG_WIDTH), BF16),
        ],
        compiler_params=pltpu.CompilerParams(
            dimension_semantics=("arbitrary", "arbitrary"),
            vmem_limit_bytes=VMEM_BYTES_V7X * 7 // 8),
        name="mixer",
    )(x, *consts)


def _attention_bias(rel_bias):
    i = np.arange(Q_BLOCK)[:, None]
    j = np.arange(WIN)[None, :]
    kk = j - (i // CHUNK) * CHUNK
    inside = (kk >= 0) & (kk < BAND)
    rel = np.clip(N_BACK * CHUNK + (i % CHUNK) - kk, -REL_CLIP, REL_CLIP) + REL_CLIP
    table = rel_bias.astype(F32)[:, rel]
    return jnp.where(inside[None], table, NEG)


def _rank_kernel(ti_ref, upper_ref, dest_ref, cnt_ref, total_scr, base_scr):
    phase = pl.program_id(0)
    i = pl.program_id(1)
    tt = ti_ref.shape[1]

    @pl.when((phase == 0) & (i == 0))
    def _():
        total_scr[...] = jnp.zeros_like(total_scr)

    ti = ti_ref[...]
    e_iota = lax.broadcasted_iota(jnp.int32, (N_EXPERTS, tt), 0)
    hits = [e_iota == ti[k:k + 1, :] for k in range(TOP_K)]
    cnt = hits[0].astype(F32)
    for k in range(1, TOP_K):
        cnt = cnt + hits[k].astype(F32)
    tile_total = jnp.sum(cnt, axis=1, keepdims=True)

    @pl.when(phase == 0)
    def _():
        total_scr[...] = total_scr[...] + tile_total

    @pl.when((phase == 1) & (i == 0))
    def _():
        cnt_ref[...] = total_scr[...].astype(jnp.int32)
        run = jnp.zeros((1, LANES), F32)
        for e in range(N_EXPERTS):
            base_scr[e:e + 1, :] = run
            run = run + total_scr[e:e + 1, :]

    @pl.when(phase == 1)
    def _():
        before = _dot(cnt.astype(BF16), upper_ref[...]) + base_scr[:, 0:1]
        for k in range(TOP_K):
            dest_ref[k:k + 1, :] = jnp.sum(jnp.where(hits[k], before, 0.0), axis=0,
                                           keepdims=True).astype(jnp.int32)
        base_scr[...] = base_scr[...] + tile_total


def _rank(top_i):
    t = top_i.shape[1]
    tt = RANK_TILE
    upper = jnp.asarray(np.triu(np.ones((tt, tt), np.float32), 1), BF16)
    return pl.pallas_call(
        _rank_kernel,
        grid=(2, t // tt),
        in_specs=[pl.BlockSpec((TOP_K, tt), lambda p, i: (0, i)),
                  pl.BlockSpec((tt, tt), lambda p, i: (0, 0), pipeline_mode=pl.Buffered(1))],
        out_specs=[pl.BlockSpec((TOP_K, tt), lambda p, i: (0, i * p)),
                   pl.BlockSpec((N_EXPERTS, LANES), lambda p, i: (0, 0))],
        out_shape=[jax.ShapeDtypeStruct((TOP_K, t), jnp.int32),
                   jax.ShapeDtypeStruct((N_EXPERTS, LANES), jnp.int32)],
        scratch_shapes=[pltpu.VMEM((N_EXPERTS, LANES), F32),
                        pltpu.VMEM((N_EXPERTS, LANES), F32)],
        compiler_params=pltpu.CompilerParams(dimension_semantics=("arbitrary", "arbitrary")),
        name="rank",
    )(top_i, upper)


def _plan_kernel(cnt_ref, tile_ref, exp_ref, lo_ref, nwork_ref):
    n_items = tile_ref.shape[0]

    def per_expert(e, carry):
        w, start = carry
        c = cnt_ref[e]
        end = start + c
        t0 = start // ROW_TILE
        n = jnp.where(c > 0, (end - 1) // ROW_TILE - t0 + 1, 0)

        def per_tile(j, _):
            tile = t0 + j
            tile_ref[w + j] = tile
            exp_ref[w + j] = e
            lo_ref[w + j] = jnp.maximum(start - tile * ROW_TILE, 0)
            return 0

        lax.fori_loop(0, n, per_tile, 0)
        return w + n, end

    n_work, _ = lax.fori_loop(0, N_EXPERTS, per_expert, (jnp.int32(0), jnp.int32(0)))
    nwork_ref[0] = n_work

    def pad(w, _):
        tile_ref[w] = tile_ref[n_work - 1]
        exp_ref[w] = exp_ref[n_work - 1]
        lo_ref[w] = 0
        return 0

    lax.fori_loop(n_work, n_items, pad, 0)


def _plan(counts, n_items):
    smem = pl.BlockSpec(memory_space=pltpu.SMEM)
    vec = jax.ShapeDtypeStruct((n_items,), jnp.int32)
    return pl.pallas_call(
        _plan_kernel,
        in_specs=[smem],
        out_specs=[smem, smem, smem, smem],
        out_shape=[vec, vec, vec, jax.ShapeDtypeStruct((1,), jnp.int32)],
        name="plan",
    )(counts)


def _scatter_kernel(dest_ref, h_ref, xs_hbm, sem):
    tb = h_ref.shape[0]

    def row_copy(r, k):
        d = dest_ref[0, 0, k * tb + r]
        return pltpu.make_async_copy(h_ref.at[pl.ds(r, 1)], xs_hbm.at[pl.ds(d, 1)], sem)

    def issue(r, carry):
        for k in range(TOP_K):
            row_copy(r, k).start()
        return carry

    lax.fori_loop(0, tb, issue, 0)

    def drain(r, carry):
        for k in range(TOP_K):
            row_copy(r, k).wait()
        return carry

    lax.fori_loop(0, tb, drain, 0)


def _tile_major(a, tb):
    k, t = a.shape
    return a.reshape(k, t // tb, tb).transpose(1, 0, 2).reshape(t // tb, 1, k * tb)


def _scatter(h2, dest):
    t, d = h2.shape
    tb = SCATTER_TILE
    return pl.pallas_call(
        _scatter_kernel,
        grid=(t // tb,),
        in_specs=[pl.BlockSpec((1, 1, TOP_K * tb), lambda i: (i, 0, 0), memory_space=pltpu.SMEM),
                  pl.BlockSpec((tb, d), lambda i: (i, 0))],
        out_specs=pl.BlockSpec(memory_space=pl.ANY),
        out_shape=jax.ShapeDtypeStruct((t * TOP_K, d), h2.dtype),
        scratch_shapes=[pltpu.SemaphoreType.DMA],
        compiler_params=pltpu.CompilerParams(dimension_semantics=("arbitrary",)),
        name="scatter",
    )(_tile_major(dest, tb), h2)


def _expert_kernel(tile_ref, exp_ref, lo_ref, nwork_ref, xs_ref, wgu_ref, bgu_ref, wd_ref, bd_ref,
                   y_ref):
    w = pl.program_id(0)

    @pl.when(w < nwork_ref[0])
    def _():
        gu = _dot(xs_ref[...].astype(BF16), wgu_ref[0]) + bgu_ref[0]
        gate = jnp.minimum(gu[:, :D_FF], SWIGLU_LIMIT)
        up = jnp.clip(gu[:, D_FF:], -SWIGLU_LIMIT, SWIGLU_LIMIT)
        act = (up + 1.0) * (gate * jax.nn.sigmoid(gate * SWIGLU_ALPHA))
        y = _dot(act.astype(BF16), wd_ref[0]) + bd_ref[0]
        lo = lo_ref[w]

        @pl.when(lo == 0)
        def _():
            y_ref[...] = y

        @pl.when(lo > 0)
        def _():
            row = lax.broadcasted_iota(jnp.int32, y.shape, 0)
            y_ref[...] = jnp.where(row >= lo, y, y_ref[...])


def _experts(xs, plan, wgu, bgu, wd, bd):
    n_rows, d = xs.shape
    w_tile, w_exp, w_lo, n_work = plan
    n_items = w_tile.shape[0]
    tm = ROW_TILE
    grid_spec = pltpu.PrefetchScalarGridSpec(
        num_scalar_prefetch=4,
        grid=(n_items,),
        in_specs=[
            pl.BlockSpec((tm, d), lambda w, tl, ex, lo, nw: (tl[w], 0)),
            pl.BlockSpec((1, d, 2 * D_FF), lambda w, tl, ex, lo, nw: (ex[w], 0, 0)),
            pl.BlockSpec((1, 1, 2 * D_FF), lambda w, tl, ex, lo, nw: (ex[w], 0, 0)),
            pl.BlockSpec((1, D_FF, d), lambda w, tl, ex, lo, nw: (ex[w], 0, 0)),
            pl.BlockSpec((1, 1, d), lambda w, tl, ex, lo, nw: (ex[w], 0, 0)),
        ],
        out_specs=pl.BlockSpec((tm, d), lambda w, tl, ex, lo, nw: (tl[w], 0)),
    )
    return pl.pallas_call(
        _expert_kernel,
        grid_spec=grid_spec,
        out_shape=jax.ShapeDtypeStruct((n_rows, d), F32),
        compiler_params=pltpu.CompilerParams(
            dimension_semantics=("arbitrary",),
            vmem_limit_bytes=VMEM_BYTES_V7X * 3 // 4),
        name="experts",
    )(w_tile, w_exp, w_lo, n_work, xs, wgu, bgu, wd, bd)


def _combine_kernel(dest_ref, x1_ref, pt_ref, y_hbm, out_ref, rows_scr, sem):
    tb = x1_ref.shape[0]

    def row_copy(r, k):
        d = dest_ref[0, 0, k * tb + r]
        return pltpu.make_async_copy(y_hbm.at[pl.ds(d, 1)], rows_scr.at[k, pl.ds(r, 1)], sem)

    def issue(r, carry):
        for k in range(TOP_K):
            row_copy(r, k).start()
        return carry

    lax.fori_loop(0, tb, issue, 0)

    def drain(r, carry):
        for k in range(TOP_K):
            row_copy(r, k).wait()
        return carry

    lax.fori_loop(0, tb, drain, 0)

    acc = rows_scr[0] * pt_ref[:, 0:1]
    for k in range(1, TOP_K):
        acc = acc + rows_scr[k] * pt_ref[:, k:k + 1]
    out_ref[...] = x1_ref[...] + acc


def _combine(x1, probs_t, dest, y):
    t, d = x1.shape
    tb = COMBINE_TILE
    return pl.pallas_call(
        _combine_kernel,
        grid=(t // tb,),
        in_specs=[pl.BlockSpec((1, 1, TOP_K * tb), lambda i: (i, 0, 0), memory_space=pltpu.SMEM),
                  pl.BlockSpec((tb, d), lambda i: (i, 0)),
                  pl.BlockSpec((tb, TOP_K), lambda i: (i, 0)),
                  pl.BlockSpec(memory_space=pl.ANY)],
        out_specs=pl.BlockSpec((tb, d), lambda i: (i, 0)),
        out_shape=jax.ShapeDtypeStruct((t, d), F32),
        scratch_shapes=[pltpu.VMEM((TOP_K, tb, d), F32), pltpu.SemaphoreType.DMA],
        compiler_params=pltpu.CompilerParams(dimension_semantics=("arbitrary",)),
        name="combine",
    )(_tile_major(dest, tb), x1, probs_t, y)


def _layer(x, norm1_g, w_in, q_norm_g, k_norm_g, rel_bias, sg_ln_g, sg_ln_b, sg_w, sg_b,
           w_branch_a, w_branch_b, w_out, norm2_g, router_w, router_b,
           w_gate_up, b_gate_up, w_down, b_down):
    bsz, seq, d = x.shape
    t = bsz * seq
    row = lambda v: v.reshape(1, -1).astype(F32)
    scale = 1.0 / math.sqrt(HEAD_DIM)
    head_sum = jnp.asarray(np.kron(np.eye(N_HEADS), np.ones((HEAD_DIM, HEAD_DIM))), BF16)
    sgb = jnp.repeat(sg_b.T.astype(F32), SG_WIDTH // SG_GROUPS, axis=1)

    x1, h2, top_i, probs = _mixer(
        x, row(norm1_g), w_in.astype(BF16),
        row(jnp.tile(q_norm_g, N_HEADS)) * scale, row(jnp.tile(k_norm_g, N_HEADS)),
        head_sum, _attention_bias(rel_bias), row(sg_ln_g), row(sg_ln_b),
        sg_w.astype(F32), sgb, w_branch_a.astype(BF16), w_branch_b.astype(BF16),
        w_out.astype(BF16), row(norm2_g), router_w.T.astype(F32),
        router_b.reshape(N_EXPERTS, 1).astype(F32))
    x1 = x1.reshape(t, d)
    h2 = h2.reshape(t, d)

    dest, counts = _rank(top_i)
    n_items = t * TOP_K // ROW_TILE + N_EXPERTS
    plan = _plan(counts[:, 0], n_items)
    xs = _scatter(h2, dest)
    y = _experts(xs, plan, w_gate_up.astype(BF16), b_gate_up.reshape(N_EXPERTS, 1, -1).astype(F32),
                 w_down.astype(BF16), b_down.reshape(N_EXPERTS, 1, -1).astype(F32))
    out = _combine(x1, probs.T, dest, y)
    return out.reshape(bsz, seq, d)


def kernel(x, norm1_g, w_in, q_norm_g, k_norm_g, rel_bias, sg_ln_g, sg_ln_b, sg_w, sg_b,
           w_branch_a, w_branch_b, w_out, norm2_g, router_w, router_b,
           w_gate_up, b_gate_up, w_down, b_down):
    depth = norm1_g.shape[0]
    for l in range(depth):
        x = _layer(x, norm1_g[l], w_in[l], q_norm_g[l], k_norm_g[l], rel_bias[l], sg_ln_g[l],
                   sg_ln_b[l], sg_w[l], sg_b[l], w_branch_a[l], w_branch_b[l], w_out[l],
                   norm2_g[l], router_w[l], router_b[l], w_gate_up[l], b_gate_up[l],
                   w_down[l], b_down[l])
    return x
```

```python
import functools
import math

import numpy as np
import jax
import jax.numpy as jnp
from jax import lax
from jax.experimental import pallas as pl
from jax.experimental.pallas import tpu as pltpu
from jax.experimental.pallas import tpu_sc as plsc

F32 = jnp.float32
BF16 = jnp.bfloat16

D_MODEL = 1024
CHUNK = 64
N_BACK = 8
BAND = (N_BACK + 1) * CHUNK
N_HEADS = 8
HEAD_DIM = 64
ATTN_WIDTH = N_HEADS * HEAD_DIM
REL_CLIP = 256
SG_BLOCK = 128
SG_GROUPS = 4
SG_WIDTH = 512
IN_WIDTH = 3 * ATTN_WIDTH + 2 * SG_WIDTH + 2 * D_MODEL
N_EXPERTS = 32
TOP_K = 4
D_FF = D_MODEL
SWIGLU_LIMIT = 7.0
SWIGLU_ALPHA = 1.702
EPS = 1e-6
NEG = -1e30

LANES = 128
VMEM_BYTES_V7X = 64 * 1024 * 1024

SEQ_TILE = 512
Q_BLOCK = 2 * CHUNK
PREV = N_BACK * CHUNK
WIN = PREV + Q_BLOCK
RANK_TILE = 512
ROW_TILE = 512
COMBINE_TILE = 128

SC_CORES = 2
SC_SUBCORES = 16
SC_WORKERS = SC_CORES * SC_SUBCORES
SC_SCATTER_CHUNK = 32

_C_Q = 0
_C_K = ATTN_WIDTH
_C_V = 2 * ATTN_WIDTH
_C_U = 3 * ATTN_WIDTH
_C_VS = _C_U + SG_WIDTH
_C_GA = _C_VS + SG_WIDTH
_C_GB = _C_GA + D_MODEL


def _dot(a, b):
    return jnp.dot(a, b, preferred_element_type=F32)


def _dot_nt(a, b, precision=None):
    return lax.dot_general(a, b, (((1,), (1,)), ((), ())),
                           preferred_element_type=F32, precision=precision)


def _mixer_kernel(x_ref, g1_ref, win_ref, qg_ref, kg_ref, hsum_ref, bias_ref, lng_ref, lnb_ref,
                  sgw_ref, sgb_ref, wa_ref, wb_ref, wo_ref, g2_ref, rwt_ref, rb_ref,
                  x1_ref, h2_ref, ti_ref, pr_ref,
                  h_scr, qe_scr, qo_scr, k_win, v_win, ya_scr, ysg_scr):
    ts = x_ref.shape[1]
    s_idx = pl.program_id(1)

    @pl.when(s_idx == 0)
    def _():
        k_win[0:PREV, :] = jnp.zeros((PREV, ATTN_WIDTH), BF16)
        v_win[0:PREV, :] = jnp.zeros((PREV, ATTN_WIDTH), BF16)

    @pl.when(s_idx > 0)
    def _():
        k_win[0:PREV, :] = k_win[ts:ts + PREV, :]
        v_win[0:PREV, :] = v_win[ts:ts + PREV, :]

    x = x_ref[0]
    ms = jnp.mean(x * x, axis=-1, keepdims=True)
    h_scr[...] = (x * lax.rsqrt(ms + EPS) * g1_ref[...]).astype(BF16)

    def proj(lo, width):
        return _dot(h_scr[...], win_ref[:, lo:lo + width])

    def head_rms(t, g_ref):
        ssq = _dot((t * t).astype(BF16), hsum_ref[...])
        return t * lax.rsqrt(ssq * (1.0 / HEAD_DIM) + EPS) * g_ref[...]

    lane = lax.broadcasted_iota(jnp.int32, (1, ATTN_WIDTH), 1)
    even_head = (lane % (2 * HEAD_DIM)) < HEAD_DIM
    qn = head_rms(proj(_C_Q, ATTN_WIDTH), qg_ref)
    qe_scr[...] = jnp.where(even_head, qn, 0.0).astype(BF16)
    qo_scr[...] = jnp.where(even_head, 0.0, qn).astype(BF16)
    k_win[PREV:PREV + ts, :] = head_rms(proj(_C_K, ATTN_WIDTH), kg_ref).astype(BF16)
    v_win[PREV:PREV + ts, :] = proj(_C_V, ATTN_WIDTH).astype(BF16)

    u = jax.nn.gelu(proj(_C_U, SG_WIDTH))
    vs = jax.nn.gelu(proj(_C_VS, SG_WIDTH))
    r_i = lax.broadcasted_iota(jnp.int32, (SG_BLOCK, SG_BLOCK), 0)
    c_i = lax.broadcasted_iota(jnp.int32, (SG_BLOCK, SG_BLOCK), 1)
    tril = r_i >= c_i
    gdim = SG_WIDTH // SG_GROUPS
    for g in range(SG_GROUPS):
        cs = slice(g * gdim, (g + 1) * gdim)
        vg = vs[:, cs]
        mu = jnp.mean(vg, axis=-1, keepdims=True)
        xc = vg - mu
        var = jnp.mean(xc * xc, axis=-1, keepdims=True)
        vn = (xc * lax.rsqrt(var + EPS) * lng_ref[:, cs] + lnb_ref[:, cs]).astype(BF16)
        wm = jnp.where(tril, sgw_ref[g], 0.0).astype(BF16)
        for j in range(ts // SG_BLOCK):
            rs = slice(j * SG_BLOCK, (j + 1) * SG_BLOCK)
            mixed = _dot(wm, vn[rs]) + sgb_ref[:, cs]
            ysg_scr[rs, cs] = (u[rs, cs] * mixed).astype(BF16)

    col = lax.broadcasted_iota(jnp.int32, (1, WIN), 1)
    out_lane = lax.broadcasted_iota(jnp.int32, (1, 2 * HEAD_DIM), 1)
    first_half = out_lane < HEAD_DIM

    def attn_block(qb, carry):
        r0 = pl.multiple_of(qb * Q_BLOCK, Q_BLOCK)
        valid = col >= (PREV - (s_idx * ts + r0))
        for pair in range(N_HEADS // 2):
            cs = slice(pair * 2 * HEAD_DIM, (pair + 1) * 2 * HEAD_DIM)
            kb = k_win[pl.ds(r0, WIN), cs]
            vb = v_win[pl.ds(r0, WIN), cs]
            outs = []
            for hh, q_scr in enumerate((qe_scr, qo_scr)):
                s = _dot_nt(q_scr[pl.ds(r0, Q_BLOCK), cs], kb) + bias_ref[2 * pair + hh]
                s = jnp.where(valid, s, NEG)
                m = jnp.max(s, axis=-1, keepdims=True)
                p = jnp.exp(s - m)
                l = jnp.sum(p, axis=-1, keepdims=True)
                outs.append(_dot(p.astype(BF16), vb) / l)
            ya_scr[pl.ds(r0, Q_BLOCK), cs] = jnp.where(first_half, outs[0], outs[1]).astype(BF16)
        return carry

    lax.fori_loop(0, ts // Q_BLOCK, attn_block, 0)

    merged = jax.nn.sigmoid(proj(_C_GA, D_MODEL)) * _dot(ya_scr[...], wa_ref[...])
    merged = merged + jax.nn.sigmoid(proj(_C_GB, D_MODEL)) * _dot(ysg_scr[...], wb_ref[...])
    x1 = x_ref[0] + _dot(merged.astype(BF16), wo_ref[...])
    x1_ref[0] = x1

    ms2 = jnp.mean(x1 * x1, axis=-1, keepdims=True)
    h2 = x1 * lax.rsqrt(ms2 + EPS) * g2_ref[...]
    h2_ref[0] = h2
    lt = _dot_nt(rwt_ref[...], h2, precision=lax.Precision.HIGHEST) + rb_ref[...]
    e_iota = lax.broadcasted_iota(jnp.int32, lt.shape, 0)
    vals = []
    for k in range(TOP_K):
        m = jnp.max(lt, axis=0, keepdims=True)
        idx = jnp.min(jnp.where(lt == m, e_iota, N_EXPERTS), axis=0, keepdims=True)
        vals.append(m)
        ti_ref[k:k + 1, :] = idx
        lt = jnp.where(e_iota == idx, -jnp.inf, lt)
    exps = [jnp.exp(v - vals[0]) for v in vals]
    denom = exps[0] + exps[1] + exps[2] + exps[3]
    for k in range(TOP_K):
        pr_ref[k:k + 1, :] = exps[k] / denom


def _const_spec(shape):
    zeros = (0,) * len(shape)
    return pl.BlockSpec(shape, lambda b, s: zeros, pipeline_mode=pl.Buffered(1))


def _mixer(x, g1, win, qg, kg, hsum, bias, lng, lnb, sgw, sgb, wa, wb, wo, g2, rwt, rb):
    bsz, seq, d = x.shape
    ts = SEQ_TILE
    ns = seq // ts
    t = bsz * seq
    consts = (g1, win, qg, kg, hsum, bias, lng, lnb, sgw, sgb, wa, wb, wo, g2, rwt, rb)
    tok_spec = pl.BlockSpec((1, ts, d), lambda b, s: (b, s, 0))
    idx_spec = pl.BlockSpec((TOP_K, ts), lambda b, s: (0, b * ns + s))
    return pl.pallas_call(
        _mixer_kernel,
        grid=(bsz, ns),
        in_specs=[tok_spec] + [_const_spec(c.shape) for c in consts],
        out_specs=[tok_spec, tok_spec, idx_spec, idx_spec],
        out_shape=[
            jax.ShapeDtypeStruct((bsz, seq, d), F32),
            jax.ShapeDtypeStruct((bsz, seq, d), F32),
            jax.ShapeDtypeStruct((TOP_K, t), jnp.int32),
            jax.ShapeDtypeStruct((TOP_K, t), F32),
        ],
        scratch_shapes=[
            pltpu.VMEM((ts, d), BF16),
            pltpu.VMEM((ts, ATTN_WIDTH), BF16),
            pltpu.VMEM((ts, ATTN_WIDTH), BF16),
            pltpu.VMEM((PREV + ts, ATTN_WIDTH), BF16),
            pltpu.VMEM((PREV + ts, ATTN_WIDTH), BF16),
            pltpu.VMEM((ts, ATTN_WIDTH), BF16),
            pltpu.VMEM((ts, SG_WIDTH), BF16),
        ],
        compiler_params=pltpu.CompilerParams(
            dimension_semantics=("arbitrary", "arbitrary"),
            vmem_limit_bytes=VMEM_BYTES_V7X * 7 // 8),
        name="mixer",
    )(x, *consts)


def _attention_bias(rel_bias):
    i = np.arange(Q_BLOCK)[:, None]
    j = np.arange(WIN)[None, :]
    kk = j - (i // CHUNK) * CHUNK
    inside = (kk >= 0) & (kk < BAND)
    tab = rel_bias.astype(F32)
    far = PREV + Q_BLOCK - 1 - REL_CLIP + 1
    near = Q_BLOCK - 1 + REL_CLIP
    ext = jnp.concatenate(
        [jnp.broadcast_to(tab[:, 2 * REL_CLIP:], (N_HEADS, far)),
         tab[:, 2 * REL_CLIP - near:2 * REL_CLIP][:, ::-1]], axis=1)
    rows = [ext[:, Q_BLOCK - 1 - r:Q_BLOCK - 1 - r + WIN] for r in range(Q_BLOCK)]
    return jnp.where(inside[None], jnp.stack(rows, axis=1), NEG)


def _rank_kernel(ti_ref, upper_ref, dest_ref, cnt_ref, total_scr, base_scr):
    phase = pl.program_id(0)
    i = pl.program_id(1)
    tt = ti_ref.shape[1]

    @pl.when((phase == 0) & (i == 0))
    def _():
        total_scr[...] = jnp.zeros_like(total_scr)

    ti = ti_ref[...]
    e_iota = lax.broadcasted_iota(jnp.int32, (N_EXPERTS, tt), 0)
    hits = [e_iota == ti[k:k + 1, :] for k in range(TOP_K)]
    cnt = hits[0].astype(F32)
    for k in range(1, TOP_K):
        cnt = cnt + hits[k].astype(F32)
    tile_total = jnp.sum(cnt, axis=1, keepdims=True)

    @pl.when(phase == 0)
    def _():
        total_scr[...] = total_scr[...] + tile_total

    @pl.when((phase == 1) & (i == 0))
    def _():
        cnt_ref[...] = total_scr[...].astype(jnp.int32)
        run = jnp.zeros((1, LANES), F32)
        for e in range(N_EXPERTS):
            base_scr[e:e + 1, :] = run
            run = run + total_scr[e:e + 1, :]

    @pl.when(phase == 1)
    def _():
        before = _dot(cnt.astype(BF16), upper_ref[...]) + base_scr[:, 0:1]
        for k in range(TOP_K):
            dest_ref[k:k + 1, :] = jnp.sum(jnp.where(hits[k], before, 0.0), axis=0,
                                           keepdims=True).astype(jnp.int32)
        base_scr[...] = base_scr[...] + tile_total


def _rank(top_i):
    t = top_i.shape[1]
    tt = RANK_TILE
    upper = jnp.asarray(np.triu(np.ones((tt, tt), np.float32), 1), BF16)
    return pl.pallas_call(
        _rank_kernel,
        grid=(2, t // tt),
        in_specs=[pl.BlockSpec((TOP_K, tt), lambda p, i: (0, i)),
                  pl.BlockSpec((tt, tt), lambda p, i: (0, 0), pipeline_mode=pl.Buffered(1))],
        out_specs=[pl.BlockSpec((TOP_K, tt), lambda p, i: (0, i * p)),
                   pl.BlockSpec((N_EXPERTS, LANES), lambda p, i: (0, 0))],
        out_shape=[jax.ShapeDtypeStruct((TOP_K, t), jnp.int32),
                   jax.ShapeDtypeStruct((N_EXPERTS, LANES), jnp.int32)],
        scratch_shapes=[pltpu.VMEM((N_EXPERTS, LANES), F32),
                        pltpu.VMEM((N_EXPERTS, LANES), F32)],
        compiler_params=pltpu.CompilerParams(dimension_semantics=("arbitrary", "arbitrary")),
        name="rank",
    )(top_i, upper)


def _plan_kernel(cnt_ref, tile_ref, exp_ref, lo_ref, nwork_ref):
    n_items = tile_ref.shape[0]

    def per_expert(e, carry):
        w, start = carry
        c = cnt_ref[e]
        end = start + c
        t0 = start // ROW_TILE
        n = jnp.where(c > 0, (end - 1) // ROW_TILE - t0 + 1, 0)

        def per_tile(j, _):
            tile = t0 + j
            tile_ref[w + j] = tile
            exp_ref[w + j] = e
            lo_ref[w + j] = jnp.maximum(start - tile * ROW_TILE, 0)
            return 0

        lax.fori_loop(0, n, per_tile, 0)
        return w + n, end

    n_work, _ = lax.fori_loop(0, N_EXPERTS, per_expert, (jnp.int32(0), jnp.int32(0)))
    nwork_ref[0] = n_work

    def pad(w, _):
        tile_ref[w] = tile_ref[n_work - 1]
        exp_ref[w] = exp_ref[n_work - 1]
        lo_ref[w] = 0
        return 0

    lax.fori_loop(n_work, n_items, pad, 0)


def _plan(counts, n_items):
    smem = pl.BlockSpec(memory_space=pltpu.SMEM)
    vec = jax.ShapeDtypeStruct((n_items,), jnp.int32)
    return pl.pallas_call(
        _plan_kernel,
        in_specs=[smem],
        out_specs=[smem, smem, smem, smem],
        out_shape=[vec, vec, vec, jax.ShapeDtypeStruct((1,), jnp.int32)],
        name="plan",
    )(counts)


def _tile_major(a, tb):
    k, t = a.shape
    return a.reshape(k, t // tb, tb).transpose(1, 0, 2).reshape(t // tb, 1, k * tb)


def _scatter(h2, dest):
    t, d = h2.shape
    per_worker = t // SC_WORKERS
    n = SC_SCATTER_CHUNK
    mesh = plsc.VectorSubcoreMesh(core_axis_name="c", subcore_axis_name="s")

    @functools.partial(
        pl.kernel, mesh=mesh,
        out_type=jax.ShapeDtypeStruct((t * TOP_K, d), h2.dtype),
        scratch_types=[pltpu.VMEM((TOP_K, n), jnp.int32), pltpu.VMEM((n, d), h2.dtype),
                       pltpu.SemaphoreType.DMA],
        name="sc_scatter")
    def scatter_kernel(h_hbm, dest_hbm, xs_hbm, idx_v, rows_v, sem):
        wid = lax.axis_index("s") * SC_CORES + lax.axis_index("c")
        base = wid * per_worker

        @pl.loop(0, per_worker // n)
        def _(c):
            off = base + c * n
            for k in range(TOP_K):
                pltpu.sync_copy(dest_hbm.at[pl.ds(k * t + off, n)], idx_v.at[k])
            pltpu.sync_copy(h_hbm.at[pl.ds(off, n)], rows_v)
            copies = [pltpu.async_copy(rows_v, xs_hbm.at[idx_v.at[k]], sem) for k in range(TOP_K)]
            for cp in copies:
                cp.wait()

    return scatter_kernel(h2, dest.reshape(-1))


def _expert_kernel(tile_ref, exp_ref, lo_ref, nwork_ref, xs_ref, wgu_ref, bgu_ref, wd_ref, bd_ref,
                   y_ref):
    w = pl.program_id(0)

    @pl.when(w < nwork_ref[0])
    def _():
        gu = _dot(xs_ref[...].astype(BF16), wgu_ref[0]) + bgu_ref[0]
        gate = jnp.minimum(gu[:, :D_FF], SWIGLU_LIMIT)
        up = jnp.clip(gu[:, D_FF:], -SWIGLU_LIMIT, SWIGLU_LIMIT)
        act = (up + 1.0) * (gate * jax.nn.sigmoid(gate * SWIGLU_ALPHA))
        y = _dot(act.astype(BF16), wd_ref[0]) + bd_ref[0]
        lo = lo_ref[w]

        @pl.when(lo == 0)
        def _():
            y_ref[...] = y

        @pl.when(lo > 0)
        def _():
            row = lax.broadcasted_iota(jnp.int32, y.shape, 0)
            y_ref[...] = jnp.where(row >= lo, y, y_ref[...])


def _experts(xs, plan, wgu, bgu, wd, bd):
    n_rows, d = xs.shape
    w_tile, w_exp, w_lo, n_work = plan
    n_items = w_tile.shape[0]
    tm = ROW_TILE
    grid_spec = pltpu.PrefetchScalarGridSpec(
        num_scalar_prefetch=4,
        grid=(n_items,),
        in_specs=[
            pl.BlockSpec((tm, d), lambda w, tl, ex, lo, nw: (tl[w], 0)),
            pl.BlockSpec((1, d, 2 * D_FF), lambda w, tl, ex, lo, nw: (ex[w], 0, 0)),
            pl.BlockSpec((1, 1, 2 * D_FF), lambda w, tl, ex, lo, nw: (ex[w], 0, 0)),
            pl.BlockSpec((1, D_FF, d), lambda w, tl, ex, lo, nw: (ex[w], 0, 0)),
            pl.BlockSpec((1, 1, d), lambda w, tl, ex, lo, nw: (ex[w], 0, 0)),
        ],
        out_specs=pl.BlockSpec((tm, d), lambda w, tl, ex, lo, nw: (tl[w], 0)),
    )
    return pl.pallas_call(
        _expert_kernel,
        grid_spec=grid_spec,
        out_shape=jax.ShapeDtypeStruct((n_rows, d), F32),
        compiler_params=pltpu.CompilerParams(
            dimension_semantics=("arbitrary",),
            vmem_limit_bytes=VMEM_BYTES_V7X * 3 // 4),
        name="experts",
    )(w_tile, w_exp, w_lo, n_work, xs, wgu, bgu, wd, bd)


def _combine_kernel(dest_ref, x1_ref, pt_ref, y_hbm, out_ref, rows_scr, sem):
    tb = x1_ref.shape[0]

    def row_copy(r, k):
        d = dest_ref[0, 0, k * tb + r]
        return pltpu.make_async_copy(y_hbm.at[pl.ds(d, 1)], rows_scr.at[k, pl.ds(r, 1)], sem)

    def issue(r, carry):
        for k in range(TOP_K):
            row_copy(r, k).start()
        return carry

    lax.fori_loop(0, tb, issue, 0)

    def drain(r, carry):
        for k in range(TOP_K):
            row_copy(r, k).wait()
        return carry

    lax.fori_loop(0, tb, drain, 0)

    acc = rows_scr[0] * pt_ref[:, 0:1]
    for k in range(1, TOP_K):
        acc = acc + rows_scr[k] * pt_ref[:, k:k + 1]
    out_ref[...] = x1_ref[...] + acc


def _combine(x1, probs_t, dest, y):
    t, d = x1.shape
    tb = COMBINE_TILE
    return pl.pallas_call(
        _combine_kernel,
        grid=(t // tb,),
        in_specs=[pl.BlockSpec((1, 1, TOP_K * tb), lambda i: (i, 0, 0), memory_space=pltpu.SMEM),
                  pl.BlockSpec((tb, d), lambda i: (i, 0)),
                  pl.BlockSpec((tb, TOP_K), lambda i: (i, 0)),
                  pl.BlockSpec(memory_space=pl.ANY)],
        out_specs=pl.BlockSpec((tb, d), lambda i: (i, 0)),
        out_shape=jax.ShapeDtypeStruct((t, d), F32),
        scratch_shapes=[pltpu.VMEM((TOP_K, tb, d), F32), pltpu.SemaphoreType.DMA],
        compiler_params=pltpu.CompilerParams(dimension_semantics=("arbitrary",)),
        name="combine",
    )(_tile_major(dest, tb), x1, probs_t, y)


def _layer(x, norm1_g, w_in, q_norm_g, k_norm_g, rel_bias, sg_ln_g, sg_ln_b, sg_w, sg_b,
           w_branch_a, w_branch_b, w_out, norm2_g, router_w, router_b,
           w_gate_up, b_gate_up, w_down, b_down):
    bsz, seq, d = x.shape
    t = bsz * seq
    row = lambda v: v.reshape(1, -1).astype(F32)
    scale = 1.0 / math.sqrt(HEAD_DIM)
    head_sum = jnp.asarray(np.kron(np.eye(N_HEADS), np.ones((HEAD_DIM, HEAD_DIM))), BF16)
    sgb = jnp.repeat(sg_b.T.astype(F32), SG_WIDTH // SG_GROUPS, axis=1)

    x1, h2, top_i, probs = _mixer(
        x, row(norm1_g), w_in.astype(BF16),
        row(jnp.tile(q_norm_g, N_HEADS)) * scale, row(jnp.tile(k_norm_g, N_HEADS)),
        head_sum, _attention_bias(rel_bias), row(sg_ln_g), row(sg_ln_b),
        sg_w.astype(F32), sgb, w_branch_a.astype(BF16), w_branch_b.astype(BF16),
        w_out.astype(BF16), row(norm2_g), router_w.T.astype(F32),
        router_b.reshape(N_EXPERTS, 1).astype(F32))
    x1 = x1.reshape(t, d)
    h2 = h2.reshape(t, d)

    dest, counts = _rank(top_i)
    n_items = t * TOP_K // ROW_TILE + N_EXPERTS
    plan = _plan(counts[:, 0], n_items)
    xs = _scatter(h2, dest)
    y = _experts(xs, plan, w_gate_up.astype(BF16), b_gate_up.reshape(N_EXPERTS, 1, -1).astype(F32),
                 w_down.astype(BF16), b_down.reshape(N_EXPERTS, 1, -1).astype(F32))
    out = _combine(x1, probs.T, dest, y)
    return out.reshape(bsz, seq, d)


def kernel(x, norm1_g, w_in, q_norm_g, k_norm_g, rel_bias, sg_ln_g, sg_ln_b, sg_w, sg_b,
           w_branch_a, w_branch_b, w_out, norm2_g, router_w, router_b,
           w_gate_up, b_gate_up, w_down, b_down):
    depth = norm1_g.shape[0]
    for l in range(depth):
        x = _layer(x, norm1_g[l], w_in[l], q_norm_g[l], k_norm_g[l], rel_bias[l], sg_ln_g[l],
                   sg_ln_b[l], sg_w[l], sg_b[l], w_branch_a[l], w_branch_b[l], w_out[l],
                   norm2_g[l], router_w[l], router_b[l], w_gate_up[l], b_gate_up[l],
                   w_down[l], b_down[l])
    return x
```

```python
import functools
import math

import numpy as np
import jax
import jax.numpy as jnp
from jax import lax
from jax.experimental import pallas as pl
from jax.experimental.pallas import tpu as pltpu
from jax.experimental.pallas import tpu_sc as plsc

F32 = jnp.float32
BF16 = jnp.bfloat16

D_MODEL = 1024
CHUNK = 64
N_BACK = 8
BAND = (N_BACK + 1) * CHUNK
N_HEADS = 8
HEAD_DIM = 64
ATTN_WIDTH = N_HEADS * HEAD_DIM
REL_CLIP = 256
SG_BLOCK = 128
SG_GROUPS = 4
SG_WIDTH = 512
IN_WIDTH = 3 * ATTN_WIDTH + 2 * SG_WIDTH + 2 * D_MODEL
N_EXPERTS = 32
TOP_K = 4
D_FF = D_MODEL
SWIGLU_LIMIT = 7.0
SWIGLU_ALPHA = 1.702
EPS = 1e-6
NEG = -1e30

LANES = 128
VMEM_BYTES_V7X = 64 * 1024 * 1024

SEQ_TILE = 512
Q_BLOCK = 2 * CHUNK
PREV = N_BACK * CHUNK
WIN = PREV + Q_BLOCK
RANK_TILE = 512
ROW_TILE = 512

SC_CORES = 2
SC_SUBCORES = 16
SC_LANES = 16
SC_WORKERS = SC_CORES * SC_SUBCORES
SC_SCATTER_CHUNK = 32
SC_COMBINE_CHUNK = 16

_C_Q = 0
_C_K = ATTN_WIDTH
_C_V = 2 * ATTN_WIDTH
_C_U = 3 * ATTN_WIDTH
_C_VS = _C_U + SG_WIDTH
_C_GA = _C_VS + SG_WIDTH
_C_GB = _C_GA + D_MODEL


def _dot(a, b):
    return jnp.dot(a, b, preferred_element_type=F32)


def _dot_nt(a, b, precision=None):
    return lax.dot_general(a, b, (((1,), (1,)), ((), ())),
                           preferred_element_type=F32, precision=precision)


def _mixer_kernel(x_ref, g1_ref, win_ref, qg_ref, kg_ref, hsum_ref, bias_ref, lng_ref, lnb_ref,
                  sgw_ref, sgb_ref, wa_ref, wb_ref, wo_ref, g2_ref, rwt_ref, rb_ref,
                  x1_ref, h2_ref, ti_ref, pr_ref,
                  h_scr, qe_scr, qo_scr, k_win, v_win, ya_scr, ysg_scr):
    ts = x_ref.shape[1]
    s_idx = pl.program_id(1)

    @pl.when(s_idx == 0)
    def _():
        k_win[0:PREV, :] = jnp.zeros((PREV, ATTN_WIDTH), BF16)
        v_win[0:PREV, :] = jnp.zeros((PREV, ATTN_WIDTH), BF16)

    @pl.when(s_idx > 0)
    def _():
        k_win[0:PREV, :] = k_win[ts:ts + PREV, :]
        v_win[0:PREV, :] = v_win[ts:ts + PREV, :]

    x = x_ref[0]
    ms = jnp.mean(x * x, axis=-1, keepdims=True)
    h_scr[...] = (x * lax.rsqrt(ms + EPS) * g1_ref[...]).astype(BF16)

    def proj(lo, width):
        return _dot(h_scr[...], win_ref[:, lo:lo + width])

    def head_rms(t, g_ref):
        ssq = _dot((t * t).astype(BF16), hsum_ref[...])
        return t * lax.rsqrt(ssq * (1.0 / HEAD_DIM) + EPS) * g_ref[...]

    lane = lax.broadcasted_iota(jnp.int32, (1, ATTN_WIDTH), 1)
    even_head = (lane % (2 * HEAD_DIM)) < HEAD_DIM
    qn = head_rms(proj(_C_Q, ATTN_WIDTH), qg_ref)
    qe_scr[...] = jnp.where(even_head, qn, 0.0).astype(BF16)
    qo_scr[...] = jnp.where(even_head, 0.0, qn).astype(BF16)
    k_win[PREV:PREV + ts, :] = head_rms(proj(_C_K, ATTN_WIDTH), kg_ref).astype(BF16)
    v_win[PREV:PREV + ts, :] = proj(_C_V, ATTN_WIDTH).astype(BF16)

    u = jax.nn.gelu(proj(_C_U, SG_WIDTH))
    vs = jax.nn.gelu(proj(_C_VS, SG_WIDTH))
    r_i = lax.broadcasted_iota(jnp.int32, (SG_BLOCK, SG_BLOCK), 0)
    c_i = lax.broadcasted_iota(jnp.int32, (SG_BLOCK, SG_BLOCK), 1)
    tril = r_i >= c_i
    gdim = SG_WIDTH // SG_GROUPS
    for g in range(SG_GROUPS):
        cs = slice(g * gdim, (g + 1) * gdim)
        vg = vs[:, cs]
        mu = jnp.mean(vg, axis=-1, keepdims=True)
        xc = vg - mu
        var = jnp.mean(xc * xc, axis=-1, keepdims=True)
        vn = (xc * lax.rsqrt(var + EPS) * lng_ref[:, cs] + lnb_ref[:, cs]).astype(BF16)
        wm = jnp.where(tril, sgw_ref[g], 0.0).astype(BF16)
        for j in range(ts // SG_BLOCK):
            rs = slice(j * SG_BLOCK, (j + 1) * SG_BLOCK)
            mixed = _dot(wm, vn[rs]) + sgb_ref[:, cs]
            ysg_scr[rs, cs] = (u[rs, cs] * mixed).astype(BF16)

    col = lax.broadcasted_iota(jnp.int32, (1, WIN), 1)
    out_lane = lax.broadcasted_iota(jnp.int32, (1, 2 * HEAD_DIM), 1)
    first_half = out_lane < HEAD_DIM

    def attn_block(qb, carry):
        r0 = pl.multiple_of(qb * Q_BLOCK, Q_BLOCK)
        valid = col >= (PREV - (s_idx * ts + r0))
        for pair in range(N_HEADS // 2):
            cs = slice(pair * 2 * HEAD_DIM, (pair + 1) * 2 * HEAD_DIM)
            kb = k_win[pl.ds(r0, WIN), cs]
            vb = v_win[pl.ds(r0, WIN), cs]
            outs = []
            for hh, q_scr in enumerate((qe_scr, qo_scr)):
                s = _dot_nt(q_scr[pl.ds(r0, Q_BLOCK), cs], kb) + bias_ref[2 * pair + hh]
                s = jnp.where(valid, s, NEG)
                m = jnp.max(s, axis=-1, keepdims=True)
                p = jnp.exp(s - m)
                l = jnp.sum(p, axis=-1, keepdims=True)
                outs.append(_dot(p.astype(BF16), vb) / l)
            ya_scr[pl.ds(r0, Q_BLOCK), cs] = jnp.where(first_half, outs[0], outs[1]).astype(BF16)
        return carry

    lax.fori_loop(0, ts // Q_BLOCK, attn_block, 0)

    merged = jax.nn.sigmoid(proj(_C_GA, D_MODEL)) * _dot(ya_scr[...], wa_ref[...])
    merged = merged + jax.nn.sigmoid(proj(_C_GB, D_MODEL)) * _dot(ysg_scr[...], wb_ref[...])
    x1 = x_ref[0] + _dot(merged.astype(BF16), wo_ref[...])
    x1_ref[0] = x1

    ms2 = jnp.mean(x1 * x1, axis=-1, keepdims=True)
    h2 = x1 * lax.rsqrt(ms2 + EPS) * g2_ref[...]
    h2_ref[0] = h2
    lt = _dot_nt(rwt_ref[...], h2, precision=lax.Precision.HIGHEST) + rb_ref[...]
    e_iota = lax.broadcasted_iota(jnp.int32, lt.shape, 0)
    vals = []
    for k in range(TOP_K):
        m = jnp.max(lt, axis=0, keepdims=True)
        idx = jnp.min(jnp.where(lt == m, e_iota, N_EXPERTS), axis=0, keepdims=True)
        vals.append(m)
        ti_ref[k:k + 1, :] = idx
        lt = jnp.where(e_iota == idx, -jnp.inf, lt)
    exps = [jnp.exp(v - vals[0]) for v in vals]
    denom = exps[0] + exps[1] + exps[2] + exps[3]
    for k in range(TOP_K):
        pr_ref[k:k + 1, :] = exps[k] / denom


def _const_spec(shape):
    zeros = (0,) * len(shape)
    return pl.BlockSpec(shape, lambda b, s: zeros, pipeline_mode=pl.Buffered(1))


def _mixer(x, g1, win, qg, kg, hsum, bias, lng, lnb, sgw, sgb, wa, wb, wo, g2, rwt, rb):
    bsz, seq, d = x.shape
    ts = SEQ_TILE
    ns = seq // ts
    t = bsz * seq
    consts = (g1, win, qg, kg, hsum, bias, lng, lnb, sgw, sgb, wa, wb, wo, g2, rwt, rb)
    tok_spec = pl.BlockSpec((1, ts, d), lambda b, s: (b, s, 0))
    idx_spec = pl.BlockSpec((TOP_K, ts), lambda b, s: (0, b * ns + s))
    return pl.pallas_call(
        _mixer_kernel,
        grid=(bsz, ns),
        in_specs=[tok_spec] + [_const_spec(c.shape) for c in consts],
        out_specs=[tok_spec, tok_spec, idx_spec, idx_spec],
        out_shape=[
            jax.ShapeDtypeStruct((bsz, seq, d), F32),
            jax.ShapeDtypeStruct((bsz, seq, d), F32),
            jax.ShapeDtypeStruct((TOP_K, t), jnp.int32),
            jax.ShapeDtypeStruct((TOP_K, t), F32),
        ],
        scratch_shapes=[
            pltpu.VMEM((ts, d), BF16),
            pltpu.VMEM((ts, ATTN_WIDTH), BF16),
            pltpu.VMEM((ts, ATTN_WIDTH), BF16),
            pltpu.VMEM((PREV + ts, ATTN_WIDTH), BF16),
            pltpu.VMEM((PREV + ts, ATTN_WIDTH), BF16),
            pltpu.VMEM((ts, ATTN_WIDTH), BF16),
            pltpu.VMEM((ts, SG_WIDTH), BF16),
        ],
        compiler_params=pltpu.CompilerParams(
            dimension_semantics=("arbitrary", "arbitrary"),
            vmem_limit_bytes=VMEM_BYTES_V7X * 7 // 8),
        name="mixer",
    )(x, *consts)


def _attention_bias(rel_bias):
    i = np.arange(Q_BLOCK)[:, None]
    j = np.arange(WIN)[None, :]
    kk = j - (i // CHUNK) * CHUNK
    inside = (kk >= 0) & (kk < BAND)
    tab = rel_bias.astype(F32)
    far = PREV + Q_BLOCK - 1 - REL_CLIP + 1
    near = Q_BLOCK - 1 + REL_CLIP
    ext = jnp.concatenate(
        [jnp.broadcast_to(tab[:, 2 * REL_CLIP:], (N_HEADS, far)),
         tab[:, 2 * REL_CLIP - near:2 * REL_CLIP][:, ::-1]], axis=1)
    rows = [ext[:, Q_BLOCK - 1 - r:Q_BLOCK - 1 - r + WIN] for r in range(Q_BLOCK)]
    return jnp.where(inside[None], jnp.stack(rows, axis=1), NEG)


def _rank_kernel(ti_ref, upper_ref, dest_ref, cnt_ref, total_scr, base_scr):
    phase = pl.program_id(0)
    i = pl.program_id(1)
    tt = ti_ref.shape[1]

    @pl.when((phase == 0) & (i == 0))
    def _():
        total_scr[...] = jnp.zeros_like(total_scr)

    ti = ti_ref[...]
    e_iota = lax.broadcasted_iota(jnp.int32, (N_EXPERTS, tt), 0)
    hits = [e_iota == ti[k:k + 1, :] for k in range(TOP_K)]
    cnt = hits[0].astype(F32)
    for k in range(1, TOP_K):
        cnt = cnt + hits[k].astype(F32)
    tile_total = jnp.sum(cnt, axis=1, keepdims=True)

    @pl.when(phase == 0)
    def _():
        total_scr[...] = total_scr[...] + tile_total

    @pl.when((phase == 1) & (i == 0))
    def _():
        cnt_ref[...] = total_scr[...].astype(jnp.int32)
        run = jnp.zeros((1, LANES), F32)
        for e in range(N_EXPERTS):
            base_scr[e:e + 1, :] = run
            run = run + total_scr[e:e + 1, :]

    @pl.when(phase == 1)
    def _():
        before = _dot(cnt.astype(BF16), upper_ref[...]) + base_scr[:, 0:1]
        for k in range(TOP_K):
            dest_ref[k:k + 1, :] = jnp.sum(jnp.where(hits[k], before, 0.0), axis=0,
                                           keepdims=True).astype(jnp.int32)
        base_scr[...] = base_scr[...] + tile_total


def _rank(top_i):
    t = top_i.shape[1]
    tt = RANK_TILE
    upper = jnp.asarray(np.triu(np.ones((tt, tt), np.float32), 1), BF16)
    return pl.pallas_call(
        _rank_kernel,
        grid=(2, t // tt),
        in_specs=[pl.BlockSpec((TOP_K, tt), lambda p, i: (0, i)),
                  pl.BlockSpec((tt, tt), lambda p, i: (0, 0), pipeline_mode=pl.Buffered(1))],
        out_specs=[pl.BlockSpec((TOP_K, tt), lambda p, i: (0, i * p)),
                   pl.BlockSpec((N_EXPERTS, LANES), lambda p, i: (0, 0))],
        out_shape=[jax.ShapeDtypeStruct((TOP_K, t), jnp.int32),
                   jax.ShapeDtypeStruct((N_EXPERTS, LANES), jnp.int32)],
        scratch_shapes=[pltpu.VMEM((N_EXPERTS, LANES), F32),
                        pltpu.VMEM((N_EXPERTS, LANES), F32)],
        compiler_params=pltpu.CompilerParams(dimension_semantics=("arbitrary", "arbitrary")),
        name="rank",
    )(top_i, upper)


def _plan_kernel(cnt_ref, tile_ref, exp_ref, lo_ref, nwork_ref):
    n_items = tile_ref.shape[0]

    def per_expert(e, carry):
        w, start = carry
        c = cnt_ref[e]
        end = start + c
        t0 = start // ROW_TILE
        n = jnp.where(c > 0, (end - 1) // ROW_TILE - t0 + 1, 0)

        def per_tile(j, _):
            tile = t0 + j
            tile_ref[w + j] = tile
            exp_ref[w + j] = e
            lo_ref[w + j] = jnp.maximum(start - tile * ROW_TILE, 0)
            return 0

        lax.fori_loop(0, n, per_tile, 0)
        return w + n, end

    n_work, _ = lax.fori_loop(0, N_EXPERTS, per_expert, (jnp.int32(0), jnp.int32(0)))
    nwork_ref[0] = n_work

    def pad(w, _):
        tile_ref[w] = tile_ref[n_work - 1]
        exp_ref[w] = exp_ref[n_work - 1]
        lo_ref[w] = 0
        return 0

    lax.fori_loop(n_work, n_items, pad, 0)


def _plan(counts, n_items):
    smem = pl.BlockSpec(memory_space=pltpu.SMEM)
    vec = jax.ShapeDtypeStruct((n_items,), jnp.int32)
    return pl.pallas_call(
        _plan_kernel,
        in_specs=[smem],
        out_specs=[smem, smem, smem, smem],
        out_shape=[vec, vec, vec, jax.ShapeDtypeStruct((1,), jnp.int32)],
        name="plan",
    )(counts)


def _scatter(h2, dest):
    t, d = h2.shape
    per_worker = t // SC_WORKERS
    n = SC_SCATTER_CHUNK
    mesh = plsc.VectorSubcoreMesh(core_axis_name="c", subcore_axis_name="s")

    @functools.partial(
        pl.kernel, mesh=mesh,
        out_type=jax.ShapeDtypeStruct((t * TOP_K, d), h2.dtype),
        scratch_types=[pltpu.VMEM((TOP_K, n), jnp.int32), pltpu.VMEM((n, d), h2.dtype),
                       pltpu.SemaphoreType.DMA],
        name="sc_scatter")
    def scatter_kernel(h_hbm, dest_hbm, xs_hbm, idx_v, rows_v, sem):
        wid = lax.axis_index("s") * SC_CORES + lax.axis_index("c")
        base = wid * per_worker

        @pl.loop(0, per_worker // n)
        def _(c):
            off = base + c * n
            for k in range(TOP_K):
                pltpu.sync_copy(dest_hbm.at[pl.ds(k * t + off, n)], idx_v.at[k])
            pltpu.sync_copy(h_hbm.at[pl.ds(off, n)], rows_v)
            copies = [pltpu.async_copy(rows_v, xs_hbm.at[idx_v.at[k]], sem) for k in range(TOP_K)]
            for cp in copies:
                cp.wait()

    return scatter_kernel(h2, dest.reshape(-1))


def _expert_kernel(tile_ref, exp_ref, lo_ref, nwork_ref, xs_ref, wgu_ref, bgu_ref, wd_ref, bd_ref,
                   y_ref):
    w = pl.program_id(0)

    @pl.when(w < nwork_ref[0])
    def _():
        gu = _dot(xs_ref[...].astype(BF16), wgu_ref[0]) + bgu_ref[0]
        gate = jnp.minimum(gu[:, :D_FF], SWIGLU_LIMIT)
        up = jnp.clip(gu[:, D_FF:], -SWIGLU_LIMIT, SWIGLU_LIMIT)
        act = (up + 1.0) * (gate * jax.nn.sigmoid(gate * SWIGLU_ALPHA))
        y = _dot(act.astype(BF16), wd_ref[0]) + bd_ref[0]
        lo = lo_ref[w]

        @pl.when(lo == 0)
        def _():
            y_ref[...] = y

        @pl.when(lo > 0)
        def _():
            row = lax.broadcasted_iota(jnp.int32, y.shape, 0)
            y_ref[...] = jnp.where(row >= lo, y, y_ref[...])


def _experts(xs, plan, wgu, bgu, wd, bd):
    n_rows, d = xs.shape
    w_tile, w_exp, w_lo, n_work = plan
    n_items = w_tile.shape[0]
    tm = ROW_TILE
    grid_spec = pltpu.PrefetchScalarGridSpec(
        num_scalar_prefetch=4,
        grid=(n_items,),
        in_specs=[
            pl.BlockSpec((tm, d), lambda w, tl, ex, lo, nw: (tl[w], 0)),
            pl.BlockSpec((1, d, 2 * D_FF), lambda w, tl, ex, lo, nw: (ex[w], 0, 0)),
            pl.BlockSpec((1, 1, 2 * D_FF), lambda w, tl, ex, lo, nw: (ex[w], 0, 0)),
            pl.BlockSpec((1, D_FF, d), lambda w, tl, ex, lo, nw: (ex[w], 0, 0)),
            pl.BlockSpec((1, 1, d), lambda w, tl, ex, lo, nw: (ex[w], 0, 0)),
        ],
        out_specs=pl.BlockSpec((tm, d), lambda w, tl, ex, lo, nw: (tl[w], 0)),
    )
    return pl.pallas_call(
        _expert_kernel,
        grid_spec=grid_spec,
        out_shape=jax.ShapeDtypeStruct((n_rows, d), F32),
        compiler_params=pltpu.CompilerParams(
            dimension_semantics=("arbitrary",),
            vmem_limit_bytes=VMEM_BYTES_V7X * 3 // 4),
        name="experts",
    )(w_tile, w_exp, w_lo, n_work, xs, wgu, bgu, wd, bd)


def _combine(x1, probs, dest, y):
    t, d = x1.shape
    per_worker = t // SC_WORKERS
    n = SC_COMBINE_CHUNK
    probs_e = jnp.broadcast_to(probs[:, :, None], (TOP_K, t, SC_LANES))
    mesh = plsc.VectorSubcoreMesh(core_axis_name="c", subcore_axis_name="s")

    @functools.partial(
        pl.kernel, mesh=mesh,
        out_type=jax.ShapeDtypeStruct((t, d), F32),
        scratch_types=[pltpu.VMEM((TOP_K, n), jnp.int32), pltpu.VMEM((TOP_K, n, d), F32),
                       pltpu.VMEM((n, d), F32), pltpu.VMEM((TOP_K, n, SC_LANES), F32),
                       pltpu.SemaphoreType.DMA],
        name="sc_combine")
    def combine_kernel(x1_hbm, p_hbm, dest_hbm, y_hbm, out_hbm, idx_v, rows_v, x_v, p_v, sem):
        wid = lax.axis_index("s") * SC_CORES + lax.axis_index("c")
        base = wid * per_worker

        @pl.loop(0, per_worker // n)
        def _(c):
            off = base + c * n
            for k in range(TOP_K):
                pltpu.sync_copy(dest_hbm.at[pl.ds(k * t + off, n)], idx_v.at[k])
            gathers = [pltpu.async_copy(y_hbm.at[idx_v.at[k]], rows_v.at[k], sem)
                       for k in range(TOP_K)]
            pltpu.sync_copy(x1_hbm.at[pl.ds(off, n)], x_v)
            for k in range(TOP_K):
                pltpu.sync_copy(p_hbm.at[k, pl.ds(off, n)], p_v.at[k])
            for g in gathers:
                g.wait()

            @pl.loop(0, n)
            def _(r):
                w = [p_v[k, r, :] for k in range(TOP_K)]

                @pl.loop(0, d, step=SC_LANES)
                def _(c0):
                    sl = pl.ds(c0, SC_LANES)
                    acc = x_v[r, sl]
                    for k in range(TOP_K):
                        acc = acc + w[k] * rows_v[k, r, sl]
                    x_v[r, sl] = acc

            pltpu.sync_copy(x_v, out_hbm.at[pl.ds(off, n)])

    return combine_kernel(x1, probs_e, dest.reshape(-1), y)


def _layer(x, norm1_g, w_in, q_norm_g, k_norm_g, rel_bias, sg_ln_g, sg_ln_b, sg_w, sg_b,
           w_branch_a, w_branch_b, w_out, norm2_g, router_w, router_b,
           w_gate_up, b_gate_up, w_down, b_down):
    bsz, seq, d = x.shape
    t = bsz * seq
    row = lambda v: v.reshape(1, -1).astype(F32)
    scale = 1.0 / math.sqrt(HEAD_DIM)
    head_sum = jnp.asarray(np.kron(np.eye(N_HEADS), np.ones((HEAD_DIM, HEAD_DIM))), BF16)
    sgb = jnp.repeat(sg_b.T.astype(F32), SG_WIDTH // SG_GROUPS, axis=1)

    x1, h2, top_i, probs = _mixer(
        x, row(norm1_g), w_in.astype(BF16),
        row(jnp.tile(q_norm_g, N_HEADS)) * scale, row(jnp.tile(k_norm_g, N_HEADS)),
        head_sum, _attention_bias(rel_bias), row(sg_ln_g), row(sg_ln_b),
        sg_w.astype(F32), sgb, w_branch_a.astype(BF16), w_branch_b.astype(BF16),
        w_out.astype(BF16), row(norm2_g), router_w.T.astype(F32),
        router_b.reshape(N_EXPERTS, 1).astype(F32))
    x1 = x1.reshape(t, d)
    h2 = h2.reshape(t, d)

    dest, counts = _rank(top_i)
    n_items = t * TOP_K // ROW_TILE + N_EXPERTS
    plan = _plan(counts[:, 0], n_items)
    xs = _scatter(h2, dest)
    y = _experts(xs, plan, w_gate_up.astype(BF16), b_gate_up.reshape(N_EXPERTS, 1, -1).astype(F32),
                 w_down.astype(BF16), b_down.reshape(N_EXPERTS, 1, -1).astype(F32))
    out = _combine(x1, probs, dest, y)
    return out.reshape(bsz, seq, d)


def kernel(x, norm1_g, w_in, q_norm_g, k_norm_g, rel_bias, sg_ln_g, sg_ln_b, sg_w, sg_b,
           w_branch_a, w_branch_b, w_out, norm2_g, router_w, router_b,
           w_gate_up, b_gate_up, w_down, b_down):
    depth = norm1_g.shape[0]
    for l in range(depth):
        x = _layer(x, norm1_g[l], w_in[l], q_norm_g[l], k_norm_g[l], rel_bias[l], sg_ln_g[l],
                   sg_ln_b[l], sg_w[l], sg_b[l], w_branch_a[l], w_branch_b[l], w_out[l],
                   norm2_g[l], router_w[l], router_b[l], w_gate_up[l], b_gate_up[l],
                   w_down[l], b_down[l])
    return x
```

```python
import functools
import math

import numpy as np
import jax
import jax.numpy as jnp
from jax import lax
from jax.experimental import pallas as pl
from jax.experimental.pallas import tpu as pltpu
from jax.experimental.pallas import tpu_sc as plsc

F32 = jnp.float32
BF16 = jnp.bfloat16

D_MODEL = 1024
CHUNK = 64
N_BACK = 8
BAND = (N_BACK + 1) * CHUNK
N_HEADS = 8
HEAD_DIM = 64
ATTN_WIDTH = N_HEADS * HEAD_DIM
REL_CLIP = 256
SG_BLOCK = 128
SG_GROUPS = 4
SG_WIDTH = 512
IN_WIDTH = 3 * ATTN_WIDTH + 2 * SG_WIDTH + 2 * D_MODEL
N_EXPERTS = 32
TOP_K = 4
D_FF = D_MODEL
SWIGLU_LIMIT = 7.0
SWIGLU_ALPHA = 1.702
EPS = 1e-6
NEG = -1e30

LANES = 128
VMEM_BYTES_V7X = 64 * 1024 * 1024

SEQ_TILE = 512
Q_BLOCK = 2 * CHUNK
PREV = N_BACK * CHUNK
WIN = PREV + Q_BLOCK
RANK_TILE = 512
ROW_TILE = 512

SC_CORES = 2
SC_SUBCORES = 16
SC_LANES = 16
SC_WORKERS = SC_CORES * SC_SUBCORES
SC_SCATTER_CHUNK = 64
SC_COMBINE_CHUNK = 8

_C_Q = 0
_C_K = ATTN_WIDTH
_C_V = 2 * ATTN_WIDTH
_C_U = 3 * ATTN_WIDTH
_C_VS = _C_U + SG_WIDTH
_C_GA = _C_VS + SG_WIDTH
_C_GB = _C_GA + D_MODEL


def _dot(a, b):
    return jnp.dot(a, b, preferred_element_type=F32)


def _pack_bf16_pairs(v):
    w = v.shape[1] // 2
    lo = lax.bitcast_convert_type(v[:, :w].astype(BF16).astype(F32), jnp.uint32)
    hi = lax.bitcast_convert_type(v[:, w:].astype(BF16).astype(F32), jnp.uint32)
    return lax.bitcast_convert_type((lo >> 16) | hi, jnp.int32)


def _unpack_bf16_pairs(p):
    u = lax.bitcast_convert_type(p, jnp.uint32)
    lo = lax.bitcast_convert_type(u << 16, F32)
    hi = lax.bitcast_convert_type(u & jnp.uint32(0xFFFF0000), F32)
    return jnp.concatenate([lo, hi], axis=1).astype(BF16)


def _dot_nt(a, b, precision=None):
    return lax.dot_general(a, b, (((1,), (1,)), ((), ())),
                           preferred_element_type=F32, precision=precision)


def _mixer_kernel(x_ref, g1_ref, win_ref, qg_ref, kg_ref, hsum_ref, bias_ref, lng_ref, lnb_ref,
                  sgw_ref, sgb_ref, wa_ref, wb_ref, wo_ref, g2_ref, rwt_ref, rb_ref,
                  x1_ref, h2_ref, ti_ref, pr_ref,
                  h_scr, qe_scr, qo_scr, k_win, v_win, ya_scr, ysg_scr):
    ts = x_ref.shape[1]
    s_idx = pl.program_id(1)

    @pl.when(s_idx == 0)
    def _():
        k_win[0:PREV, :] = jnp.zeros((PREV, ATTN_WIDTH), BF16)
        v_win[0:PREV, :] = jnp.zeros((PREV, ATTN_WIDTH), BF16)

    @pl.when(s_idx > 0)
    def _():
        k_win[0:PREV, :] = k_win[ts:ts + PREV, :]
        v_win[0:PREV, :] = v_win[ts:ts + PREV, :]

    x = x_ref[0]
    ms = jnp.mean(x * x, axis=-1, keepdims=True)
    h_scr[...] = (x * lax.rsqrt(ms + EPS) * g1_ref[...]).astype(BF16)

    def proj(lo, width):
        return _dot(h_scr[...], win_ref[:, lo:lo + width])

    def head_rms(t, g_ref):
        ssq = _dot((t * t).astype(BF16), hsum_ref[...])
        return t * lax.rsqrt(ssq * (1.0 / HEAD_DIM) + EPS) * g_ref[...]

    lane = lax.broadcasted_iota(jnp.int32, (1, ATTN_WIDTH), 1)
    even_head = (lane % (2 * HEAD_DIM)) < HEAD_DIM
    qn = head_rms(proj(_C_Q, ATTN_WIDTH), qg_ref)
    qe_scr[...] = jnp.where(even_head, qn, 0.0).astype(BF16)
    qo_scr[...] = jnp.where(even_head, 0.0, qn).astype(BF16)
    k_win[PREV:PREV + ts, :] = head_rms(proj(_C_K, ATTN_WIDTH), kg_ref).astype(BF16)
    v_win[PREV:PREV + ts, :] = proj(_C_V, ATTN_WIDTH).astype(BF16)

    u = jax.nn.gelu(proj(_C_U, SG_WIDTH))
    vs = jax.nn.gelu(proj(_C_VS, SG_WIDTH))
    r_i = lax.broadcasted_iota(jnp.int32, (SG_BLOCK, SG_BLOCK), 0)
    c_i = lax.broadcasted_iota(jnp.int32, (SG_BLOCK, SG_BLOCK), 1)
    tril = r_i >= c_i
    gdim = SG_WIDTH // SG_GROUPS
    for g in range(SG_GROUPS):
        cs = slice(g * gdim, (g + 1) * gdim)
        vg = vs[:, cs]
        mu = jnp.mean(vg, axis=-1, keepdims=True)
        xc = vg - mu
        var = jnp.mean(xc * xc, axis=-1, keepdims=True)
        vn = (xc * lax.rsqrt(var + EPS) * lng_ref[:, cs] + lnb_ref[:, cs]).astype(BF16)
        wm = jnp.where(tril, sgw_ref[g], 0.0).astype(BF16)
        for j in range(ts // SG_BLOCK):
            rs = slice(j * SG_BLOCK, (j + 1) * SG_BLOCK)
            mixed = _dot(wm, vn[rs]) + sgb_ref[:, cs]
            ysg_scr[rs, cs] = (u[rs, cs] * mixed).astype(BF16)

    col = lax.broadcasted_iota(jnp.int32, (1, WIN), 1)
    out_lane = lax.broadcasted_iota(jnp.int32, (1, 2 * HEAD_DIM), 1)
    first_half = out_lane < HEAD_DIM

    def attn_block(qb, carry):
        r0 = pl.multiple_of(qb * Q_BLOCK, Q_BLOCK)
        valid = col >= (PREV - (s_idx * ts + r0))
        for pair in range(N_HEADS // 2):
            cs = slice(pair * 2 * HEAD_DIM, (pair + 1) * 2 * HEAD_DIM)
            kb = k_win[pl.ds(r0, WIN), cs]
            vb = v_win[pl.ds(r0, WIN), cs]
            outs = []
            for hh, q_scr in enumerate((qe_scr, qo_scr)):
                s = _dot_nt(q_scr[pl.ds(r0, Q_BLOCK), cs], kb) + bias_ref[2 * pair + hh]
                s = jnp.where(valid, s, NEG)
                m = jnp.max(s, axis=-1, keepdims=True)
                p = jnp.exp(s - m)
                l = jnp.sum(p, axis=-1, keepdims=True)
                outs.append(_dot(p.astype(BF16), vb) / l)
            ya_scr[pl.ds(r0, Q_BLOCK), cs] = jnp.where(first_half, outs[0], outs[1]).astype(BF16)
        return carry

    lax.fori_loop(0, ts // Q_BLOCK, attn_block, 0)

    merged = jax.nn.sigmoid(proj(_C_GA, D_MODEL)) * _dot(ya_scr[...], wa_ref[...])
    merged = merged + jax.nn.sigmoid(proj(_C_GB, D_MODEL)) * _dot(ysg_scr[...], wb_ref[...])
    x1 = x_ref[0] + _dot(merged.astype(BF16), wo_ref[...])
    x1_ref[0] = x1

    ms2 = jnp.mean(x1 * x1, axis=-1, keepdims=True)
    h2 = x1 * lax.rsqrt(ms2 + EPS) * g2_ref[...]
    h2_ref[0] = _pack_bf16_pairs(h2)
    lt = _dot_nt(rwt_ref[...], h2, precision=lax.Precision.HIGHEST) + rb_ref[...]
    e_iota = lax.broadcasted_iota(jnp.int32, lt.shape, 0)
    vals = []
    for k in range(TOP_K):
        m = jnp.max(lt, axis=0, keepdims=True)
        idx = jnp.min(jnp.where(lt == m, e_iota, N_EXPERTS), axis=0, keepdims=True)
        vals.append(m)
        ti_ref[k:k + 1, :] = idx
        lt = jnp.where(e_iota == idx, -jnp.inf, lt)
    exps = [jnp.exp(v - vals[0]) for v in vals]
    denom = exps[0] + exps[1] + exps[2] + exps[3]
    for k in range(TOP_K):
        pr_ref[k:k + 1, :] = exps[k] / denom


def _const_spec(shape):
    zeros = (0,) * len(shape)
    return pl.BlockSpec(shape, lambda b, s: zeros, pipeline_mode=pl.Buffered(1))


def _mixer(x, g1, win, qg, kg, hsum, bias, lng, lnb, sgw, sgb, wa, wb, wo, g2, rwt, rb):
    bsz, seq, d = x.shape
    ts = SEQ_TILE
    ns = seq // ts
    t = bsz * seq
    consts = (g1, win, qg, kg, hsum, bias, lng, lnb, sgw, sgb, wa, wb, wo, g2, rwt, rb)
    tok_spec = pl.BlockSpec((1, ts, d), lambda b, s: (b, s, 0))
    packed_spec = pl.BlockSpec((1, ts, d // 2), lambda b, s: (b, s, 0))
    idx_spec = pl.BlockSpec((TOP_K, ts), lambda b, s: (0, b * ns + s))
    return pl.pallas_call(
        _mixer_kernel,
        grid=(bsz, ns),
        in_specs=[tok_spec] + [_const_spec(c.shape) for c in consts],
        out_specs=[tok_spec, packed_spec, idx_spec, idx_spec],
        out_shape=[
            jax.ShapeDtypeStruct((bsz, seq, d), F32),
            jax.ShapeDtypeStruct((bsz, seq, d // 2), jnp.int32),
            jax.ShapeDtypeStruct((TOP_K, t), jnp.int32),
            jax.ShapeDtypeStruct((TOP_K, t), F32),
        ],
        scratch_shapes=[
            pltpu.VMEM((ts, d), BF16),
            pltpu.VMEM((ts, ATTN_WIDTH), BF16),
            pltpu.VMEM((ts, ATTN_WIDTH), BF16),
            pltpu.VMEM((PREV + ts, ATTN_WIDTH), BF16),
            pltpu.VMEM((PREV + ts, ATTN_WIDTH), BF16),
            pltpu.VMEM((ts, ATTN_WIDTH), BF16),
            pltpu.VMEM((ts, SG_WIDTH), BF16),
        ],
        compiler_params=pltpu.CompilerParams(
            dimension_semantics=("arbitrary", "arbitrary"),
            vmem_limit_bytes=VMEM_BYTES_V7X * 7 // 8),
        name="mixer",
    )(x, *consts)


def _attention_bias(rel_bias):
    i = np.arange(Q_BLOCK)[:, None]
    j = np.arange(WIN)[None, :]
    kk = j - (i // CHUNK) * CHUNK
    inside = (kk >= 0) & (kk < BAND)
    tab = rel_bias.astype(F32)
    far = PREV + Q_BLOCK - 1 - REL_CLIP + 1
    near = Q_BLOCK - 1 + REL_CLIP
    ext = jnp.concatenate(
        [jnp.broadcast_to(tab[:, 2 * REL_CLIP:], (N_HEADS, far)),
         tab[:, 2 * REL_CLIP - near:2 * REL_CLIP][:, ::-1]], axis=1)
    rows = [ext[:, Q_BLOCK - 1 - r:Q_BLOCK - 1 - r + WIN] for r in range(Q_BLOCK)]
    return jnp.where(inside[None], jnp.stack(rows, axis=1), NEG)


def _rank_kernel(ti_ref, upper_ref, dest_ref, cnt_ref, total_scr, base_scr):
    phase = pl.program_id(0)
    i = pl.program_id(1)
    tt = ti_ref.shape[1]

    @pl.when((phase == 0) & (i == 0))
    def _():
        total_scr[...] = jnp.zeros_like(total_scr)

    ti = ti_ref[...]
    e_iota = lax.broadcasted_iota(jnp.int32, (N_EXPERTS, tt), 0)
    hits = [e_iota == ti[k:k + 1, :] for k in range(TOP_K)]
    cnt = hits[0].astype(F32)
    for k in range(1, TOP_K):
        cnt = cnt + hits[k].astype(F32)
    tile_total = jnp.sum(cnt, axis=1, keepdims=True)

    @pl.when(phase == 0)
    def _():
        total_scr[...] = total_scr[...] + tile_total

    @pl.when((phase == 1) & (i == 0))
    def _():
        cnt_ref[...] = total_scr[...].astype(jnp.int32)
        run = jnp.zeros((1, LANES), F32)
        for e in range(N_EXPERTS):
            base_scr[e:e + 1, :] = run
            run = run + total_scr[e:e + 1, :]

    @pl.when(phase == 1)
    def _():
        before = _dot(cnt.astype(BF16), upper_ref[...]) + base_scr[:, 0:1]
        for k in range(TOP_K):
            dest_ref[k:k + 1, :] = jnp.sum(jnp.where(hits[k], before, 0.0), axis=0,
                                           keepdims=True).astype(jnp.int32)
        base_scr[...] = base_scr[...] + tile_total


def _rank(top_i):
    t = top_i.shape[1]
    tt = RANK_TILE
    upper = jnp.asarray(np.triu(np.ones((tt, tt), np.float32), 1), BF16)
    return pl.pallas_call(
        _rank_kernel,
        grid=(2, t // tt),
        in_specs=[pl.BlockSpec((TOP_K, tt), lambda p, i: (0, i)),
                  pl.BlockSpec((tt, tt), lambda p, i: (0, 0), pipeline_mode=pl.Buffered(1))],
        out_specs=[pl.BlockSpec((TOP_K, tt), lambda p, i: (0, i * p)),
                   pl.BlockSpec((N_EXPERTS, LANES), lambda p, i: (0, 0))],
        out_shape=[jax.ShapeDtypeStruct((TOP_K, t), jnp.int32),
                   jax.ShapeDtypeStruct((N_EXPERTS, LANES), jnp.int32)],
        scratch_shapes=[pltpu.VMEM((N_EXPERTS, LANES), F32),
                        pltpu.VMEM((N_EXPERTS, LANES), F32)],
        compiler_params=pltpu.CompilerParams(dimension_semantics=("arbitrary", "arbitrary")),
        name="rank",
    )(top_i, upper)


def _plan_kernel(cnt_ref, tile_ref, exp_ref, lo_ref, nwork_ref):
    n_items = tile_ref.shape[0]

    def per_expert(e, carry):
        w, start = carry
        c = cnt_ref[e]
        end = start + c
        t0 = start // ROW_TILE
        n = jnp.where(c > 0, (end - 1) // ROW_TILE - t0 + 1, 0)

        def per_tile(j, _):
            tile = t0 + j
            tile_ref[w + j] = tile
            exp_ref[w + j] = e
            lo_ref[w + j] = jnp.maximum(start - tile * ROW_TILE, 0)
            return 0

        lax.fori_loop(0, n, per_tile, 0)
        return w + n, end

    n_work, _ = lax.fori_loop(0, N_EXPERTS, per_expert, (jnp.int32(0), jnp.int32(0)))
    nwork_ref[0] = n_work

    def pad(w, _):
        tile_ref[w] = tile_ref[n_work - 1]
        exp_ref[w] = exp_ref[n_work - 1]
        lo_ref[w] = 0
        return 0

    lax.fori_loop(n_work, n_items, pad, 0)


def _plan(counts, n_items):
    smem = pl.BlockSpec(memory_space=pltpu.SMEM)
    vec = jax.ShapeDtypeStruct((n_items,), jnp.int32)
    return pl.pallas_call(
        _plan_kernel,
        in_specs=[smem],
        out_specs=[smem, smem, smem, smem],
        out_shape=[vec, vec, vec, jax.ShapeDtypeStruct((1,), jnp.int32)],
        name="plan",
    )(counts)


def _scatter(h2, dest):
    t, d = h2.shape
    per_worker = t // SC_WORKERS
    n = SC_SCATTER_CHUNK
    nch = per_worker // n
    assert nch % 2 == 0
    mesh = plsc.VectorSubcoreMesh(core_axis_name="c", subcore_axis_name="s")

    @functools.partial(
        pl.kernel, mesh=mesh,
        out_type=jax.ShapeDtypeStruct((t * TOP_K, d), h2.dtype),
        scratch_types=[pltpu.VMEM((TOP_K, nch, n), jnp.int32), pltpu.VMEM((2, n, d), h2.dtype),
                       pltpu.SemaphoreType.DMA((2,)), pltpu.SemaphoreType.DMA((2,))],
        name="sc_scatter")
    def scatter_kernel(h_hbm, dest_hbm, xs_hbm, idx_v, rows_v, sem_in, sem_out):
        wid = lax.axis_index("s") * SC_CORES + lax.axis_index("c")
        base = wid * per_worker
        for k in range(TOP_K):
            pltpu.sync_copy(dest_hbm.at[k, pl.ds(wid * nch, nch)], idx_v.at[k])

        def load(c, s):
            return pltpu.make_async_copy(h_hbm.at[pl.ds(base + c * n, n)], rows_v.at[s],
                                         sem_in.at[s])

        def store(c, s, k):
            return pltpu.make_async_copy(rows_v.at[s], xs_hbm.at[idx_v.at[k, c]], sem_out.at[s])

        load(0, 0).start()

        @pl.loop(0, nch, step=2)
        def _(c):
            for s in range(2):
                cc = c + s

                @pl.when(cc >= 1)
                def _():
                    for k in range(TOP_K):
                        store(cc - 1, 1 - s, k).wait()

                @pl.when(cc + 1 < nch)
                def _():
                    load(cc + 1, 1 - s).start()

                load(cc, s).wait()
                for k in range(TOP_K):
                    store(cc, s, k).start()

        for k in range(TOP_K):
            store(nch - 1, (nch - 1) % 2, k).wait()

    return scatter_kernel(h2, dest.reshape(TOP_K, t // n, n))


def _expert_kernel(tile_ref, exp_ref, lo_ref, nwork_ref, xs_ref, wgu_ref, bgu_ref, wd_ref, bd_ref,
                   y_ref):
    w = pl.program_id(0)

    @pl.when(w < nwork_ref[0])
    def _():
        gu = _dot(_unpack_bf16_pairs(xs_ref[...]), wgu_ref[0]) + bgu_ref[0]
        gate = jnp.minimum(gu[:, :D_FF], SWIGLU_LIMIT)
        up = jnp.clip(gu[:, D_FF:], -SWIGLU_LIMIT, SWIGLU_LIMIT)
        act = (up + 1.0) * (gate * jax.nn.sigmoid(gate * SWIGLU_ALPHA))
        y = _pack_bf16_pairs(_dot(act.astype(BF16), wd_ref[0]) + bd_ref[0])
        lo = lo_ref[w]

        @pl.when(lo == 0)
        def _():
            y_ref[...] = y

        @pl.when(lo > 0)
        def _():
            row = lax.broadcasted_iota(jnp.int32, y.shape, 0)
            y_ref[...] = jnp.where(row >= lo, y, y_ref[...])


def _experts(xs, plan, wgu, bgu, wd, bd):
    n_rows, dp = xs.shape
    d = 2 * dp
    w_tile, w_exp, w_lo, n_work = plan
    n_items = w_tile.shape[0]
    tm = ROW_TILE
    grid_spec = pltpu.PrefetchScalarGridSpec(
        num_scalar_prefetch=4,
        grid=(n_items,),
        in_specs=[
            pl.BlockSpec((tm, dp), lambda w, tl, ex, lo, nw: (tl[w], 0)),
            pl.BlockSpec((1, d, 2 * D_FF), lambda w, tl, ex, lo, nw: (ex[w], 0, 0)),
            pl.BlockSpec((1, 1, 2 * D_FF), lambda w, tl, ex, lo, nw: (ex[w], 0, 0)),
            pl.BlockSpec((1, D_FF, d), lambda w, tl, ex, lo, nw: (ex[w], 0, 0)),
            pl.BlockSpec((1, 1, d), lambda w, tl, ex, lo, nw: (ex[w], 0, 0)),
        ],
        out_specs=pl.BlockSpec((tm, dp), lambda w, tl, ex, lo, nw: (tl[w], 0)),
    )
    return pl.pallas_call(
        _expert_kernel,
        grid_spec=grid_spec,
        out_shape=jax.ShapeDtypeStruct((n_rows, dp), jnp.int32),
        compiler_params=pltpu.CompilerParams(
            dimension_semantics=("arbitrary",),
            vmem_limit_bytes=VMEM_BYTES_V7X * 3 // 4),
        name="experts",
    )(w_tile, w_exp, w_lo, n_work, xs, wgu, bgu, wd, bd)


def _combine(x1, probs, dest, y):
    t, d = x1.shape
    per_worker = t // SC_WORKERS
    dp = d // 2
    n = SC_COMBINE_CHUNK
    nch = per_worker // n
    assert nch % 2 == 0
    probs_e = jnp.repeat(probs.T, SC_LANES, axis=1)
    mesh = plsc.VectorSubcoreMesh(core_axis_name="c", subcore_axis_name="s")
    high_half = -65536

    @functools.partial(
        pl.kernel, mesh=mesh,
        out_type=jax.ShapeDtypeStruct((t, d), F32),
        scratch_types=[pltpu.VMEM((TOP_K, per_worker), jnp.int32),
                       pltpu.VMEM((2, TOP_K, n, dp), jnp.int32),
                       pltpu.VMEM((2, n, d), F32), pltpu.VMEM((2, n, d), F32),
                       pltpu.VMEM((2, n, TOP_K * SC_LANES), F32),
                       pltpu.SemaphoreType.DMA((2,)), pltpu.SemaphoreType.DMA((2,))],
        compiler_params=pltpu.CompilerParams(needs_layout_passes=False),
        name="sc_combine")
    def combine_kernel(x1_hbm, p_hbm, dest_hbm, y_hbm, out_hbm,
                       idx_v, rows_v, x_v, o_v, p_v, sem_in, sem_out):
        wid = lax.axis_index("s") * SC_CORES + lax.axis_index("c")
        base = wid * per_worker
        for k in range(TOP_K):
            pltpu.sync_copy(dest_hbm.at[k, pl.ds(base, per_worker)], idx_v.at[k])

        def loads(c, s):
            off = base + c * n
            cps = [pltpu.make_async_copy(y_hbm.at[idx_v.at[k, pl.ds(c * n, n)]], rows_v.at[s, k], sem_in.at[s])
                   for k in range(TOP_K)]
            cps.append(pltpu.make_async_copy(x1_hbm.at[pl.ds(off, n)], x_v.at[s], sem_in.at[s]))
            cps.append(pltpu.make_async_copy(p_hbm.at[pl.ds(off, n)], p_v.at[s], sem_in.at[s]))
            return cps

        def store(c, s):
            return pltpu.make_async_copy(o_v.at[s], out_hbm.at[pl.ds(base + c * n, n)],
                                         sem_out.at[s])

        for cp in loads(0, 0):
            cp.start()

        @pl.loop(0, nch, step=2)
        def _(c):
            for s in range(2):
                cc = c + s

                @pl.when(cc + 1 < nch)
                def _():
                    for cp in loads(cc + 1, 1 - s):
                        cp.start()

                for cp in loads(cc, s):
                    cp.wait()

                @pl.when(cc >= 2)
                def _():
                    store(cc - 2, s).wait()

                @pl.loop(0, n)
                def _(r):
                    w = [p_v[s, r, pl.ds(k * SC_LANES, SC_LANES)] for k in range(TOP_K)]
                    for g in range(dp // SC_LANES):
                        lo_sl = pl.ds(g * SC_LANES, SC_LANES)
                        hi_sl = pl.ds(dp + g * SC_LANES, SC_LANES)
                        lo = x_v[s, r, lo_sl]
                        hi = x_v[s, r, hi_sl]
                        for k in range(TOP_K):
                            word = rows_v[s, k, r, lo_sl]
                            lo = lo + w[k] * plsc.bitcast(word << 16, F32)
                            hi = hi + w[k] * plsc.bitcast(word & high_half, F32)
                        o_v[s, r, lo_sl] = lo
                        o_v[s, r, hi_sl] = hi

                store(cc, s).start()

        store(nch - 2, 0).wait()
        store(nch - 1, 1).wait()

    return combine_kernel(x1, probs_e, dest, y)


def _layer(x, norm1_g, w_in, q_norm_g, k_norm_g, rel_bias, sg_ln_g, sg_ln_b, sg_w, sg_b,
           w_branch_a, w_branch_b, w_out, norm2_g, router_w, router_b,
           w_gate_up, b_gate_up, w_down, b_down):
    bsz, seq, d = x.shape
    t = bsz * seq
    row = lambda v: v.reshape(1, -1).astype(F32)
    scale = 1.0 / math.sqrt(HEAD_DIM)
    head_sum = jnp.asarray(np.kron(np.eye(N_HEADS), np.ones((HEAD_DIM, HEAD_DIM))), BF16)
    sgb = jnp.repeat(sg_b.T.astype(F32), SG_WIDTH // SG_GROUPS, axis=1)

    x1, h2, top_i, probs = _mixer(
        x, row(norm1_g), w_in.astype(BF16),
        row(jnp.tile(q_norm_g, N_HEADS)) * scale, row(jnp.tile(k_norm_g, N_HEADS)),
        head_sum, _attention_bias(rel_bias), row(sg_ln_g), row(sg_ln_b),
        sg_w.astype(F32), sgb, w_branch_a.astype(BF16), w_branch_b.astype(BF16),
        w_out.astype(BF16), row(norm2_g), router_w.T.astype(F32),
        router_b.reshape(N_EXPERTS, 1).astype(F32))
    x1 = x1.reshape(t, d)
    h2 = h2.reshape(t, d // 2)

    dest, counts = _rank(top_i)
    n_items = t * TOP_K // ROW_TILE + N_EXPERTS
    plan = _plan(counts[:, 0], n_items)
    xs = _scatter(h2, dest)
    y = _experts(xs, plan, w_gate_up.astype(BF16), b_gate_up.reshape(N_EXPERTS, 1, -1).astype(F32),
                 w_down.astype(BF16), b_down.reshape(N_EXPERTS, 1, -1).astype(F32))
    out = _combine(x1, probs, dest, y)
    return out.reshape(bsz, seq, d)


def kernel(x, norm1_g, w_in, q_norm_g, k_norm_g, rel_bias, sg_ln_g, sg_ln_b, sg_w, sg_b,
           w_branch_a, w_branch_b, w_out, norm2_g, router_w, router_b,
           w_gate_up, b_gate_up, w_down, b_down):
    depth = norm1_g.shape[0]
    for l in range(depth):
        x = _layer(x, norm1_g[l], w_in[l], q_norm_g[l], k_norm_g[l], rel_bias[l], sg_ln_g[l],
                   sg_ln_b[l], sg_w[l], sg_b[l], w_branch_a[l], w_branch_b[l], w_out[l],
                   norm2_g[l], router_w[l], router_b[l], w_gate_up[l], b_gate_up[l],
                   w_down[l], b_down[l])
    return x
```

```python
import functools
import math

import numpy as np
import jax
import jax.numpy as jnp
from jax import lax
from jax.experimental import pallas as pl
from jax.experimental.pallas import tpu as pltpu
from jax.experimental.pallas import tpu_sc as plsc

F32 = jnp.float32
BF16 = jnp.bfloat16

D_MODEL = 1024
CHUNK = 64
N_BACK = 8
BAND = (N_BACK + 1) * CHUNK
N_HEADS = 8
HEAD_DIM = 64
ATTN_WIDTH = N_HEADS * HEAD_DIM
REL_CLIP = 256
SG_BLOCK = 128
SG_GROUPS = 4
SG_WIDTH = 512
IN_WIDTH = 3 * ATTN_WIDTH + 2 * SG_WIDTH + 2 * D_MODEL
N_EXPERTS = 32
TOP_K = 4
D_FF = D_MODEL
SWIGLU_LIMIT = 7.0
SWIGLU_ALPHA = 1.702
EPS = 1e-6
NEG = -1e30

LANES = 128
VMEM_BYTES_V7X = 64 * 1024 * 1024

SEQ_TILE = 512
Q_BLOCK = 2 * CHUNK
PREV = N_BACK * CHUNK
WIN = PREV + Q_BLOCK
RANK_TILE = 512
ROW_TILE = 512

SC_CORES = 2
SC_SUBCORES = 16
SC_LANES = 16
SC_WORKERS = SC_CORES * SC_SUBCORES
SC_SCATTER_CHUNK = 64
SC_COMBINE_CHUNK = 8

_C_Q = 0
_C_K = ATTN_WIDTH
_C_V = 2 * ATTN_WIDTH
_C_U = 3 * ATTN_WIDTH
_C_VS = _C_U + SG_WIDTH
_C_GA = _C_VS + SG_WIDTH
_C_GB = _C_GA + D_MODEL


def _dot(a, b):
    return jnp.dot(a, b, preferred_element_type=F32)


def _pack_bf16_pairs(v):
    w = v.shape[1] // 2
    lo = lax.bitcast_convert_type(v[:, :w].astype(BF16).astype(F32), jnp.uint32)
    hi = lax.bitcast_convert_type(v[:, w:].astype(BF16).astype(F32), jnp.uint32)
    return lax.bitcast_convert_type((lo >> 16) | hi, jnp.int32)


def _unpack_bf16_pairs(p):
    u = lax.bitcast_convert_type(p, jnp.uint32)
    lo = lax.bitcast_convert_type(u << 16, F32)
    hi = lax.bitcast_convert_type(u & jnp.uint32(0xFFFF0000), F32)
    return jnp.concatenate([lo, hi], axis=1).astype(BF16)


def _dot_nt(a, b, precision=None):
    return lax.dot_general(a, b, (((1,), (1,)), ((), ())),
                           preferred_element_type=F32, precision=precision)


def _mixer_kernel(x_ref, g1_ref, win_ref, qg_ref, kg_ref, hsum_ref, bias_ref, lng_ref, lnb_ref,
                  sgw_ref, sgb_ref, wa_ref, wb_ref, wo_ref, g2_ref, rwt_ref, rb_ref,
                  x1_ref, h2_ref, ti_ref, pr_ref,
                  h_scr, qe_scr, qo_scr, k_win, v_win, ya_scr, ysg_scr):
    ts = x_ref.shape[1]
    s_idx = pl.program_id(1)

    @pl.when(s_idx == 0)
    def _():
        k_win[0:PREV, :] = jnp.zeros((PREV, ATTN_WIDTH), BF16)
        v_win[0:PREV, :] = jnp.zeros((PREV, ATTN_WIDTH), BF16)

    @pl.when(s_idx > 0)
    def _():
        k_win[0:PREV, :] = k_win[ts:ts + PREV, :]
        v_win[0:PREV, :] = v_win[ts:ts + PREV, :]

    x = x_ref[0]
    ms = jnp.mean(x * x, axis=-1, keepdims=True)
    h_scr[...] = (x * lax.rsqrt(ms + EPS) * g1_ref[...]).astype(BF16)

    def proj(lo, width):
        return _dot(h_scr[...], win_ref[:, lo:lo + width])

    def head_rms(t, g_ref):
        ssq = _dot((t * t).astype(BF16), hsum_ref[...])
        return t * lax.rsqrt(ssq * (1.0 / HEAD_DIM) + EPS) * g_ref[...]

    lane = lax.broadcasted_iota(jnp.int32, (1, ATTN_WIDTH), 1)
    even_head = (lane % (2 * HEAD_DIM)) < HEAD_DIM
    qn = head_rms(proj(_C_Q, ATTN_WIDTH), qg_ref)
    qe_scr[...] = jnp.where(even_head, qn, 0.0).astype(BF16)
    qo_scr[...] = jnp.where(even_head, 0.0, qn).astype(BF16)
    k_win[PREV:PREV + ts, :] = head_rms(proj(_C_K, ATTN_WIDTH), kg_ref).astype(BF16)
    v_win[PREV:PREV + ts, :] = proj(_C_V, ATTN_WIDTH).astype(BF16)

    u = jax.nn.gelu(proj(_C_U, SG_WIDTH))
    vs = jax.nn.gelu(proj(_C_VS, SG_WIDTH))
    r_i = lax.broadcasted_iota(jnp.int32, (SG_BLOCK, SG_BLOCK), 0)
    c_i = lax.broadcasted_iota(jnp.int32, (SG_BLOCK, SG_BLOCK), 1)
    tril = r_i >= c_i
    gdim = SG_WIDTH // SG_GROUPS
    for g in range(SG_GROUPS):
        cs = slice(g * gdim, (g + 1) * gdim)
        vg = vs[:, cs]
        mu = jnp.mean(vg, axis=-1, keepdims=True)
        xc = vg - mu
        var = jnp.mean(xc * xc, axis=-1, keepdims=True)
        vn = (xc * lax.rsqrt(var + EPS) * lng_ref[:, cs] + lnb_ref[:, cs]).astype(BF16)
        wm = jnp.where(tril, sgw_ref[g], 0.0).astype(BF16)
        for j in range(ts // SG_BLOCK):
            rs = slice(j * SG_BLOCK, (j + 1) * SG_BLOCK)
            mixed = _dot(wm, vn[rs]) + sgb_ref[:, cs]
            ysg_scr[rs, cs] = (u[rs, cs] * mixed).astype(BF16)

    col = lax.broadcasted_iota(jnp.int32, (1, WIN), 1)
    out_lane = lax.broadcasted_iota(jnp.int32, (1, 2 * HEAD_DIM), 1)
    first_half = out_lane < HEAD_DIM

    def attn_block(qb, carry):
        r0 = pl.multiple_of(qb * Q_BLOCK, Q_BLOCK)
        valid = col >= (PREV - (s_idx * ts + r0))
        for pair in range(N_HEADS // 2):
            cs = slice(pair * 2 * HEAD_DIM, (pair + 1) * 2 * HEAD_DIM)
            kb = k_win[pl.ds(r0, WIN), cs]
            vb = v_win[pl.ds(r0, WIN), cs]
            outs = []
            for hh, q_scr in enumerate((qe_scr, qo_scr)):
                s = _dot_nt(q_scr[pl.ds(r0, Q_BLOCK), cs], kb) + bias_ref[2 * pair + hh]
                s = jnp.where(valid, s, NEG)
                m = jnp.max(s, axis=-1, keepdims=True)
                p = jnp.exp(s - m)
                l = jnp.sum(p, axis=-1, keepdims=True)
                outs.append(_dot(p.astype(BF16), vb) / l)
            ya_scr[pl.ds(r0, Q_BLOCK), cs] = jnp.where(first_half, outs[0], outs[1]).astype(BF16)
        return carry

    lax.fori_loop(0, ts // Q_BLOCK, attn_block, 0)

    merged = jax.nn.sigmoid(proj(_C_GA, D_MODEL)) * _dot(ya_scr[...], wa_ref[...])
    merged = merged + jax.nn.sigmoid(proj(_C_GB, D_MODEL)) * _dot(ysg_scr[...], wb_ref[...])
    x1 = x_ref[0] + _dot(merged.astype(BF16), wo_ref[...])
    x1_ref[0] = x1

    ms2 = jnp.mean(x1 * x1, axis=-1, keepdims=True)
    h2 = x1 * lax.rsqrt(ms2 + EPS) * g2_ref[...]
    h2_ref[0] = _pack_bf16_pairs(h2)
    lt = _dot_nt(rwt_ref[...], h2, precision=lax.Precision.HIGHEST) + rb_ref[...]
    e_iota = lax.broadcasted_iota(jnp.int32, lt.shape, 0)
    vals = []
    for k in range(TOP_K):
        m = jnp.max(lt, axis=0, keepdims=True)
        idx = jnp.min(jnp.where(lt == m, e_iota, N_EXPERTS), axis=0, keepdims=True)
        vals.append(m)
        ti_ref[k:k + 1, :] = idx
        lt = jnp.where(e_iota == idx, -jnp.inf, lt)
    exps = [jnp.exp(v - vals[0]) for v in vals]
    denom = exps[0] + exps[1] + exps[2] + exps[3]
    for k in range(TOP_K):
        pr_ref[k:k + 1, :] = exps[k] / denom


def _const_spec(shape):
    zeros = (0,) * len(shape)
    return pl.BlockSpec(shape, lambda b, s: zeros, pipeline_mode=pl.Buffered(1))


def _mixer(x, g1, win, qg, kg, hsum, bias, lng, lnb, sgw, sgb, wa, wb, wo, g2, rwt, rb):
    bsz, seq, d = x.shape
    ts = SEQ_TILE
    ns = seq // ts
    t = bsz * seq
    consts = (g1, win, qg, kg, hsum, bias, lng, lnb, sgw, sgb, wa, wb, wo, g2, rwt, rb)
    tok_spec = pl.BlockSpec((1, ts, d), lambda b, s: (b, s, 0))
    packed_spec = pl.BlockSpec((1, ts, d // 2), lambda b, s: (b, s, 0))
    idx_spec = pl.BlockSpec((TOP_K, ts), lambda b, s: (0, b * ns + s))
    return pl.pallas_call(
        _mixer_kernel,
        grid=(bsz, ns),
        in_specs=[tok_spec] + [_const_spec(c.shape) for c in consts],
        out_specs=[tok_spec, packed_spec, idx_spec, idx_spec],
        out_shape=[
            jax.ShapeDtypeStruct((bsz, seq, d), F32),
            jax.ShapeDtypeStruct((bsz, seq, d // 2), jnp.int32),
            jax.ShapeDtypeStruct((TOP_K, t), jnp.int32),
            jax.ShapeDtypeStruct((TOP_K, t), F32),
        ],
        scratch_shapes=[
            pltpu.VMEM((ts, d), BF16),
            pltpu.VMEM((ts, ATTN_WIDTH), BF16),
            pltpu.VMEM((ts, ATTN_WIDTH), BF16),
            pltpu.VMEM((PREV + ts, ATTN_WIDTH), BF16),
            pltpu.VMEM((PREV + ts, ATTN_WIDTH), BF16),
            pltpu.VMEM((ts, ATTN_WIDTH), BF16),
            pltpu.VMEM((ts, SG_WIDTH), BF16),
        ],
        compiler_params=pltpu.CompilerParams(
            dimension_semantics=("arbitrary", "arbitrary"),
            vmem_limit_bytes=VMEM_BYTES_V7X * 7 // 8),
        name="mixer",
    )(x, *consts)


def _attention_bias(rel_bias):
    i = np.arange(Q_BLOCK)[:, None]
    j = np.arange(WIN)[None, :]
    kk = j - (i // CHUNK) * CHUNK
    inside = (kk >= 0) & (kk < BAND)
    tab = rel_bias.astype(F32)
    far = PREV + Q_BLOCK - 1 - REL_CLIP + 1
    near = Q_BLOCK - 1 + REL_CLIP
    ext = jnp.concatenate(
        [jnp.broadcast_to(tab[:, 2 * REL_CLIP:], (N_HEADS, far)),
         tab[:, 2 * REL_CLIP - near:2 * REL_CLIP][:, ::-1]], axis=1)
    rows = [ext[:, Q_BLOCK - 1 - r:Q_BLOCK - 1 - r + WIN] for r in range(Q_BLOCK)]
    return jnp.where(inside[None], jnp.stack(rows, axis=1), NEG)


def _rank_kernel(ti_ref, upper_ref, dest_ref, cnt_ref, total_scr, base_scr):
    phase = pl.program_id(0)
    i = pl.program_id(1)
    tt = ti_ref.shape[1]

    @pl.when((phase == 0) & (i == 0))
    def _():
        total_scr[...] = jnp.zeros_like(total_scr)

    ti = ti_ref[...]
    e_iota = lax.broadcasted_iota(jnp.int32, (N_EXPERTS, tt), 0)
    hits = [e_iota == ti[k:k + 1, :] for k in range(TOP_K)]
    cnt = hits[0].astype(F32)
    for k in range(1, TOP_K):
        cnt = cnt + hits[k].astype(F32)
    tile_total = jnp.sum(cnt, axis=1, keepdims=True)

    @pl.when(phase == 0)
    def _():
        total_scr[...] = total_scr[...] + tile_total

    @pl.when((phase == 1) & (i == 0))
    def _():
        cnt_ref[...] = total_scr[...].astype(jnp.int32)
        run = jnp.zeros((1, LANES), F32)
        for e in range(N_EXPERTS):
            base_scr[e:e + 1, :] = run
            run = run + total_scr[e:e + 1, :]

    @pl.when(phase == 1)
    def _():
        before = _dot(cnt.astype(BF16), upper_ref[...]) + base_scr[:, 0:1]
        for k in range(TOP_K):
            dest_ref[k:k + 1, :] = jnp.sum(jnp.where(hits[k], before, 0.0), axis=0,
                                           keepdims=True).astype(jnp.int32)
        base_scr[...] = base_scr[...] + tile_total


def _rank(top_i):
    t = top_i.shape[1]
    tt = RANK_TILE
    upper = jnp.asarray(np.triu(np.ones((tt, tt), np.float32), 1), BF16)
    return pl.pallas_call(
        _rank_kernel,
        grid=(2, t // tt),
        in_specs=[pl.BlockSpec((TOP_K, tt), lambda p, i: (0, i)),
                  pl.BlockSpec((tt, tt), lambda p, i: (0, 0), pipeline_mode=pl.Buffered(1))],
        out_specs=[pl.BlockSpec((TOP_K, tt), lambda p, i: (0, i * p)),
                   pl.BlockSpec((N_EXPERTS, LANES), lambda p, i: (0, 0))],
        out_shape=[jax.ShapeDtypeStruct((TOP_K, t), jnp.int32),
                   jax.ShapeDtypeStruct((N_EXPERTS, LANES), jnp.int32)],
        scratch_shapes=[pltpu.VMEM((N_EXPERTS, LANES), F32),
                        pltpu.VMEM((N_EXPERTS, LANES), F32)],
        compiler_params=pltpu.CompilerParams(dimension_semantics=("arbitrary", "arbitrary")),
        name="rank",
    )(top_i, upper)


def _plan_kernel(cnt_ref, tile_ref, exp_ref, lo_ref, first_ref, nwork_ref):
    n_items = tile_ref.shape[0]

    def per_expert(e, carry):
        w, start = carry
        c = cnt_ref[e]
        end = start + c
        t0 = start // ROW_TILE
        n = jnp.where(c > 0, (end - 1) // ROW_TILE - t0 + 1, 0)

        def per_tile(j, _):
            tile = t0 + j
            tile_ref[w + j] = tile
            exp_ref[w + j] = e
            lo_ref[w + j] = jnp.maximum(start - tile * ROW_TILE, 0)
            first_ref[w + j] = (j == 0).astype(jnp.int32)
            return 0

        lax.fori_loop(0, n, per_tile, 0)
        return w + n, end

    n_work, _ = lax.fori_loop(0, N_EXPERTS, per_expert, (jnp.int32(0), jnp.int32(0)))
    nwork_ref[0] = n_work

    def pad(w, _):
        tile_ref[w] = tile_ref[n_work - 1]
        exp_ref[w] = exp_ref[n_work - 1]
        lo_ref[w] = 0
        first_ref[w] = 0
        return 0

    lax.fori_loop(n_work, n_items, pad, 0)


def _plan(counts, n_items):
    smem = pl.BlockSpec(memory_space=pltpu.SMEM)
    vec = jax.ShapeDtypeStruct((n_items,), jnp.int32)
    return pl.pallas_call(
        _plan_kernel,
        in_specs=[smem],
        out_specs=[smem, smem, smem, smem, smem],
        out_shape=[vec, vec, vec, vec, jax.ShapeDtypeStruct((1,), jnp.int32)],
        name="plan",
    )(counts)


def _scatter(h2, dest):
    t, d = h2.shape
    per_worker = t // SC_WORKERS
    n = SC_SCATTER_CHUNK
    nch = per_worker // n
    assert nch % 2 == 0
    mesh = plsc.VectorSubcoreMesh(core_axis_name="c", subcore_axis_name="s")

    @functools.partial(
        pl.kernel, mesh=mesh,
        out_type=jax.ShapeDtypeStruct((t * TOP_K, d), h2.dtype),
        scratch_types=[pltpu.VMEM((TOP_K, nch, n), jnp.int32), pltpu.VMEM((2, n, d), h2.dtype),
                       pltpu.SemaphoreType.DMA((2,)), pltpu.SemaphoreType.DMA((2,))],
        name="sc_scatter")
    def scatter_kernel(h_hbm, dest_hbm, xs_hbm, idx_v, rows_v, sem_in, sem_out):
        wid = lax.axis_index("s") * SC_CORES + lax.axis_index("c")
        base = wid * per_worker
        for k in range(TOP_K):
            pltpu.sync_copy(dest_hbm.at[k, pl.ds(wid * nch, nch)], idx_v.at[k])

        def load(c, s):
            return pltpu.make_async_copy(h_hbm.at[pl.ds(base + c * n, n)], rows_v.at[s],
                                         sem_in.at[s])

        def store(c, s, k):
            return pltpu.make_async_copy(rows_v.at[s], xs_hbm.at[idx_v.at[k, c]], sem_out.at[s])

        load(0, 0).start()

        @pl.loop(0, nch, step=2)
        def _(c):
            for s in range(2):
                cc = c + s

                @pl.when(cc >= 1)
                def _():
                    for k in range(TOP_K):
                        store(cc - 1, 1 - s, k).wait()

                @pl.when(cc + 1 < nch)
                def _():
                    load(cc + 1, 1 - s).start()

                load(cc, s).wait()
                for k in range(TOP_K):
                    store(cc, s, k).start()

        for k in range(TOP_K):
            store(nch - 1, (nch - 1) % 2, k).wait()

    return scatter_kernel(h2, dest.reshape(TOP_K, t // n, n))


def _expert_kernel(tile_ref, exp_ref, lo_ref, first_ref, nwork_ref,
                   xs_ref, wgu_ref, bgu_ref, wd_ref, bd_ref, y_ref, wgu_bf, wd_bf):
    w = pl.program_id(0)

    @pl.when(first_ref[w] == 1)
    def _():
        wgu_bf[...] = wgu_ref[0].astype(BF16)
        wd_bf[...] = wd_ref[0].astype(BF16)

    @pl.when(w < nwork_ref[0])
    def _():
        gu = _dot(_unpack_bf16_pairs(xs_ref[...]), wgu_bf[...]) + bgu_ref[0]
        gate = jnp.minimum(gu[:, :D_FF], SWIGLU_LIMIT)
        up = jnp.clip(gu[:, D_FF:], -SWIGLU_LIMIT, SWIGLU_LIMIT)
        act = (up + 1.0) * (gate * jax.nn.sigmoid(gate * SWIGLU_ALPHA))
        y = _pack_bf16_pairs(_dot(act.astype(BF16), wd_bf[...]) + bd_ref[0])
        lo = lo_ref[w]

        @pl.when(lo == 0)
        def _():
            y_ref[...] = y

        @pl.when(lo > 0)
        def _():
            row = lax.broadcasted_iota(jnp.int32, y.shape, 0)
            y_ref[...] = jnp.where(row >= lo, y, y_ref[...])


def _experts(xs, plan, wgu, bgu, wd, bd):
    n_rows, dp = xs.shape
    d = 2 * dp
    w_tile, w_exp, w_lo, w_first, n_work = plan
    n_items = w_tile.shape[0]
    tm = ROW_TILE
    grid_spec = pltpu.PrefetchScalarGridSpec(
        num_scalar_prefetch=5,
        grid=(n_items,),
        in_specs=[
            pl.BlockSpec((tm, dp), lambda w, tl, ex, lo, fi, nw: (tl[w], 0)),
            pl.BlockSpec((1, d, 2 * D_FF), lambda w, tl, ex, lo, fi, nw: (ex[w], 0, 0)),
            pl.BlockSpec((1, 1, 2 * D_FF), lambda w, tl, ex, lo, fi, nw: (ex[w], 0, 0)),
            pl.BlockSpec((1, D_FF, d), lambda w, tl, ex, lo, fi, nw: (ex[w], 0, 0)),
            pl.BlockSpec((1, 1, d), lambda w, tl, ex, lo, fi, nw: (ex[w], 0, 0)),
        ],
        out_specs=pl.BlockSpec((tm, dp), lambda w, tl, ex, lo, fi, nw: (tl[w], 0)),
        scratch_shapes=[pltpu.VMEM((d, 2 * D_FF), BF16), pltpu.VMEM((D_FF, d), BF16)],
    )
    return pl.pallas_call(
        _expert_kernel,
        grid_spec=grid_spec,
        out_shape=jax.ShapeDtypeStruct((n_rows, dp), jnp.int32),
        compiler_params=pltpu.CompilerParams(
            dimension_semantics=("arbitrary",),
            vmem_limit_bytes=VMEM_BYTES_V7X * 7 // 8),
        name="experts",
    )(w_tile, w_exp, w_lo, w_first, n_work, xs, wgu, bgu, wd, bd)


def _combine(x1, probs, dest, y):
    t, d = x1.shape
    per_worker = t // SC_WORKERS
    dp = d // 2
    n = SC_COMBINE_CHUNK
    nch = per_worker // n
    assert nch % 2 == 0
    probs_e = jnp.repeat(probs.T, SC_LANES, axis=1)
    mesh = plsc.VectorSubcoreMesh(core_axis_name="c", subcore_axis_name="s")
    high_half = -65536

    @functools.partial(
        pl.kernel, mesh=mesh,
        out_type=jax.ShapeDtypeStruct((t, d), F32),
        scratch_types=[pltpu.VMEM((TOP_K, per_worker), jnp.int32),
                       pltpu.VMEM((2, TOP_K, n, dp), jnp.int32),
                       pltpu.VMEM((2, n, d), F32), pltpu.VMEM((2, n, d), F32),
                       pltpu.VMEM((2, n, TOP_K * SC_LANES), F32),
                       pltpu.SemaphoreType.DMA((2,)), pltpu.SemaphoreType.DMA((2,))],
        compiler_params=pltpu.CompilerParams(needs_layout_passes=False),
        name="sc_combine")
    def combine_kernel(x1_hbm, p_hbm, dest_hbm, y_hbm, out_hbm,
                       idx_v, rows_v, x_v, o_v, p_v, sem_in, sem_out):
        wid = lax.axis_index("s") * SC_CORES + lax.axis_index("c")
        base = wid * per_worker
        for k in range(TOP_K):
            pltpu.sync_copy(dest_hbm.at[k, pl.ds(base, per_worker)], idx_v.at[k])

        def loads(c, s):
            off = base + c * n
            cps = [pltpu.make_async_copy(y_hbm.at[idx_v.at[k, pl.ds(c * n, n)]], rows_v.at[s, k], sem_in.at[s])
                   for k in range(TOP_K)]
            cps.append(pltpu.make_async_copy(x1_hbm.at[pl.ds(off, n)], x_v.at[s], sem_in.at[s]))
            cps.append(pltpu.make_async_copy(p_hbm.at[pl.ds(off, n)], p_v.at[s], sem_in.at[s]))
            return cps

        def store(c, s):
            return pltpu.make_async_copy(o_v.at[s], out_hbm.at[pl.ds(base + c * n, n)],
                                         sem_out.at[s])

        for cp in loads(0, 0):
            cp.start()

        @pl.loop(0, nch, step=2)
        def _(c):
            for s in range(2):
                cc = c + s

                @pl.when(cc + 1 < nch)
                def _():
                    for cp in loads(cc + 1, 1 - s):
                        cp.start()

                for cp in loads(cc, s):
                    cp.wait()

                @pl.when(cc >= 2)
                def _():
                    store(cc - 2, s).wait()

                @pl.loop(0, n)
                def _(r):
                    w = [p_v[s, r, pl.ds(k * SC_LANES, SC_LANES)] for k in range(TOP_K)]
                    for g in range(dp // SC_LANES):
                        lo_sl = pl.ds(g * SC_LANES, SC_LANES)
                        hi_sl = pl.ds(dp + g * SC_LANES, SC_LANES)
                        lo = x_v[s, r, lo_sl]
                        hi = x_v[s, r, hi_sl]
                        for k in range(TOP_K):
                            word = rows_v[s, k, r, lo_sl]
                            lo = lo + w[k] * plsc.bitcast(word << 16, F32)
                            hi = hi + w[k] * plsc.bitcast(word & high_half, F32)
                        o_v[s, r, lo_sl] = lo
                        o_v[s, r, hi_sl] = hi

                store(cc, s).start()

        store(nch - 2, 0).wait()
        store(nch - 1, 1).wait()

    return combine_kernel(x1, probs_e, dest, y)


def _layer(x, norm1_g, w_in, q_norm_g, k_norm_g, rel_bias, sg_ln_g, sg_ln_b, sg_w, sg_b,
           w_branch_a, w_branch_b, w_out, norm2_g, router_w, router_b,
           w_gate_up, b_gate_up, w_down, b_down):
    bsz, seq, d = x.shape
    t = bsz * seq
    row = lambda v: v.reshape(1, -1).astype(F32)
    scale = 1.0 / math.sqrt(HEAD_DIM)
    head_sum = jnp.asarray(np.kron(np.eye(N_HEADS), np.ones((HEAD_DIM, HEAD_DIM))), BF16)
    sgb = jnp.repeat(sg_b.T.astype(F32), SG_WIDTH // SG_GROUPS, axis=1)

    x1, h2, top_i, probs = _mixer(
        x, row(norm1_g), w_in.astype(BF16),
        row(jnp.tile(q_norm_g, N_HEADS)) * scale, row(jnp.tile(k_norm_g, N_HEADS)),
        head_sum, _attention_bias(rel_bias), row(sg_ln_g), row(sg_ln_b),
        sg_w.astype(F32), sgb, w_branch_a.astype(BF16), w_branch_b.astype(BF16),
        w_out.astype(BF16), row(norm2_g), router_w.T.astype(F32),
        router_b.reshape(N_EXPERTS, 1).astype(F32))
    x1 = x1.reshape(t, d)
    h2 = h2.reshape(t, d // 2)

    dest, counts = _rank(top_i)
    n_items = t * TOP_K // ROW_TILE + N_EXPERTS
    plan = _plan(counts[:, 0], n_items)
    xs = _scatter(h2, dest)
    y = _experts(xs, plan, w_gate_up.astype(F32), b_gate_up.reshape(N_EXPERTS, 1, -1).astype(F32),
                 w_down.astype(F32), b_down.reshape(N_EXPERTS, 1, -1).astype(F32))
    out = _combine(x1, probs, dest, y)
    return out.reshape(bsz, seq, d)


def kernel(x, norm1_g, w_in, q_norm_g, k_norm_g, rel_bias, sg_ln_g, sg_ln_b, sg_w, sg_b,
           w_branch_a, w_branch_b, w_out, norm2_g, router_w, router_b,
           w_gate_up, b_gate_up, w_down, b_down):
    depth = norm1_g.shape[0]
    for l in range(depth):
        x = _layer(x, norm1_g[l], w_in[l], q_norm_g[l], k_norm_g[l], rel_bias[l], sg_ln_g[l],
                   sg_ln_b[l], sg_w[l], sg_b[l], w_branch_a[l], w_branch_b[l], w_out[l],
                   norm2_g[l], router_w[l], router_b[l], w_gate_up[l], b_gate_up[l],
                   w_down[l], b_down[l])
    return x
```

```python
import functools
import math

import numpy as np
import jax
import jax.numpy as jnp
from jax import lax
from jax.experimental import pallas as pl
from jax.experimental.pallas import tpu as pltpu
from jax.experimental.pallas import tpu_sc as plsc

F32 = jnp.float32
BF16 = jnp.bfloat16

D_MODEL = 1024
CHUNK = 64
N_BACK = 8
BAND = (N_BACK + 1) * CHUNK
N_HEADS = 8
HEAD_DIM = 64
ATTN_WIDTH = N_HEADS * HEAD_DIM
REL_CLIP = 256
SG_BLOCK = 128
SG_GROUPS = 4
SG_WIDTH = 512
IN_WIDTH = 3 * ATTN_WIDTH + 2 * SG_WIDTH + 2 * D_MODEL
N_EXPERTS = 32
TOP_K = 4
D_FF = D_MODEL
SWIGLU_LIMIT = 7.0
SWIGLU_ALPHA = 1.702
EPS = 1e-6
NEG = -1e30

LANES = 128
VMEM_BYTES_V7X = 64 * 1024 * 1024

SEQ_TILE = 512
Q_BLOCK = 2 * CHUNK
PREV = N_BACK * CHUNK
WIN = PREV + Q_BLOCK
RANK_TILE = 512
ROW_TILE = 512

SC_CORES = 2
SC_SUBCORES = 16
SC_LANES = 16
SC_WORKERS = SC_CORES * SC_SUBCORES
SC_SCATTER_CHUNK = 64
SC_COMBINE_CHUNK = 8

_C_Q = 0
_C_K = ATTN_WIDTH
_C_V = 2 * ATTN_WIDTH
_C_U = 3 * ATTN_WIDTH
_C_VS = _C_U + SG_WIDTH
_C_GA = _C_VS + SG_WIDTH
_C_GB = _C_GA + D_MODEL


def _dot(a, b):
    return jnp.dot(a, b, preferred_element_type=F32)


def _pack_bf16_pairs(v):
    w = v.shape[1] // 2
    lo = lax.bitcast_convert_type(v[:, :w].astype(BF16).astype(F32), jnp.uint32)
    hi = lax.bitcast_convert_type(v[:, w:].astype(BF16).astype(F32), jnp.uint32)
    return lax.bitcast_convert_type((lo >> 16) | hi, jnp.int32)


def _unpack_bf16_pairs(p):
    u = lax.bitcast_convert_type(p, jnp.uint32)
    lo = lax.bitcast_convert_type(u << 16, F32)
    hi = lax.bitcast_convert_type(u & jnp.uint32(0xFFFF0000), F32)
    return jnp.concatenate([lo, hi], axis=1).astype(BF16)


def _dot_nt(a, b, precision=None):
    return lax.dot_general(a, b, (((1,), (1,)), ((), ())),
                           preferred_element_type=F32, precision=precision)


def _mixer_kernel(x_ref, g1_ref, win_ref, wvt_ref, qg_ref, kg_ref, hsum_ref, bias_ref, lng_ref,
                  lnb_ref, sgw_ref, sgb_ref, wa_ref, wb_ref, wo_ref, g2_ref, rwt_ref, rb_ref,
                  x1_ref, h2_ref, ti_ref, pr_ref,
                  h_scr, q_scr, k_win, vt_win, ya_scr, ysg_scr):
    ts = x_ref.shape[1]
    s_idx = pl.program_id(1)

    @pl.when(s_idx == 0)
    def _():
        k_win[0:PREV, :] = jnp.zeros((PREV, ATTN_WIDTH), BF16)
        vt_win[:, 0:PREV] = jnp.zeros((ATTN_WIDTH, PREV), BF16)

    @pl.when(s_idx > 0)
    def _():
        k_win[0:PREV, :] = k_win[ts:ts + PREV, :]
        vt_win[:, 0:PREV] = vt_win[:, ts:ts + PREV]

    x = x_ref[0]
    ms = jnp.mean(x * x, axis=-1, keepdims=True)
    h_scr[...] = (x * lax.rsqrt(ms + EPS) * g1_ref[...]).astype(BF16)

    def proj(lo, width):
        return _dot(h_scr[...], win_ref[:, lo:lo + width])

    def head_rms(t, g_ref):
        ssq = _dot((t * t).astype(BF16), hsum_ref[...])
        return t * lax.rsqrt(ssq * (1.0 / HEAD_DIM) + EPS) * g_ref[...]

    lane = lax.broadcasted_iota(jnp.int32, (1, ATTN_WIDTH), 1)
    even_head = (lane % (2 * HEAD_DIM)) < HEAD_DIM
    qn = head_rms(proj(_C_Q, ATTN_WIDTH), qg_ref)
    q_even = jnp.where(even_head, qn, 0.0).astype(BF16)
    q_odd = jnp.where(even_head, 0.0, qn).astype(BF16)
    for qb in range(ts // Q_BLOCK):
        rs = slice(qb * Q_BLOCK, (qb + 1) * Q_BLOCK)
        q_scr[qb, 0:Q_BLOCK, :] = q_even[rs]
        q_scr[qb, Q_BLOCK:2 * Q_BLOCK, :] = q_odd[rs]
    k_win[PREV:PREV + ts, :] = head_rms(proj(_C_K, ATTN_WIDTH), kg_ref).astype(BF16)
    vt_win[:, PREV:PREV + ts] = _dot_nt(wvt_ref[...], h_scr[...]).astype(BF16)

    u = jax.nn.gelu(proj(_C_U, SG_WIDTH))
    vs = jax.nn.gelu(proj(_C_VS, SG_WIDTH))
    r_i = lax.broadcasted_iota(jnp.int32, (SG_BLOCK, SG_BLOCK), 0)
    c_i = lax.broadcasted_iota(jnp.int32, (SG_BLOCK, SG_BLOCK), 1)
    tril = r_i >= c_i
    gdim = SG_WIDTH // SG_GROUPS
    for g in range(SG_GROUPS):
        cs = slice(g * gdim, (g + 1) * gdim)
        vg = vs[:, cs]
        mu = jnp.mean(vg, axis=-1, keepdims=True)
        xc = vg - mu
        var = jnp.mean(xc * xc, axis=-1, keepdims=True)
        vn = (xc * lax.rsqrt(var + EPS) * lng_ref[:, cs] + lnb_ref[:, cs]).astype(BF16)
        wm = jnp.where(tril, sgw_ref[g], 0.0).astype(BF16)
        for j in range(ts // SG_BLOCK):
            rs = slice(j * SG_BLOCK, (j + 1) * SG_BLOCK)
            mixed = _dot(wm, vn[rs]) + sgb_ref[:, cs]
            ysg_scr[rs, cs] = (u[rs, cs] * mixed).astype(BF16)

    key_row = lax.broadcasted_iota(jnp.int32, (WIN, 1), 0)
    for qb in range(ts // Q_BLOCK):
        r0 = qb * Q_BLOCK
        valid = key_row >= (PREV - (s_idx * ts + r0))
        for pair in range(N_HEADS // 2):
            cs = slice(pair * 2 * HEAD_DIM, (pair + 1) * 2 * HEAD_DIM)
            st = _dot_nt(k_win[r0:r0 + WIN, cs], q_scr[qb, :, cs]) + bias_ref[pair]
            st = jnp.where(valid, st, NEG)
            m = jnp.max(st, axis=0, keepdims=True)
            p = jnp.exp(st - m)
            inv = 1.0 / jnp.sum(p, axis=0, keepdims=True)
            ot = _dot(vt_win[cs, r0:r0 + WIN], p.astype(BF16))
            o = jnp.concatenate(
                [ot[0:HEAD_DIM, 0:Q_BLOCK] * inv[:, 0:Q_BLOCK],
                 ot[HEAD_DIM:2 * HEAD_DIM, Q_BLOCK:2 * Q_BLOCK] * inv[:, Q_BLOCK:2 * Q_BLOCK]],
                axis=0)
            ya_scr[r0:r0 + Q_BLOCK, cs] = o.T.astype(BF16)

    merged = jax.nn.sigmoid(proj(_C_GA, D_MODEL)) * _dot(ya_scr[...], wa_ref[...])
    merged = merged + jax.nn.sigmoid(proj(_C_GB, D_MODEL)) * _dot(ysg_scr[...], wb_ref[...])
    x1 = x_ref[0] + _dot(merged.astype(BF16), wo_ref[...])
    x1_ref[0] = x1

    ms2 = jnp.mean(x1 * x1, axis=-1, keepdims=True)
    h2 = x1 * lax.rsqrt(ms2 + EPS) * g2_ref[...]
    h2_ref[0] = _pack_bf16_pairs(h2)
    lt = _dot_nt(rwt_ref[...], h2, precision=lax.Precision.HIGHEST) + rb_ref[...]
    e_iota = lax.broadcasted_iota(jnp.int32, lt.shape, 0)
    vals = []
    for k in range(TOP_K):
        m = jnp.max(lt, axis=0, keepdims=True)
        idx = jnp.min(jnp.where(lt == m, e_iota, N_EXPERTS), axis=0, keepdims=True)
        vals.append(m)
        ti_ref[k:k + 1, :] = idx
        lt = jnp.where(e_iota == idx, -jnp.inf, lt)
    exps = [jnp.exp(v - vals[0]) for v in vals]
    denom = exps[0] + exps[1] + exps[2] + exps[3]
    for k in range(TOP_K):
        pr_ref[k:k + 1, :] = exps[k] / denom


def _const_spec(shape):
    zeros = (0,) * len(shape)
    return pl.BlockSpec(shape, lambda b, s: zeros, pipeline_mode=pl.Buffered(1))


def _mixer(x, g1, win, wvt, qg, kg, hsum, bias, lng, lnb, sgw, sgb, wa, wb, wo, g2, rwt, rb):
    bsz, seq, d = x.shape
    ts = SEQ_TILE
    ns = seq // ts
    t = bsz * seq
    consts = (g1, win, wvt, qg, kg, hsum, bias, lng, lnb, sgw, sgb, wa, wb, wo, g2, rwt, rb)
    tok_spec = pl.BlockSpec((1, ts, d), lambda b, s: (b, s, 0))
    packed_spec = pl.BlockSpec((1, ts, d // 2), lambda b, s: (b, s, 0))
    idx_spec = pl.BlockSpec((TOP_K, ts), lambda b, s: (0, b * ns + s))
    return pl.pallas_call(
        _mixer_kernel,
        grid=(bsz, ns),
        in_specs=[tok_spec] + [_const_spec(c.shape) for c in consts],
        out_specs=[tok_spec, packed_spec, idx_spec, idx_spec],
        out_shape=[
            jax.ShapeDtypeStruct((bsz, seq, d), F32),
            jax.ShapeDtypeStruct((bsz, seq, d // 2), jnp.int32),
            jax.ShapeDtypeStruct((TOP_K, t), jnp.int32),
            jax.ShapeDtypeStruct((TOP_K, t), F32),
        ],
        scratch_shapes=[
            pltpu.VMEM((ts, d), BF16),
            pltpu.VMEM((ts // Q_BLOCK, 2 * Q_BLOCK, ATTN_WIDTH), BF16),
            pltpu.VMEM((PREV + ts, ATTN_WIDTH), BF16),
            pltpu.VMEM((ATTN_WIDTH, PREV + ts), BF16),
            pltpu.VMEM((ts, ATTN_WIDTH), BF16),
            pltpu.VMEM((ts, SG_WIDTH), BF16),
        ],
        compiler_params=pltpu.CompilerParams(
            dimension_semantics=("arbitrary", "arbitrary"),
            vmem_limit_bytes=VMEM_BYTES_V7X * 7 // 8),
        name="mixer",
    )(x, *consts)


def _attention_bias(rel_bias):
    i = np.arange(Q_BLOCK)[:, None]
    j = np.arange(WIN)[None, :]
    kk = j - (i // CHUNK) * CHUNK
    inside = (kk >= 0) & (kk < BAND)
    tab = rel_bias.astype(F32)
    far = PREV + Q_BLOCK - 1 - REL_CLIP + 1
    near = Q_BLOCK - 1 + REL_CLIP
    ext = jnp.concatenate(
        [jnp.broadcast_to(tab[:, 2 * REL_CLIP:], (N_HEADS, far)),
         tab[:, 2 * REL_CLIP - near:2 * REL_CLIP][:, ::-1]], axis=1)
    rows = [ext[:, Q_BLOCK - 1 - r:Q_BLOCK - 1 - r + WIN] for r in range(Q_BLOCK)]
    bias = jnp.where(inside[None], jnp.stack(rows, axis=1), NEG)
    return (bias.reshape(N_HEADS // 2, 2 * Q_BLOCK, WIN).transpose(0, 2, 1))


def _rank_kernel(ti_ref, upper_ref, dest_ref, cnt_ref, total_scr, base_scr):
    phase = pl.program_id(0)
    i = pl.program_id(1)
    tt = ti_ref.shape[1]

    @pl.when((phase == 0) & (i == 0))
    def _():
        total_scr[...] = jnp.zeros_like(total_scr)

    ti = ti_ref[...]
    e_iota = lax.broadcasted_iota(jnp.int32, (N_EXPERTS, tt), 0)
    hits = [e_iota == ti[k:k + 1, :] for k in range(TOP_K)]
    cnt = hits[0].astype(F32)
    for k in range(1, TOP_K):
        cnt = cnt + hits[k].astype(F32)
    tile_total = jnp.sum(cnt, axis=1, keepdims=True)

    @pl.when(phase == 0)
    def _():
        total_scr[...] = total_scr[...] + tile_total

    @pl.when((phase == 1) & (i == 0))
    def _():
        cnt_ref[...] = total_scr[...].astype(jnp.int32)
        run = jnp.zeros((1, LANES), F32)
        for e in range(N_EXPERTS):
            base_scr[e:e + 1, :] = run
            run = run + total_scr[e:e + 1, :]

    @pl.when(phase == 1)
    def _():
        before = _dot(cnt.astype(BF16), upper_ref[...]) + base_scr[:, 0:1]
        for k in range(TOP_K):
            dest_ref[k:k + 1, :] = jnp.sum(jnp.where(hits[k], before, 0.0), axis=0,
                                           keepdims=True).astype(jnp.int32)
        base_scr[...] = base_scr[...] + tile_total


def _rank(top_i):
    t = top_i.shape[1]
    tt = RANK_TILE
    upper = jnp.asarray(np.triu(np.ones((tt, tt), np.float32), 1), BF16)
    return pl.pallas_call(
        _rank_kernel,
        grid=(2, t // tt),
        in_specs=[pl.BlockSpec((TOP_K, tt), lambda p, i: (0, i)),
                  pl.BlockSpec((tt, tt), lambda p, i: (0, 0), pipeline_mode=pl.Buffered(1))],
        out_specs=[pl.BlockSpec((TOP_K, tt), lambda p, i: (0, i * p)),
                   pl.BlockSpec((N_EXPERTS, LANES), lambda p, i: (0, 0))],
        out_shape=[jax.ShapeDtypeStruct((TOP_K, t), jnp.int32),
                   jax.ShapeDtypeStruct((N_EXPERTS, LANES), jnp.int32)],
        scratch_shapes=[pltpu.VMEM((N_EXPERTS, LANES), F32),
                        pltpu.VMEM((N_EXPERTS, LANES), F32)],
        compiler_params=pltpu.CompilerParams(dimension_semantics=("arbitrary", "arbitrary")),
        name="rank",
    )(top_i, upper)


def _plan_kernel(cnt_ref, tile_ref, exp_ref, lo_ref, first_ref, nwork_ref):
    n_items = tile_ref.shape[0]

    def per_expert(e, carry):
        w, start = carry
        c = cnt_ref[e]
        end = start + c
        t0 = start // ROW_TILE
        n = jnp.where(c > 0, (end - 1) // ROW_TILE - t0 + 1, 0)

        def per_tile(j, _):
            tile = t0 + j
            tile_ref[w + j] = tile
            exp_ref[w + j] = e
            lo_ref[w + j] = jnp.maximum(start - tile * ROW_TILE, 0)
            first_ref[w + j] = (j == 0).astype(jnp.int32)
            return 0

        lax.fori_loop(0, n, per_tile, 0)
        return w + n, end

    n_work, _ = lax.fori_loop(0, N_EXPERTS, per_expert, (jnp.int32(0), jnp.int32(0)))
    nwork_ref[0] = n_work

    def pad(w, _):
        tile_ref[w] = tile_ref[n_work - 1]
        exp_ref[w] = exp_ref[n_work - 1]
        lo_ref[w] = 0
        first_ref[w] = 0
        return 0

    lax.fori_loop(n_work, n_items, pad, 0)


def _plan(counts, n_items):
    smem = pl.BlockSpec(memory_space=pltpu.SMEM)
    vec = jax.ShapeDtypeStruct((n_items,), jnp.int32)
    return pl.pallas_call(
        _plan_kernel,
        in_specs=[smem],
        out_specs=[smem, smem, smem, smem, smem],
        out_shape=[vec, vec, vec, vec, jax.ShapeDtypeStruct((1,), jnp.int32)],
        name="plan",
    )(counts)


def _scatter(h2, dest):
    t, d = h2.shape
    per_worker = t // SC_WORKERS
    n = SC_SCATTER_CHUNK
    nch = per_worker // n
    assert nch % 2 == 0
    mesh = plsc.VectorSubcoreMesh(core_axis_name="c", subcore_axis_name="s")

    @functools.partial(
        pl.kernel, mesh=mesh,
        out_type=jax.ShapeDtypeStruct((t * TOP_K, d), h2.dtype),
        scratch_types=[pltpu.VMEM((TOP_K, nch, n), jnp.int32), pltpu.VMEM((2, n, d), h2.dtype),
                       pltpu.SemaphoreType.DMA((2,)), pltpu.SemaphoreType.DMA((2,))],
        name="sc_scatter")
    def scatter_kernel(h_hbm, dest_hbm, xs_hbm, idx_v, rows_v, sem_in, sem_out):
        wid = lax.axis_index("s") * SC_CORES + lax.axis_index("c")
        base = wid * per_worker
        for k in range(TOP_K):
            pltpu.sync_copy(dest_hbm.at[k, pl.ds(wid * nch, nch)], idx_v.at[k])

        def load(c, s):
            return pltpu.make_async_copy(h_hbm.at[pl.ds(base + c * n, n)], rows_v.at[s],
                                         sem_in.at[s])

        def store(c, s, k):
            return pltpu.make_async_copy(rows_v.at[s], xs_hbm.at[idx_v.at[k, c]], sem_out.at[s])

        load(0, 0).start()

        @pl.loop(0, nch, step=2)
        def _(c):
            for s in range(2):
                cc = c + s

                @pl.when(cc >= 1)
                def _():
                    for k in range(TOP_K):
                        store(cc - 1, 1 - s, k).wait()

                @pl.when(cc + 1 < nch)
                def _():
                    load(cc + 1, 1 - s).start()

                load(cc, s).wait()
                for k in range(TOP_K):
                    store(cc, s, k).start()

        for k in range(TOP_K):
            store(nch - 1, (nch - 1) % 2, k).wait()

    return scatter_kernel(h2, dest.reshape(TOP_K, t // n, n))


def _expert_kernel(tile_ref, exp_ref, lo_ref, first_ref, nwork_ref,
                   xs_ref, wgu_ref, bgu_ref, wd_ref, bd_ref, y_ref, wgu_bf, wd_bf):
    w = pl.program_id(0)

    @pl.when(first_ref[w] == 1)
    def _():
        wgu_bf[...] = wgu_ref[0].astype(BF16)
        wd_bf[...] = wd_ref[0].astype(BF16)

    @pl.when(w < nwork_ref[0])
    def _():
        gu = _dot(_unpack_bf16_pairs(xs_ref[...]), wgu_bf[...]) + bgu_ref[0]
        gate = jnp.minimum(gu[:, :D_FF], SWIGLU_LIMIT)
        up = jnp.clip(gu[:, D_FF:], -SWIGLU_LIMIT, SWIGLU_LIMIT)
        act = (up + 1.0) * (gate * jax.nn.sigmoid(gate * SWIGLU_ALPHA))
        y = _pack_bf16_pairs(_dot(act.astype(BF16), wd_bf[...]) + bd_ref[0])
        lo = lo_ref[w]

        @pl.when(lo == 0)
        def _():
            y_ref[...] = y

        @pl.when(lo > 0)
        def _():
            row = lax.broadcasted_iota(jnp.int32, y.shape, 0)
            y_ref[...] = jnp.where(row >= lo, y, y_ref[...])


def _experts(xs, plan, wgu, bgu, wd, bd):
    n_rows, dp = xs.shape
    d = 2 * dp
    w_tile, w_exp, w_lo, w_first, n_work = plan
    n_items = w_tile.shape[0]
    tm = ROW_TILE
    grid_spec = pltpu.PrefetchScalarGridSpec(
        num_scalar_prefetch=5,
        grid=(n_items,),
        in_specs=[
            pl.BlockSpec((tm, dp), lambda w, tl, ex, lo, fi, nw: (tl[w], 0)),
            pl.BlockSpec((1, d, 2 * D_FF), lambda w, tl, ex, lo, fi, nw: (ex[w], 0, 0)),
            pl.BlockSpec((1, 1, 2 * D_FF), lambda w, tl, ex, lo, fi, nw: (ex[w], 0, 0)),
            pl.BlockSpec((1, D_FF, d), lambda w, tl, ex, lo, fi, nw: (ex[w], 0, 0)),
            pl.BlockSpec((1, 1, d), lambda w, tl, ex, lo, fi, nw: (ex[w], 0, 0)),
        ],
        out_specs=pl.BlockSpec((tm, dp), lambda w, tl, ex, lo, fi, nw: (tl[w], 0)),
        scratch_shapes=[pltpu.VMEM((d, 2 * D_FF), BF16), pltpu.VMEM((D_FF, d), BF16)],
    )
    return pl.pallas_call(
        _expert_kernel,
        grid_spec=grid_spec,
        out_shape=jax.ShapeDtypeStruct((n_rows, dp), jnp.int32),
        compiler_params=pltpu.CompilerParams(
            dimension_semantics=("arbitrary",),
            vmem_limit_bytes=VMEM_BYTES_V7X * 7 // 8),
        name="experts",
    )(w_tile, w_exp, w_lo, w_first, n_work, xs, wgu, bgu, wd, bd)


def _combine(x1, probs, dest, y):
    t, d = x1.shape
    per_worker = t // SC_WORKERS
    dp = d // 2
    n = SC_COMBINE_CHUNK
    nch = per_worker // n
    assert nch % 2 == 0
    probs_e = jnp.repeat(probs.T, SC_LANES, axis=1)
    mesh = plsc.VectorSubcoreMesh(core_axis_name="c", subcore_axis_name="s")
    high_half = -65536

    @functools.partial(
        pl.kernel, mesh=mesh,
        out_type=jax.ShapeDtypeStruct((t, d), F32),
        scratch_types=[pltpu.VMEM((TOP_K, per_worker), jnp.int32),
                       pltpu.VMEM((2, TOP_K, n, dp), jnp.int32),
                       pltpu.VMEM((2, n, d), F32), pltpu.VMEM((2, n, d), F32),
                       pltpu.VMEM((2, n, TOP_K * SC_LANES), F32),
                       pltpu.SemaphoreType.DMA((2,)), pltpu.SemaphoreType.DMA((2,))],
        compiler_params=pltpu.CompilerParams(needs_layout_passes=False),
        name="sc_combine")
    def combine_kernel(x1_hbm, p_hbm, dest_hbm, y_hbm, out_hbm,
                       idx_v, rows_v, x_v, o_v, p_v, sem_in, sem_out):
        wid = lax.axis_index("s") * SC_CORES + lax.axis_index("c")
        base = wid * per_worker
        for k in range(TOP_K):
            pltpu.sync_copy(dest_hbm.at[k, pl.ds(base, per_worker)], idx_v.at[k])

        def loads(c, s):
            off = base + c * n
            cps = [pltpu.make_async_copy(y_hbm.at[idx_v.at[k, pl.ds(c * n, n)]], rows_v.at[s, k], sem_in.at[s])
                   for k in range(TOP_K)]
            cps.append(pltpu.make_async_copy(x1_hbm.at[pl.ds(off, n)], x_v.at[s], sem_in.at[s]))
            cps.append(pltpu.make_async_copy(p_hbm.at[pl.ds(off, n)], p_v.at[s], sem_in.at[s]))
            return cps

        def store(c, s):
            return pltpu.make_async_copy(o_v.at[s], out_hbm.at[pl.ds(base + c * n, n)],
                                         sem_out.at[s])

        for cp in loads(0, 0):
            cp.start()

        @pl.loop(0, nch, step=2)
        def _(c):
            for s in range(2):
                cc = c + s

                @pl.when(cc + 1 < nch)
                def _():
                    for cp in loads(cc + 1, 1 - s):
                        cp.start()

                for cp in loads(cc, s):
                    cp.wait()

                @pl.when(cc >= 2)
                def _():
                    store(cc - 2, s).wait()

                @pl.loop(0, n)
                def _(r):
                    w = [p_v[s, r, pl.ds(k * SC_LANES, SC_LANES)] for k in range(TOP_K)]
                    for g in range(dp // SC_LANES):
                        lo_sl = pl.ds(g * SC_LANES, SC_LANES)
                        hi_sl = pl.ds(dp + g * SC_LANES, SC_LANES)
                        lo = x_v[s, r, lo_sl]
                        hi = x_v[s, r, hi_sl]
                        for k in range(TOP_K):
                            word = rows_v[s, k, r, lo_sl]
                            lo = lo + w[k] * plsc.bitcast(word << 16, F32)
                            hi = hi + w[k] * plsc.bitcast(word & high_half, F32)
                        o_v[s, r, lo_sl] = lo
                        o_v[s, r, hi_sl] = hi

                store(cc, s).start()

        store(nch - 2, 0).wait()
        store(nch - 1, 1).wait()

    return combine_kernel(x1, probs_e, dest, y)


def _layer(x, norm1_g, w_in, q_norm_g, k_norm_g, rel_bias, sg_ln_g, sg_ln_b, sg_w, sg_b,
           w_branch_a, w_branch_b, w_out, norm2_g, router_w, router_b,
           w_gate_up, b_gate_up, w_down, b_down):
    bsz, seq, d = x.shape
    t = bsz * seq
    row = lambda v: v.reshape(1, -1).astype(F32)
    scale = 1.0 / math.sqrt(HEAD_DIM)
    head_sum = jnp.asarray(np.kron(np.eye(N_HEADS), np.ones((HEAD_DIM, HEAD_DIM))), BF16)
    sgb = jnp.repeat(sg_b.T.astype(F32), SG_WIDTH // SG_GROUPS, axis=1)

    x1, h2, top_i, probs = _mixer(
        x, row(norm1_g), w_in.astype(BF16), w_in[:, _C_V:_C_V + ATTN_WIDTH].T.astype(BF16),
        row(jnp.tile(q_norm_g, N_HEADS)) * scale, row(jnp.tile(k_norm_g, N_HEADS)),
        head_sum, _attention_bias(rel_bias), row(sg_ln_g), row(sg_ln_b),
        sg_w.astype(F32), sgb, w_branch_a.astype(BF16), w_branch_b.astype(BF16),
        w_out.astype(BF16), row(norm2_g), router_w.T.astype(F32),
        router_b.reshape(N_EXPERTS, 1).astype(F32))
    x1 = x1.reshape(t, d)
    h2 = h2.reshape(t, d // 2)

    dest, counts = _rank(top_i)
    n_items = t * TOP_K // ROW_TILE + N_EXPERTS
    plan = _plan(counts[:, 0], n_items)
    xs = _scatter(h2, dest)
    y = _experts(xs, plan, w_gate_up.astype(F32), b_gate_up.reshape(N_EXPERTS, 1, -1).astype(F32),
                 w_down.astype(F32), b_down.reshape(N_EXPERTS, 1, -1).astype(F32))
    out = _combine(x1, probs, dest, y)
    return out.reshape(bsz, seq, d)


def kernel(x, norm1_g, w_in, q_norm_g, k_norm_g, rel_bias, sg_ln_g, sg_ln_b, sg_w, sg_b,
           w_branch_a, w_branch_b, w_out, norm2_g, router_w, router_b,
           w_gate_up, b_gate_up, w_down, b_down):
    depth = norm1_g.shape[0]
    for l in range(depth):
        x = _layer(x, norm1_g[l], w_in[l], q_norm_g[l], k_norm_g[l], rel_bias[l], sg_ln_g[l],
                   sg_ln_b[l], sg_w[l], sg_b[l], w_branch_a[l], w_branch_b[l], w_out[l],
                   norm2_g[l], router_w[l], router_b[l], w_gate_up[l], b_gate_up[l],
                   w_down[l], b_down[l])
    return x
```

```python
import functools
import math

import numpy as np
import jax
import jax.numpy as jnp
from jax import lax
from jax.experimental import pallas as pl
from jax.experimental.pallas import tpu as pltpu
from jax.experimental.pallas import tpu_sc as plsc

F32 = jnp.float32
BF16 = jnp.bfloat16

D_MODEL = 1024
CHUNK = 64
N_BACK = 8
BAND = (N_BACK + 1) * CHUNK
N_HEADS = 8
HEAD_DIM = 64
ATTN_WIDTH = N_HEADS * HEAD_DIM
REL_CLIP = 256
SG_BLOCK = 128
SG_GROUPS = 4
SG_WIDTH = 512
IN_WIDTH = 3 * ATTN_WIDTH + 2 * SG_WIDTH + 2 * D_MODEL
N_EXPERTS = 32
TOP_K = 4
D_FF = D_MODEL
SWIGLU_LIMIT = 7.0
SWIGLU_ALPHA = 1.702
EPS = 1e-6
NEG = -1e30

LANES = 128
VMEM_BYTES_V7X = 64 * 1024 * 1024

SEQ_TILE = 512
Q_BLOCK = 2 * CHUNK
PREV = N_BACK * CHUNK
WIN = PREV + Q_BLOCK
GATE_CHUNK = 256
FINISH_ROWS = SEQ_TILE
RANK_TILE = 512
ROW_TILE = 512

SC_CORES = 2
SC_SUBCORES = 16
SC_LANES = 16
SC_WORKERS = SC_CORES * SC_SUBCORES
SC_SCATTER_CHUNK = 64
SC_COMBINE_CHUNK = 8

_C_Q = 0
_C_K = ATTN_WIDTH
_C_V = 2 * ATTN_WIDTH
_C_U = 3 * ATTN_WIDTH
_C_VS = _C_U + SG_WIDTH
_C_GA = _C_VS + SG_WIDTH
_C_GB = _C_GA + D_MODEL


def _dot(a, b):
    return jnp.dot(a, b, preferred_element_type=F32)


def _pack_bf16_pairs(v):
    w = v.shape[1] // 2
    lo = lax.bitcast_convert_type(v[:, :w].astype(BF16).astype(F32), jnp.uint32)
    hi = lax.bitcast_convert_type(v[:, w:].astype(BF16).astype(F32), jnp.uint32)
    return lax.bitcast_convert_type((lo >> 16) | hi, jnp.int32)


def _unpack_bf16_pairs(p):
    u = lax.bitcast_convert_type(p, jnp.uint32)
    lo = lax.bitcast_convert_type(u << 16, F32)
    hi = lax.bitcast_convert_type(u & jnp.uint32(0xFFFF0000), F32)
    return jnp.concatenate([lo, hi], axis=1).astype(BF16)


def _dot_nt(a, b, precision=None):
    return lax.dot_general(a, b, (((1,), (1,)), ((), ())),
                           preferred_element_type=F32, precision=precision)


def _mixer_kernel(x_ref, g1_ref, win_ref, wvt_ref, qg_ref, kg_ref, hsum_ref, bias_ref, lng_ref,
                  lnb_ref, sgw_ref, sgb_ref, wa_ref, wb_ref, wo_ref, g2_ref, rwt_ref, rb_ref,
                  x1_ref, h2_ref, ti_ref, pr_ref,
                  h_scr, q_scr, k_win, vt_win, ya_scr, u_scr, vs_scr, ysg_scr, ga_scr, mb_scr):
    ts = x_ref.shape[1]
    s_idx = pl.program_id(1)

    @pl.when(s_idx == 0)
    def _():
        k_win[0:PREV, :] = jnp.zeros((PREV, ATTN_WIDTH), BF16)
        vt_win[:, 0:PREV] = jnp.zeros((ATTN_WIDTH, PREV), BF16)

    @pl.when(s_idx > 0)
    def _():
        k_win[0:PREV, :] = k_win[ts:ts + PREV, :]
        vt_win[:, 0:PREV] = vt_win[:, ts:ts + PREV]

    x = x_ref[0]
    ms = jnp.mean(x * x, axis=-1, keepdims=True)
    h_scr[...] = (x * lax.rsqrt(ms + EPS) * g1_ref[...]).astype(BF16)

    def proj(lo, width):
        return _dot(h_scr[...], win_ref[:, lo:lo + width])

    def head_rms(t, g_ref):
        ssq = _dot((t * t).astype(BF16), hsum_ref[...])
        return t * lax.rsqrt(ssq * (1.0 / HEAD_DIM) + EPS) * g_ref[...]

    lane = lax.broadcasted_iota(jnp.int32, (1, ATTN_WIDTH), 1)
    even_head = (lane % (2 * HEAD_DIM)) < HEAD_DIM
    qn = head_rms(proj(_C_Q, ATTN_WIDTH), qg_ref)
    q_even = jnp.where(even_head, qn, 0.0).astype(BF16)
    q_odd = jnp.where(even_head, 0.0, qn).astype(BF16)
    for qb in range(ts // Q_BLOCK):
        rs = slice(qb * Q_BLOCK, (qb + 1) * Q_BLOCK)
        q_scr[qb, 0:Q_BLOCK, :] = q_even[rs]
        q_scr[qb, Q_BLOCK:2 * Q_BLOCK, :] = q_odd[rs]
    k_win[PREV:PREV + ts, :] = head_rms(proj(_C_K, ATTN_WIDTH), kg_ref).astype(BF16)
    vt_win[:, PREV:PREV + ts] = _dot_nt(wvt_ref[...], h_scr[...]).astype(BF16)

    def sg_u():
        u_scr[...] = jax.nn.gelu(proj(_C_U, SG_WIDTH))

    def sg_v():
        vs_scr[...] = jax.nn.gelu(proj(_C_VS, SG_WIDTH))

    gdim = SG_WIDTH // SG_GROUPS

    def sg_group(g):
        cs = slice(g * gdim, (g + 1) * gdim)
        vg = vs_scr[:, cs]
        mu = jnp.mean(vg, axis=-1, keepdims=True)
        xc = vg - mu
        var = jnp.mean(xc * xc, axis=-1, keepdims=True)
        vn = (xc * lax.rsqrt(var + EPS) * lng_ref[:, cs] + lnb_ref[:, cs]).astype(BF16)
        r_i = lax.broadcasted_iota(jnp.int32, (SG_BLOCK, SG_BLOCK), 0)
        c_i = lax.broadcasted_iota(jnp.int32, (SG_BLOCK, SG_BLOCK), 1)
        wm = jnp.where(r_i >= c_i, sgw_ref[g], 0.0).astype(BF16)
        for j in range(ts // SG_BLOCK):
            rs = slice(j * SG_BLOCK, (j + 1) * SG_BLOCK)
            mixed = _dot(wm, vn[rs]) + sgb_ref[:, cs]
            ysg_scr[rs, cs] = (u_scr[rs, cs] * mixed).astype(BF16)

    def gate_a_chunk(c):
        cols = slice(c * GATE_CHUNK, (c + 1) * GATE_CHUNK)
        ga_scr[:, cols] = jax.nn.sigmoid(proj(_C_GA + c * GATE_CHUNK, GATE_CHUNK))

    def gated_b_chunk(c):
        cols = slice(c * GATE_CHUNK, (c + 1) * GATE_CHUNK)
        gate = jax.nn.sigmoid(proj(_C_GB + c * GATE_CHUNK, GATE_CHUNK))
        mb_scr[:, cols] = gate * _dot(ysg_scr[...], wb_ref[:, cols])

    n_chunks = D_MODEL // GATE_CHUNK
    side_work = [sg_u, sg_v] + [functools.partial(sg_group, g) for g in range(SG_GROUPS)]
    side_work += [functools.partial(gate_a_chunk, c) for c in range(n_chunks)]
    side_work += [functools.partial(gated_b_chunk, c) for c in range(n_chunks)]

    def finish(grp):
        rs = slice(grp * FINISH_ROWS, (grp + 1) * FINISH_ROWS)
        merged = ga_scr[rs, :] * _dot(ya_scr[rs, :], wa_ref[...]) + mb_scr[rs, :]
        x1 = x_ref[0, rs, :] + _dot(merged.astype(BF16), wo_ref[...])
        x1_ref[0, rs, :] = x1
        ms2 = jnp.mean(x1 * x1, axis=-1, keepdims=True)
        h2 = x1 * lax.rsqrt(ms2 + EPS) * g2_ref[...]
        h2_ref[0, rs, :] = _pack_bf16_pairs(h2)
        lt = _dot_nt(rwt_ref[...], h2, precision=lax.Precision.HIGHEST) + rb_ref[...]
        e_iota = lax.broadcasted_iota(jnp.int32, lt.shape, 0)
        vals = []
        for k in range(TOP_K):
            m = jnp.max(lt, axis=0, keepdims=True)
            idx = jnp.min(jnp.where(lt == m, e_iota, N_EXPERTS), axis=0, keepdims=True)
            vals.append(m)
            ti_ref[k:k + 1, rs] = idx
            lt = jnp.where(e_iota == idx, -jnp.inf, lt)
        exps = [jnp.exp(v - vals[0]) for v in vals]
        denom = exps[0] + exps[1] + exps[2] + exps[3]
        for k in range(TOP_K):
            pr_ref[k:k + 1, rs] = exps[k] / denom

    pairs = N_HEADS // 2
    n_qb = ts // Q_BLOCK
    n_blocks = n_qb * pairs
    schedule = [[] for _ in range(n_blocks)]
    side_blocks = n_blocks if FINISH_ROWS == ts else n_blocks // 2
    for i, piece in enumerate(side_work):
        schedule[-(-(i + 1) * side_blocks // len(side_work)) - 1].append(piece)
    qb_per_group = FINISH_ROWS // Q_BLOCK
    for grp in range(ts // FINISH_ROWS):
        ready = max(side_blocks, (grp + 1) * qb_per_group * pairs) - 1
        schedule[min(n_blocks - 1, ready + 1)].append(functools.partial(finish, grp))

    key_row = lax.broadcasted_iota(jnp.int32, (WIN, 1), 0)
    for qb in range(ts // Q_BLOCK):
        r0 = qb * Q_BLOCK
        valid = key_row >= (PREV - (s_idx * ts + r0))
        for pair in range(N_HEADS // 2):
            cs = slice(pair * 2 * HEAD_DIM, (pair + 1) * 2 * HEAD_DIM)
            st = _dot_nt(k_win[r0:r0 + WIN, cs], q_scr[qb, :, cs]) + bias_ref[pair]
            st = jnp.where(valid, st, NEG)
            m = jnp.max(st, axis=0, keepdims=True)
            p = jnp.exp(st - m)
            inv = 1.0 / jnp.sum(p, axis=0, keepdims=True)
            ot = _dot(vt_win[cs, r0:r0 + WIN], p.astype(BF16))
            o = jnp.concatenate(
                [ot[0:HEAD_DIM, 0:Q_BLOCK] * inv[:, 0:Q_BLOCK],
                 ot[HEAD_DIM:2 * HEAD_DIM, Q_BLOCK:2 * Q_BLOCK] * inv[:, Q_BLOCK:2 * Q_BLOCK]],
                axis=0)
            ya_scr[r0:r0 + Q_BLOCK, cs] = o.T.astype(BF16)
            for piece in schedule[qb * (N_HEADS // 2) + pair]:
                piece()


def _const_spec(shape):
    zeros = (0,) * len(shape)
    return pl.BlockSpec(shape, lambda b, s: zeros, pipeline_mode=pl.Buffered(1))


def _mixer(x, g1, win, wvt, qg, kg, hsum, bias, lng, lnb, sgw, sgb, wa, wb, wo, g2, rwt, rb):
    bsz, seq, d = x.shape
    ts = SEQ_TILE
    ns = seq // ts
    t = bsz * seq
    consts = (g1, win, wvt, qg, kg, hsum, bias, lng, lnb, sgw, sgb, wa, wb, wo, g2, rwt, rb)
    tok_spec = pl.BlockSpec((1, ts, d), lambda b, s: (b, s, 0))
    packed_spec = pl.BlockSpec((1, ts, d // 2), lambda b, s: (b, s, 0))
    idx_spec = pl.BlockSpec((TOP_K, ts), lambda b, s: (0, b * ns + s))
    return pl.pallas_call(
        _mixer_kernel,
        grid=(bsz, ns),
        in_specs=[tok_spec] + [_const_spec(c.shape) for c in consts],
        out_specs=[tok_spec, packed_spec, idx_spec, idx_spec],
        out_shape=[
            jax.ShapeDtypeStruct((bsz, seq, d), F32),
            jax.ShapeDtypeStruct((bsz, seq, d // 2), jnp.int32),
            jax.ShapeDtypeStruct((TOP_K, t), jnp.int32),
            jax.ShapeDtypeStruct((TOP_K, t), F32),
        ],
        scratch_shapes=[
            pltpu.VMEM((ts, d), BF16),
            pltpu.VMEM((ts // Q_BLOCK, 2 * Q_BLOCK, ATTN_WIDTH), BF16),
            pltpu.VMEM((PREV + ts, ATTN_WIDTH), BF16),
            pltpu.VMEM((ATTN_WIDTH, PREV + ts), BF16),
            pltpu.VMEM((ts, ATTN_WIDTH), BF16),
            pltpu.VMEM((ts, SG_WIDTH), F32),
            pltpu.VMEM((ts, SG_WIDTH), F32),
            pltpu.VMEM((ts, SG_WIDTH), BF16),
            pltpu.VMEM((ts, d), F32),
            pltpu.VMEM((ts, d), F32),
        ],
        compiler_params=pltpu.CompilerParams(
            dimension_semantics=("arbitrary", "arbitrary"),
            vmem_limit_bytes=VMEM_BYTES_V7X * 7 // 8),
        name="mixer",
    )(x, *consts)


def _attention_bias(rel_bias):
    i = np.arange(Q_BLOCK)[:, None]
    j = np.arange(WIN)[None, :]
    kk = j - (i // CHUNK) * CHUNK
    inside = (kk >= 0) & (kk < BAND)
    tab = rel_bias.astype(F32)
    far = PREV + Q_BLOCK - 1 - REL_CLIP + 1
    near = Q_BLOCK - 1 + REL_CLIP
    ext = jnp.concatenate(
        [jnp.broadcast_to(tab[:, 2 * REL_CLIP:], (N_HEADS, far)),
         tab[:, 2 * REL_CLIP - near:2 * REL_CLIP][:, ::-1]], axis=1)
    rows = [ext[:, Q_BLOCK - 1 - r:Q_BLOCK - 1 - r + WIN] for r in range(Q_BLOCK)]
    bias = jnp.where(inside[None], jnp.stack(rows, axis=1), NEG)
    return (bias.reshape(N_HEADS // 2, 2 * Q_BLOCK, WIN).transpose(0, 2, 1))


def _rank_kernel(ti_ref, upper_ref, dest_ref, cnt_ref, total_scr, base_scr):
    phase = pl.program_id(0)
    i = pl.program_id(1)
    tt = ti_ref.shape[1]

    @pl.when((phase == 0) & (i == 0))
    def _():
        total_scr[...] = jnp.zeros_like(total_scr)

    ti = ti_ref[...]
    e_iota = lax.broadcasted_iota(jnp.int32, (N_EXPERTS, tt), 0)
    hits = [e_iota == ti[k:k + 1, :] for k in range(TOP_K)]
    cnt = hits[0].astype(F32)
    for k in range(1, TOP_K):
        cnt = cnt + hits[k].astype(F32)
    tile_total = jnp.sum(cnt, axis=1, keepdims=True)

    @pl.when(phase == 0)
    def _():
        total_scr[...] = total_scr[...] + tile_total

    @pl.when((phase == 1) & (i == 0))
    def _():
        cnt_ref[...] = total_scr[...].astype(jnp.int32)
        run = jnp.zeros((1, LANES), F32)
        for e in range(N_EXPERTS):
            base_scr[e:e + 1, :] = run
            run = run + total_scr[e:e + 1, :]

    @pl.when(phase == 1)
    def _():
        before = _dot(cnt.astype(BF16), upper_ref[...]) + base_scr[:, 0:1]
        for k in range(TOP_K):
            dest_ref[k:k + 1, :] = jnp.sum(jnp.where(hits[k], before, 0.0), axis=0,
                                           keepdims=True).astype(jnp.int32)
        base_scr[...] = base_scr[...] + tile_total


def _rank(top_i):
    t = top_i.shape[1]
    tt = RANK_TILE
    upper = jnp.asarray(np.triu(np.ones((tt, tt), np.float32), 1), BF16)
    return pl.pallas_call(
        _rank_kernel,
        grid=(2, t // tt),
        in_specs=[pl.BlockSpec((TOP_K, tt), lambda p, i: (0, i)),
                  pl.BlockSpec((tt, tt), lambda p, i: (0, 0), pipeline_mode=pl.Buffered(1))],
        out_specs=[pl.BlockSpec((TOP_K, tt), lambda p, i: (0, i * p)),
                   pl.BlockSpec((N_EXPERTS, LANES), lambda p, i: (0, 0))],
        out_shape=[jax.ShapeDtypeStruct((TOP_K, t), jnp.int32),
                   jax.ShapeDtypeStruct((N_EXPERTS, LANES), jnp.int32)],
        scratch_shapes=[pltpu.VMEM((N_EXPERTS, LANES), F32),
                        pltpu.VMEM((N_EXPERTS, LANES), F32)],
        compiler_params=pltpu.CompilerParams(dimension_semantics=("arbitrary", "arbitrary")),
        name="rank",
    )(top_i, upper)


def _plan_kernel(cnt_ref, tile_ref, exp_ref, lo_ref, hi_ref, first_ref):
    n_items = tile_ref.shape[0]

    def per_expert(e, carry):
        w, start = carry
        c = cnt_ref[e]
        end = start + c
        t0 = start // ROW_TILE
        n = jnp.where(c > 0, (end - 1) // ROW_TILE - t0 + 1, 0)

        def per_tile(j, _):
            tile = t0 + j
            tile_ref[w + j] = tile
            exp_ref[w + j] = e
            lo_ref[w + j] = jnp.maximum(start - tile * ROW_TILE, 0)
            hi_ref[w + j] = jnp.minimum(end - tile * ROW_TILE, ROW_TILE)
            first_ref[w + j] = (j == 0).astype(jnp.int32)
            return 0

        lax.fori_loop(0, n, per_tile, 0)
        return w + n, end

    n_work, _ = lax.fori_loop(0, N_EXPERTS, per_expert, (jnp.int32(0), jnp.int32(0)))

    def pad(w, _):
        tile_ref[w] = tile_ref[n_work - 1]
        exp_ref[w] = exp_ref[n_work - 1]
        lo_ref[w] = 0
        hi_ref[w] = 0
        first_ref[w] = 0
        return 0

    lax.fori_loop(n_work, n_items, pad, 0)


def _plan(counts, n_items):
    smem = pl.BlockSpec(memory_space=pltpu.SMEM)
    vec = jax.ShapeDtypeStruct((n_items,), jnp.int32)
    return pl.pallas_call(
        _plan_kernel,
        in_specs=[smem],
        out_specs=[smem, smem, smem, smem, smem],
        out_shape=[vec, vec, vec, vec, vec],
        name="plan",
    )(counts)


def _scatter(h2, dest):
    t, d = h2.shape
    per_worker = t // SC_WORKERS
    n = SC_SCATTER_CHUNK
    nch = per_worker // n
    assert nch % 2 == 0
    mesh = plsc.VectorSubcoreMesh(core_axis_name="c", subcore_axis_name="s")

    @functools.partial(
        pl.kernel, mesh=mesh,
        out_type=jax.ShapeDtypeStruct((t * TOP_K, d), h2.dtype),
        scratch_types=[pltpu.VMEM((TOP_K, nch, n), jnp.int32), pltpu.VMEM((2, n, d), h2.dtype),
                       pltpu.SemaphoreType.DMA((2,)), pltpu.SemaphoreType.DMA((2,))],
        name="sc_scatter")
    def scatter_kernel(h_hbm, dest_hbm, xs_hbm, idx_v, rows_v, sem_in, sem_out):
        wid = lax.axis_index("s") * SC_CORES + lax.axis_index("c")
        base = wid * per_worker
        for k in range(TOP_K):
            pltpu.sync_copy(dest_hbm.at[k, pl.ds(wid * nch, nch)], idx_v.at[k])

        def load(c, s):
            return pltpu.make_async_copy(h_hbm.at[pl.ds(base + c * n, n)], rows_v.at[s],
                                         sem_in.at[s])

        def store(c, s, k):
            return pltpu.make_async_copy(rows_v.at[s], xs_hbm.at[idx_v.at[k, c]], sem_out.at[s])

        load(0, 0).start()

        @pl.loop(0, nch, step=2)
        def _(c):
            for s in range(2):
                cc = c + s

                @pl.when(cc >= 1)
                def _():
                    for k in range(TOP_K):
                        store(cc - 1, 1 - s, k).wait()

                @pl.when(cc + 1 < nch)
                def _():
                    load(cc + 1, 1 - s).start()

                load(cc, s).wait()
                for k in range(TOP_K):
                    store(cc, s, k).start()

        for k in range(TOP_K):
            store(nch - 1, (nch - 1) % 2, k).wait()

    return scatter_kernel(h2, dest.reshape(TOP_K, t // n, n))


def _expert_kernel(tile_ref, exp_ref, lo_ref, hi_ref, first_ref,
                   xs_ref, wgu_ref, bgu_ref, wd_ref, bd_ref, y_ref, wgu_bf, wd_bf):
    w = pl.program_id(0)

    @pl.when(first_ref[w] == 1)
    def _():
        wgu_bf[...] = wgu_ref[0].astype(BF16)
        wd_bf[...] = wd_ref[0].astype(BF16)

    lo = lo_ref[w]
    hi = hi_ref[w]
    half = xs_ref.shape[0] // 2

    for r0 in (0, half):
        rows = slice(r0, r0 + half)

        @pl.when((lo < r0 + half) & (hi > r0))
        def _():
            gu = _dot(_unpack_bf16_pairs(xs_ref[rows, :]), wgu_bf[...]) + bgu_ref[0]
            gate = jnp.minimum(gu[:, :D_FF], SWIGLU_LIMIT)
            up = jnp.clip(gu[:, D_FF:], -SWIGLU_LIMIT, SWIGLU_LIMIT)
            act = (up + 1.0) * (gate * jax.nn.sigmoid(gate * SWIGLU_ALPHA))
            y = _pack_bf16_pairs(_dot(act.astype(BF16), wd_bf[...]) + bd_ref[0])

            @pl.when(lo <= r0)
            def _():
                y_ref[rows, :] = y

            @pl.when(lo > r0)
            def _():
                row = r0 + lax.broadcasted_iota(jnp.int32, y.shape, 0)
                y_ref[rows, :] = jnp.where(row >= lo, y, y_ref[rows, :])


def _experts(xs, plan, wgu, bgu, wd, bd):
    n_rows, dp = xs.shape
    d = 2 * dp
    w_tile, w_exp, w_lo, w_hi, w_first = plan
    n_items = w_tile.shape[0]
    tm = ROW_TILE
    by_tile = lambda w, tl, ex, lo, hi, fi: (tl[w], 0)
    by_expert = lambda w, tl, ex, lo, hi, fi: (ex[w], 0, 0)
    grid_spec = pltpu.PrefetchScalarGridSpec(
        num_scalar_prefetch=5,
        grid=(n_items,),
        in_specs=[
            pl.BlockSpec((tm, dp), by_tile),
            pl.BlockSpec((1, d, 2 * D_FF), by_expert),
            pl.BlockSpec((1, 1, 2 * D_FF), by_expert),
            pl.BlockSpec((1, D_FF, d), by_expert),
            pl.BlockSpec((1, 1, d), by_expert),
        ],
        out_specs=pl.BlockSpec((tm, dp), by_tile),
        scratch_shapes=[pltpu.VMEM((d, 2 * D_FF), BF16), pltpu.VMEM((D_FF, d), BF16)],
    )
    return pl.pallas_call(
        _expert_kernel,
        grid_spec=grid_spec,
        out_shape=jax.ShapeDtypeStruct((n_rows, dp), jnp.int32),
        compiler_params=pltpu.CompilerParams(
            dimension_semantics=("arbitrary",),
            vmem_limit_bytes=VMEM_BYTES_V7X * 7 // 8),
        name="experts",
    )(w_tile, w_exp, w_lo, w_hi, w_first, xs, wgu, bgu, wd, bd)


def _combine(x1, probs, dest, y):
    t, d = x1.shape
    per_worker = t // SC_WORKERS
    dp = d // 2
    n = SC_COMBINE_CHUNK
    nch = per_worker // n
    assert nch % 2 == 0
    mesh = plsc.VectorSubcoreMesh(core_axis_name="c", subcore_axis_name="s")
    high_half = -65536

    @functools.partial(
        pl.kernel, mesh=mesh,
        out_type=jax.ShapeDtypeStruct((t, d), F32),
        scratch_types=[pltpu.VMEM((TOP_K, per_worker), jnp.int32),
                       pltpu.VMEM((2, TOP_K, n, dp), jnp.int32),
                       pltpu.VMEM((2, n, d), F32), pltpu.VMEM((2, n, d), F32),
                       pltpu.VMEM((TOP_K, per_worker), F32),
                       pltpu.SemaphoreType.DMA((2,)), pltpu.SemaphoreType.DMA((2,))],
        compiler_params=pltpu.CompilerParams(needs_layout_passes=False),
        name="sc_combine")
    def combine_kernel(x1_hbm, p_hbm, dest_hbm, y_hbm, out_hbm,
                       idx_v, rows_v, x_v, o_v, p_v, sem_in, sem_out):
        wid = lax.axis_index("s") * SC_CORES + lax.axis_index("c")
        base = wid * per_worker
        for k in range(TOP_K):
            pltpu.sync_copy(dest_hbm.at[k, pl.ds(base, per_worker)], idx_v.at[k])
            pltpu.sync_copy(p_hbm.at[k, pl.ds(base, per_worker)], p_v.at[k])

        def loads(c, s):
            off = base + c * n
            cps = [pltpu.make_async_copy(y_hbm.at[idx_v.at[k, pl.ds(c * n, n)]], rows_v.at[s, k],
                                         sem_in.at[s])
                   for k in range(TOP_K)]
            cps.append(pltpu.make_async_copy(x1_hbm.at[pl.ds(off, n)], x_v.at[s], sem_in.at[s]))
            return cps

        def store(c, s):
            return pltpu.make_async_copy(o_v.at[s], out_hbm.at[pl.ds(base + c * n, n)],
                                         sem_out.at[s])

        for cp in loads(0, 0):
            cp.start()

        @pl.loop(0, nch, step=2)
        def _(c):
            for s in range(2):
                cc = c + s

                @pl.when(cc + 1 < nch)
                def _():
                    for cp in loads(cc + 1, 1 - s):
                        cp.start()

                for cp in loads(cc, s):
                    cp.wait()

                @pl.when(cc >= 2)
                def _():
                    store(cc - 2, s).wait()

                @pl.loop(0, n)
                def _(r):
                    tok = jnp.full((SC_LANES,), cc * n + r, jnp.int32)
                    w = [plsc.load_gather(p_v, [jnp.full((SC_LANES,), k, jnp.int32), tok])
                         for k in range(TOP_K)]
                    for g in range(dp // SC_LANES):
                        lo_sl = pl.ds(g * SC_LANES, SC_LANES)
                        hi_sl = pl.ds(dp + g * SC_LANES, SC_LANES)
                        lo = x_v[s, r, lo_sl]
                        hi = x_v[s, r, hi_sl]
                        for k in range(TOP_K):
                            word = rows_v[s, k, r, lo_sl]
                            lo = lo + w[k] * plsc.bitcast(word << 16, F32)
                            hi = hi + w[k] * plsc.bitcast(word & high_half, F32)
                        o_v[s, r, lo_sl] = lo
                        o_v[s, r, hi_sl] = hi

                store(cc, s).start()

        store(nch - 2, 0).wait()
        store(nch - 1, 1).wait()

    return combine_kernel(x1, probs, dest, y)


def _layer(x, norm1_g, w_in, q_norm_g, k_norm_g, rel_bias, sg_ln_g, sg_ln_b, sg_w, sg_b,
           w_branch_a, w_branch_b, w_out, norm2_g, router_w, router_b,
           w_gate_up, b_gate_up, w_down, b_down):
    bsz, seq, d = x.shape
    t = bsz * seq
    row = lambda v: v.reshape(1, -1).astype(F32)
    scale = 1.0 / math.sqrt(HEAD_DIM)
    head_sum = jnp.asarray(np.kron(np.eye(N_HEADS), np.ones((HEAD_DIM, HEAD_DIM))), BF16)
    sgb = jnp.repeat(sg_b.T.astype(F32), SG_WIDTH // SG_GROUPS, axis=1)

    x1, h2, top_i, probs = _mixer(
        x, row(norm1_g), w_in.astype(BF16), w_in[:, _C_V:_C_V + ATTN_WIDTH].T.astype(BF16),
        row(jnp.tile(q_norm_g, N_HEADS)) * scale, row(jnp.tile(k_norm_g, N_HEADS)),
        head_sum, _attention_bias(rel_bias), row(sg_ln_g), row(sg_ln_b),
        sg_w.astype(F32), sgb, w_branch_a.astype(BF16), w_branch_b.astype(BF16),
        w_out.astype(BF16), row(norm2_g), router_w.T.astype(F32),
        router_b.reshape(N_EXPERTS, 1).astype(F32))
    x1 = x1.reshape(t, d)
    h2 = h2.reshape(t, d // 2)

    dest, counts = _rank(top_i)
    n_items = t * TOP_K // ROW_TILE + N_EXPERTS
    plan = _plan(counts[:, 0], n_items)
    xs = _scatter(h2, dest)
    y = _experts(xs, plan, w_gate_up.astype(F32), b_gate_up.reshape(N_EXPERTS, 1, -1).astype(F32),
                 w_down.astype(F32), b_down.reshape(N_EXPERTS, 1, -1).astype(F32))
    out = _combine(x1, probs, dest, y)
    return out.reshape(bsz, seq, d)


def kernel(x, norm1_g, w_in, q_norm_g, k_norm_g, rel_bias, sg_ln_g, sg_ln_b, sg_w, sg_b,
           w_branch_a, w_branch_b, w_out, norm2_g, router_w, router_b,
           w_gate_up, b_gate_up, w_down, b_down):
    depth = norm1_g.shape[0]
    for l in range(depth):
        x = _layer(x, norm1_g[l], w_in[l], q_norm_g[l], k_norm_g[l], rel_bias[l], sg_ln_g[l],
                   sg_ln_b[l], sg_w[l], sg_b[l], w_branch_a[l], w_branch_b[l], w_out[l],
                   norm2_g[l], router_w[l], router_b[l], w_gate_up[l], b_gate_up[l],
                   w_down[l], b_down[l])
    return x
```

```python
import functools
import math

import numpy as np
import jax
import jax.numpy as jnp
from jax import lax
from jax.experimental import pallas as pl
from jax.experimental.pallas import tpu as pltpu
from jax.experimental.pallas import tpu_sc as plsc

F32 = jnp.float32
BF16 = jnp.bfloat16

D_MODEL = 1024
CHUNK = 64
N_BACK = 8
BAND = (N_BACK + 1) * CHUNK
N_HEADS = 8
HEAD_DIM = 64
ATTN_WIDTH = N_HEADS * HEAD_DIM
REL_CLIP = 256
SG_BLOCK = 128
SG_GROUPS = 4
SG_WIDTH = 512
IN_WIDTH = 3 * ATTN_WIDTH + 2 * SG_WIDTH + 2 * D_MODEL
N_EXPERTS = 32
TOP_K = 4
D_FF = D_MODEL
SWIGLU_LIMIT = 7.0
SWIGLU_ALPHA = 1.702
EPS = 1e-6
NEG = -1e30

LANES = 128
VMEM_BYTES_V7X = 64 * 1024 * 1024

SEQ_TILE = 512
Q_BLOCK = 2 * CHUNK
PREV = N_BACK * CHUNK
WIN = PREV + Q_BLOCK
GATE_CHUNK = 256
FINISH_ROWS = SEQ_TILE
RANK_TILE = 1024
ROW_TILE = 512

SC_CORES = 2
SC_SUBCORES = 16
SC_LANES = 16
SC_WORKERS = SC_CORES * SC_SUBCORES
SC_SCATTER_CHUNK = 64
SC_COMBINE_CHUNK = 8

_C_Q = 0
_C_K = ATTN_WIDTH
_C_V = 2 * ATTN_WIDTH
_C_U = 3 * ATTN_WIDTH
_C_VS = _C_U + SG_WIDTH
_C_GA = _C_VS + SG_WIDTH
_C_GB = _C_GA + D_MODEL


def _dot(a, b):
    return jnp.dot(a, b, preferred_element_type=F32)


def _pack_bf16_pairs(v):
    w = v.shape[1] // 2
    lo = lax.bitcast_convert_type(v[:, :w].astype(BF16).astype(F32), jnp.uint32)
    hi = lax.bitcast_convert_type(v[:, w:].astype(BF16).astype(F32), jnp.uint32)
    return lax.bitcast_convert_type((lo >> 16) | hi, jnp.int32)


def _unpack_bf16_pairs(p):
    u = lax.bitcast_convert_type(p, jnp.uint32)
    lo = lax.bitcast_convert_type(u << 16, F32)
    hi = lax.bitcast_convert_type(u & jnp.uint32(0xFFFF0000), F32)
    return jnp.concatenate([lo, hi], axis=1).astype(BF16)


def _dot_nt(a, b, precision=None):
    return lax.dot_general(a, b, (((1,), (1,)), ((), ())),
                           preferred_element_type=F32, precision=precision)


def _mixer_kernel(x_ref, g1_ref, win_ref, wvt_ref, qg_ref, kg_ref, hsum_ref, bias_ref, lng_ref,
                  lnb_ref, sgw_ref, sgb_ref, wa_ref, wb_ref, wo_ref, g2_ref, rwt_ref, rb_ref,
                  x1_ref, h2_ref, ti_ref, pr_ref,
                  h_scr, q_scr, k_win, vt_win, ya_scr, u_scr, vs_scr, ysg_scr, ga_scr, mb_scr):
    ts = x_ref.shape[1]
    s_idx = pl.program_id(1)

    @pl.when(s_idx == 0)
    def _():
        k_win[0:PREV, :] = jnp.zeros((PREV, ATTN_WIDTH), BF16)
        vt_win[:, 0:PREV] = jnp.zeros((ATTN_WIDTH, PREV), BF16)

    @pl.when(s_idx > 0)
    def _():
        k_win[0:PREV, :] = k_win[ts:ts + PREV, :]
        vt_win[:, 0:PREV] = vt_win[:, ts:ts + PREV]

    x = x_ref[0]
    ms = jnp.mean(x * x, axis=-1, keepdims=True)
    h_scr[...] = (x * lax.rsqrt(ms + EPS) * g1_ref[...]).astype(BF16)

    def proj(lo, width):
        return _dot(h_scr[...], win_ref[:, lo:lo + width])

    def head_rms(t, g_ref):
        ssq = _dot((t * t).astype(BF16), hsum_ref[...])
        return t * lax.rsqrt(ssq * (1.0 / HEAD_DIM) + EPS) * g_ref[...]

    lane = lax.broadcasted_iota(jnp.int32, (1, ATTN_WIDTH), 1)
    even_head = (lane % (2 * HEAD_DIM)) < HEAD_DIM
    qn = head_rms(proj(_C_Q, ATTN_WIDTH), qg_ref)
    q_even = jnp.where(even_head, qn, 0.0).astype(BF16)
    q_odd = jnp.where(even_head, 0.0, qn).astype(BF16)
    for qb in range(ts // Q_BLOCK):
        rs = slice(qb * Q_BLOCK, (qb + 1) * Q_BLOCK)
        q_scr[qb, 0:Q_BLOCK, :] = q_even[rs]
        q_scr[qb, Q_BLOCK:2 * Q_BLOCK, :] = q_odd[rs]
    k_win[PREV:PREV + ts, :] = head_rms(proj(_C_K, ATTN_WIDTH), kg_ref).astype(BF16)
    vt_win[:, PREV:PREV + ts] = _dot_nt(wvt_ref[...], h_scr[...]).astype(BF16)

    def sg_u():
        u_scr[...] = jax.nn.gelu(proj(_C_U, SG_WIDTH))

    def sg_v():
        vs_scr[...] = jax.nn.gelu(proj(_C_VS, SG_WIDTH))

    gdim = SG_WIDTH // SG_GROUPS

    def sg_group(g):
        cs = slice(g * gdim, (g + 1) * gdim)
        vg = vs_scr[:, cs]
        mu = jnp.mean(vg, axis=-1, keepdims=True)
        xc = vg - mu
        var = jnp.mean(xc * xc, axis=-1, keepdims=True)
        vn = (xc * lax.rsqrt(var + EPS) * lng_ref[:, cs] + lnb_ref[:, cs]).astype(BF16)
        r_i = lax.broadcasted_iota(jnp.int32, (SG_BLOCK, SG_BLOCK), 0)
        c_i = lax.broadcasted_iota(jnp.int32, (SG_BLOCK, SG_BLOCK), 1)
        wm = jnp.where(r_i >= c_i, sgw_ref[g], 0.0).astype(BF16)
        for j in range(ts // SG_BLOCK):
            rs = slice(j * SG_BLOCK, (j + 1) * SG_BLOCK)
            mixed = _dot(wm, vn[rs]) + sgb_ref[:, cs]
            ysg_scr[rs, cs] = (u_scr[rs, cs] * mixed).astype(BF16)

    def gate_a_chunk(c):
        cols = slice(c * GATE_CHUNK, (c + 1) * GATE_CHUNK)
        ga_scr[:, cols] = jax.nn.sigmoid(proj(_C_GA + c * GATE_CHUNK, GATE_CHUNK))

    def gated_b_chunk(c):
        cols = slice(c * GATE_CHUNK, (c + 1) * GATE_CHUNK)
        gate = jax.nn.sigmoid(proj(_C_GB + c * GATE_CHUNK, GATE_CHUNK))
        mb_scr[:, cols] = gate * _dot(ysg_scr[...], wb_ref[:, cols])

    n_chunks = D_MODEL // GATE_CHUNK
    side_work = [sg_u, sg_v] + [functools.partial(sg_group, g) for g in range(SG_GROUPS)]
    side_work += [functools.partial(gate_a_chunk, c) for c in range(n_chunks)]
    side_work += [functools.partial(gated_b_chunk, c) for c in range(n_chunks)]

    def finish(grp):
        rs = slice(grp * FINISH_ROWS, (grp + 1) * FINISH_ROWS)
        merged = ga_scr[rs, :] * _dot(ya_scr[rs, :], wa_ref[...]) + mb_scr[rs, :]
        x1 = x_ref[0, rs, :] + _dot(merged.astype(BF16), wo_ref[...])
        x1_ref[0, rs, :] = x1
        ms2 = jnp.mean(x1 * x1, axis=-1, keepdims=True)
        h2 = x1 * lax.rsqrt(ms2 + EPS) * g2_ref[...]
        h2_ref[0, rs, :] = _pack_bf16_pairs(h2)
        lt = _dot_nt(rwt_ref[...], h2, precision=lax.Precision.HIGHEST) + rb_ref[...]
        e_iota = lax.broadcasted_iota(jnp.int32, lt.shape, 0)
        vals = []
        for k in range(TOP_K):
            m = jnp.max(lt, axis=0, keepdims=True)
            idx = jnp.min(jnp.where(lt == m, e_iota, N_EXPERTS), axis=0, keepdims=True)
            vals.append(m)
            ti_ref[k:k + 1, rs] = idx
            lt = jnp.where(e_iota == idx, -jnp.inf, lt)
        exps = [jnp.exp(v - vals[0]) for v in vals]
        denom = exps[0] + exps[1] + exps[2] + exps[3]
        for k in range(TOP_K):
            pr_ref[k:k + 1, rs] = exps[k] / denom

    pairs = N_HEADS // 2
    n_qb = ts // Q_BLOCK
    n_blocks = n_qb * pairs
    schedule = [[] for _ in range(n_blocks)]
    side_blocks = n_blocks if FINISH_ROWS == ts else n_blocks // 2
    for i, piece in enumerate(side_work):
        schedule[-(-(i + 1) * side_blocks // len(side_work)) - 1].append(piece)
    qb_per_group = FINISH_ROWS // Q_BLOCK
    for grp in range(ts // FINISH_ROWS):
        ready = max(side_blocks, (grp + 1) * qb_per_group * pairs) - 1
        schedule[min(n_blocks - 1, ready + 1)].append(functools.partial(finish, grp))

    key_row = lax.broadcasted_iota(jnp.int32, (WIN, 1), 0)
    for qb in range(ts // Q_BLOCK):
        r0 = qb * Q_BLOCK
        valid = key_row >= (PREV - (s_idx * ts + r0))
        for pair in range(N_HEADS // 2):
            cs = slice(pair * 2 * HEAD_DIM, (pair + 1) * 2 * HEAD_DIM)
            st = _dot_nt(k_win[r0:r0 + WIN, cs], q_scr[qb, :, cs]) + bias_ref[pair]
            st = jnp.where(valid, st, NEG)
            m = jnp.max(st, axis=0, keepdims=True)
            p = jnp.exp(st - m)
            inv = 1.0 / jnp.sum(p, axis=0, keepdims=True)
            ot = _dot(vt_win[cs, r0:r0 + WIN], p.astype(BF16))
            o = jnp.concatenate(
                [ot[0:HEAD_DIM, 0:Q_BLOCK] * inv[:, 0:Q_BLOCK],
                 ot[HEAD_DIM:2 * HEAD_DIM, Q_BLOCK:2 * Q_BLOCK] * inv[:, Q_BLOCK:2 * Q_BLOCK]],
                axis=0)
            ya_scr[r0:r0 + Q_BLOCK, cs] = o.T.astype(BF16)
            for piece in schedule[qb * (N_HEADS // 2) + pair]:
                piece()


def _const_spec(shape):
    zeros = (0,) * len(shape)
    return pl.BlockSpec(shape, lambda b, s: zeros, pipeline_mode=pl.Buffered(1))


def _mixer(x, g1, win, wvt, qg, kg, hsum, bias, lng, lnb, sgw, sgb, wa, wb, wo, g2, rwt, rb):
    bsz, seq, d = x.shape
    ts = SEQ_TILE
    ns = seq // ts
    t = bsz * seq
    consts = (g1, win, wvt, qg, kg, hsum, bias, lng, lnb, sgw, sgb, wa, wb, wo, g2, rwt, rb)
    tok_spec = pl.BlockSpec((1, ts, d), lambda b, s: (b, s, 0))
    packed_spec = pl.BlockSpec((1, ts, d // 2), lambda b, s: (b, s, 0))
    idx_spec = pl.BlockSpec((TOP_K, ts), lambda b, s: (0, b * ns + s))
    return pl.pallas_call(
        _mixer_kernel,
        grid=(bsz, ns),
        in_specs=[tok_spec] + [_const_spec(c.shape) for c in consts],
        out_specs=[tok_spec, packed_spec, idx_spec, idx_spec],
        out_shape=[
            jax.ShapeDtypeStruct((bsz, seq, d), F32),
            jax.ShapeDtypeStruct((bsz, seq, d // 2), jnp.int32),
            jax.ShapeDtypeStruct((TOP_K, t), jnp.int32),
            jax.ShapeDtypeStruct((TOP_K, t), F32),
        ],
        scratch_shapes=[
            pltpu.VMEM((ts, d), BF16),
            pltpu.VMEM((ts // Q_BLOCK, 2 * Q_BLOCK, ATTN_WIDTH), BF16),
            pltpu.VMEM((PREV + ts, ATTN_WIDTH), BF16),
            pltpu.VMEM((ATTN_WIDTH, PREV + ts), BF16),
            pltpu.VMEM((ts, ATTN_WIDTH), BF16),
            pltpu.VMEM((ts, SG_WIDTH), F32),
            pltpu.VMEM((ts, SG_WIDTH), F32),
            pltpu.VMEM((ts, SG_WIDTH), BF16),
            pltpu.VMEM((ts, d), F32),
            pltpu.VMEM((ts, d), F32),
        ],
        compiler_params=pltpu.CompilerParams(
            dimension_semantics=("arbitrary", "arbitrary"),
            vmem_limit_bytes=VMEM_BYTES_V7X * 7 // 8),
        name="mixer",
    )(x, *consts)


def _attention_bias(rel_bias):
    i = np.arange(Q_BLOCK)[:, None]
    j = np.arange(WIN)[None, :]
    kk = j - (i // CHUNK) * CHUNK
    inside = (kk >= 0) & (kk < BAND)
    tab = rel_bias.astype(F32)
    far = PREV + Q_BLOCK - 1 - REL_CLIP + 1
    near = Q_BLOCK - 1 + REL_CLIP
    ext = jnp.concatenate(
        [jnp.broadcast_to(tab[:, 2 * REL_CLIP:], (N_HEADS, far)),
         tab[:, 2 * REL_CLIP - near:2 * REL_CLIP][:, ::-1]], axis=1)
    rows = [ext[:, Q_BLOCK - 1 - r:Q_BLOCK - 1 - r + WIN] for r in range(Q_BLOCK)]
    bias = jnp.where(inside[None], jnp.stack(rows, axis=1), NEG)
    return (bias.reshape(N_HEADS // 2, 2 * Q_BLOCK, WIN).transpose(0, 2, 1))


def _rank_kernel(ti_ref, upper_ref, dest_ref, cnt_ref, total_scr, base_scr):
    phase = pl.program_id(0)
    i = pl.program_id(1)
    tt = ti_ref.shape[1]

    @pl.when((phase == 0) & (i == 0))
    def _():
        total_scr[...] = jnp.zeros_like(total_scr)

    ti = ti_ref[...]
    e_iota = lax.broadcasted_iota(jnp.int32, (N_EXPERTS, tt), 0)
    hits = [e_iota == ti[k:k + 1, :] for k in range(TOP_K)]
    cnt = hits[0].astype(F32)
    for k in range(1, TOP_K):
        cnt = cnt + hits[k].astype(F32)
    tile_total = jnp.sum(cnt, axis=1, keepdims=True)

    @pl.when(phase == 0)
    def _():
        total_scr[...] = total_scr[...] + tile_total

    @pl.when((phase == 1) & (i == 0))
    def _():
        cnt_ref[...] = total_scr[...].astype(jnp.int32)
        padded = jnp.floor((total_scr[...] + (ROW_TILE - 1)) * (1.0 / ROW_TILE)) * ROW_TILE
        run = jnp.zeros((1, LANES), F32)
        for e in range(N_EXPERTS):
            base_scr[e:e + 1, :] = run
            run = run + padded[e:e + 1, :]

    @pl.when(phase == 1)
    def _():
        before = _dot(cnt.astype(BF16), upper_ref[...]) + base_scr[:, 0:1]
        for k in range(TOP_K):
            dest_ref[k:k + 1, :] = jnp.sum(jnp.where(hits[k], before, 0.0), axis=0,
                                           keepdims=True).astype(jnp.int32)
        base_scr[...] = base_scr[...] + tile_total


def _rank(top_i):
    t = top_i.shape[1]
    tt = RANK_TILE
    upper = jnp.asarray(np.triu(np.ones((tt, tt), np.float32), 1), BF16)
    return pl.pallas_call(
        _rank_kernel,
        grid=(2, t // tt),
        in_specs=[pl.BlockSpec((TOP_K, tt), lambda p, i: (0, i)),
                  pl.BlockSpec((tt, tt), lambda p, i: (0, 0), pipeline_mode=pl.Buffered(1))],
        out_specs=[pl.BlockSpec((TOP_K, tt), lambda p, i: (0, i * p)),
                   pl.BlockSpec((N_EXPERTS, LANES), lambda p, i: (0, 0))],
        out_shape=[jax.ShapeDtypeStruct((TOP_K, t), jnp.int32),
                   jax.ShapeDtypeStruct((N_EXPERTS, LANES), jnp.int32)],
        scratch_shapes=[pltpu.VMEM((N_EXPERTS, LANES), F32),
                        pltpu.VMEM((N_EXPERTS, LANES), F32)],
        compiler_params=pltpu.CompilerParams(dimension_semantics=("arbitrary", "arbitrary")),
        name="rank",
    )(top_i, upper)


def _plan_kernel(cnt_ref, tile_ref, exp_ref, rows_ref, first_ref):
    n_tiles = exp_ref.shape[0]

    def per_expert(e, w):
        c = cnt_ref[e]
        n = (c + ROW_TILE - 1) // ROW_TILE

        def per_tile(j, _):
            tile_ref[w + j] = w + j
            exp_ref[w + j] = e
            rows_ref[w + j] = jnp.minimum(c - j * ROW_TILE, ROW_TILE)
            first_ref[w + j] = (j == 0).astype(jnp.int32)
            return 0

        lax.fori_loop(0, n, per_tile, 0)
        return w + n

    n_used = lax.fori_loop(0, N_EXPERTS, per_expert, jnp.int32(0))

    def pad(w, _):
        tile_ref[w] = n_used - 1
        exp_ref[w] = exp_ref[n_used - 1]
        rows_ref[w] = 0
        first_ref[w] = 0
        return 0

    lax.fori_loop(n_used, n_tiles, pad, 0)


def _plan(counts, n_tiles):
    smem = pl.BlockSpec(memory_space=pltpu.SMEM)
    vec = jax.ShapeDtypeStruct((n_tiles,), jnp.int32)
    return pl.pallas_call(
        _plan_kernel,
        in_specs=[smem],
        out_specs=[smem, smem, smem, smem],
        out_shape=[vec, vec, vec, vec],
        name="plan",
    )(counts)


def _scatter(h2, dest, n_slots):
    t, d = h2.shape
    per_worker = t // SC_WORKERS
    n = SC_SCATTER_CHUNK
    nch = per_worker // n
    assert nch % 2 == 0
    mesh = plsc.VectorSubcoreMesh(core_axis_name="c", subcore_axis_name="s")

    @functools.partial(
        pl.kernel, mesh=mesh,
        out_type=jax.ShapeDtypeStruct((n_slots, d), h2.dtype),
        scratch_types=[pltpu.VMEM((TOP_K, nch, n), jnp.int32), pltpu.VMEM((2, n, d), h2.dtype),
                       pltpu.SemaphoreType.DMA((2,)), pltpu.SemaphoreType.DMA((2,))],
        name="sc_scatter")
    def scatter_kernel(h_hbm, dest_hbm, xs_hbm, idx_v, rows_v, sem_in, sem_out):
        wid = lax.axis_index("s") * SC_CORES + lax.axis_index("c")
        base = wid * per_worker
        for k in range(TOP_K):
            pltpu.sync_copy(dest_hbm.at[k, pl.ds(wid * nch, nch)], idx_v.at[k])

        def load(c, s):
            return pltpu.make_async_copy(h_hbm.at[pl.ds(base + c * n, n)], rows_v.at[s],
                                         sem_in.at[s])

        def store(c, s, k):
            return pltpu.make_async_copy(rows_v.at[s], xs_hbm.at[idx_v.at[k, c]], sem_out.at[s])

        load(0, 0).start()

        @pl.loop(0, nch, step=2)
        def _(c):
            for s in range(2):
                cc = c + s

                @pl.when(cc >= 1)
                def _():
                    for k in range(TOP_K):
                        store(cc - 1, 1 - s, k).wait()

                @pl.when(cc + 1 < nch)
                def _():
                    load(cc + 1, 1 - s).start()

                load(cc, s).wait()
                for k in range(TOP_K):
                    store(cc, s, k).start()

        for k in range(TOP_K):
            store(nch - 1, (nch - 1) % 2, k).wait()

    return scatter_kernel(h2, dest.reshape(TOP_K, t // n, n))


def _expert_kernel(tile_ref, exp_ref, rows_ref, first_ref,
                   xs_ref, wgu_ref, bgu_ref, wd_ref, bd_ref, y_ref, wgu_bf, wd_bf):
    w = pl.program_id(0)

    @pl.when(first_ref[w] == 1)
    def _():
        wgu_bf[...] = wgu_ref[0].astype(BF16)
        wd_bf[...] = wd_ref[0].astype(BF16)

    n_rows = rows_ref[w]

    @pl.when(n_rows > 0)
    def _():
        row = lax.broadcasted_iota(jnp.int32, xs_ref.shape, 0)
        x = _unpack_bf16_pairs(jnp.where(row < n_rows, xs_ref[...], 0))
        gu = _dot(x, wgu_bf[...]) + bgu_ref[0]
        gate = jnp.minimum(gu[:, :D_FF], SWIGLU_LIMIT)
        up = jnp.clip(gu[:, D_FF:], -SWIGLU_LIMIT, SWIGLU_LIMIT)
        act = (up + 1.0) * (gate * jax.nn.sigmoid(gate * SWIGLU_ALPHA))
        y_ref[...] = _pack_bf16_pairs(_dot(act.astype(BF16), wd_bf[...]) + bd_ref[0])


def _experts(xs, plan, wgu, bgu, wd, bd):
    n_rows, dp = xs.shape
    d = 2 * dp
    w_tile, w_exp, w_rows, w_first = plan
    n_items = w_tile.shape[0]
    tm = ROW_TILE
    by_tile = lambda w, tl, ex, nr, fi: (tl[w], 0)
    by_expert = lambda w, tl, ex, nr, fi: (ex[w], 0, 0)
    grid_spec = pltpu.PrefetchScalarGridSpec(
        num_scalar_prefetch=4,
        grid=(n_items,),
        in_specs=[
            pl.BlockSpec((tm, dp), by_tile),
            pl.BlockSpec((1, d, 2 * D_FF), by_expert),
            pl.BlockSpec((1, 1, 2 * D_FF), by_expert),
            pl.BlockSpec((1, D_FF, d), by_expert),
            pl.BlockSpec((1, 1, d), by_expert),
        ],
        out_specs=pl.BlockSpec((tm, dp), by_tile),
        scratch_shapes=[pltpu.VMEM((d, 2 * D_FF), BF16), pltpu.VMEM((D_FF, d), BF16)],
    )
    return pl.pallas_call(
        _expert_kernel,
        grid_spec=grid_spec,
        out_shape=jax.ShapeDtypeStruct((n_rows, dp), jnp.int32),
        compiler_params=pltpu.CompilerParams(
            dimension_semantics=("arbitrary",),
            vmem_limit_bytes=VMEM_BYTES_V7X * 7 // 8),
        name="experts",
    )(w_tile, w_exp, w_rows, w_first, xs, wgu, bgu, wd, bd)


def _combine(x1, probs, dest, y):
    t, d = x1.shape
    per_worker = t // SC_WORKERS
    dp = d // 2
    n = SC_COMBINE_CHUNK
    nch = per_worker // n
    assert nch % 2 == 0
    mesh = plsc.VectorSubcoreMesh(core_axis_name="c", subcore_axis_name="s")
    high_half = -65536

    @functools.partial(
        pl.kernel, mesh=mesh,
        out_type=jax.ShapeDtypeStruct((t, d), F32),
        scratch_types=[pltpu.VMEM((TOP_K, per_worker), jnp.int32),
                       pltpu.VMEM((2, TOP_K, n, dp), jnp.int32),
                       pltpu.VMEM((2, n, d), F32), pltpu.VMEM((2, n, d), F32),
                       pltpu.VMEM((TOP_K, per_worker), F32),
                       pltpu.SemaphoreType.DMA((2,)), pltpu.SemaphoreType.DMA((2,))],
        compiler_params=pltpu.CompilerParams(needs_layout_passes=False),
        name="sc_combine")
    def combine_kernel(x1_hbm, p_hbm, dest_hbm, y_hbm, out_hbm,
                       idx_v, rows_v, x_v, o_v, p_v, sem_in, sem_out):
        wid = lax.axis_index("s") * SC_CORES + lax.axis_index("c")
        base = wid * per_worker
        for k in range(TOP_K):
            pltpu.sync_copy(dest_hbm.at[k, pl.ds(base, per_worker)], idx_v.at[k])
            pltpu.sync_copy(p_hbm.at[k, pl.ds(base, per_worker)], p_v.at[k])

        def loads(c, s):
            off = base + c * n
            cps = [pltpu.make_async_copy(y_hbm.at[idx_v.at[k, pl.ds(c * n, n)]], rows_v.at[s, k],
                                         sem_in.at[s])
                   for k in range(TOP_K)]
            cps.append(pltpu.make_async_copy(x1_hbm.at[pl.ds(off, n)], x_v.at[s], sem_in.at[s]))
            return cps

        def store(c, s):
            return pltpu.make_async_copy(o_v.at[s], out_hbm.at[pl.ds(base + c * n, n)],
                                         sem_out.at[s])

        for cp in loads(0, 0):
            cp.start()

        @pl.loop(0, nch, step=2)
        def _(c):
            for s in range(2):
                cc = c + s

                @pl.when(cc + 1 < nch)
                def _():
                    for cp in loads(cc + 1, 1 - s):
                        cp.start()

                for cp in loads(cc, s):
                    cp.wait()

                @pl.when(cc >= 2)
                def _():
                    store(cc - 2, s).wait()

                @pl.loop(0, n)
                def _(r):
                    tok = jnp.full((SC_LANES,), cc * n + r, jnp.int32)
                    w = [plsc.load_gather(p_v, [jnp.full((SC_LANES,), k, jnp.int32), tok])
                         for k in range(TOP_K)]
                    for g in range(dp // SC_LANES):
                        lo_sl = pl.ds(g * SC_LANES, SC_LANES)
                        hi_sl = pl.ds(dp + g * SC_LANES, SC_LANES)
                        lo = x_v[s, r, lo_sl]
                        hi = x_v[s, r, hi_sl]
                        for k in range(TOP_K):
                            word = rows_v[s, k, r, lo_sl]
                            lo = lo + w[k] * plsc.bitcast(word << 16, F32)
                            hi = hi + w[k] * plsc.bitcast(word & high_half, F32)
                        o_v[s, r, lo_sl] = lo
                        o_v[s, r, hi_sl] = hi

                store(cc, s).start()

        store(nch - 2, 0).wait()
        store(nch - 1, 1).wait()

    return combine_kernel(x1, probs, dest, y)


def _layer(x, norm1_g, w_in, q_norm_g, k_norm_g, rel_bias, sg_ln_g, sg_ln_b, sg_w, sg_b,
           w_branch_a, w_branch_b, w_out, norm2_g, router_w, router_b,
           w_gate_up, b_gate_up, w_down, b_down):
    bsz, seq, d = x.shape
    t = bsz * seq
    row = lambda v: v.reshape(1, -1).astype(F32)
    scale = 1.0 / math.sqrt(HEAD_DIM)
    head_sum = jnp.asarray(np.kron(np.eye(N_HEADS), np.ones((HEAD_DIM, HEAD_DIM))), BF16)
    sgb = jnp.repeat(sg_b.T.astype(F32), SG_WIDTH // SG_GROUPS, axis=1)

    x1, h2, top_i, probs = _mixer(
        x, row(norm1_g), w_in.astype(BF16), w_in[:, _C_V:_C_V + ATTN_WIDTH].T.astype(BF16),
        row(jnp.tile(q_norm_g, N_HEADS)) * scale, row(jnp.tile(k_norm_g, N_HEADS)),
        head_sum, _attention_bias(rel_bias), row(sg_ln_g), row(sg_ln_b),
        sg_w.astype(F32), sgb, w_branch_a.astype(BF16), w_branch_b.astype(BF16),
        w_out.astype(BF16), row(norm2_g), router_w.T.astype(F32),
        router_b.reshape(N_EXPERTS, 1).astype(F32))
    x1 = x1.reshape(t, d)
    h2 = h2.reshape(t, d // 2)

    dest, counts = _rank(top_i)
    n_tiles = t * TOP_K // ROW_TILE + N_EXPERTS
    plan = _plan(counts[:, 0], n_tiles)
    xs = _scatter(h2, dest, n_tiles * ROW_TILE)
    y = _experts(xs, plan, w_gate_up.astype(F32), b_gate_up.reshape(N_EXPERTS, 1, -1).astype(F32),
                 w_down.astype(F32), b_down.reshape(N_EXPERTS, 1, -1).astype(F32))
    out = _combine(x1, probs, dest, y)
    return out.reshape(bsz, seq, d)


def kernel(x, norm1_g, w_in, q_norm_g, k_norm_g, rel_bias, sg_ln_g, sg_ln_b, sg_w, sg_b,
           w_branch_a, w_branch_b, w_out, norm2_g, router_w, router_b,
           w_gate_up, b_gate_up, w_down, b_down):
    depth = norm1_g.shape[0]
    for l in range(depth):
        x = _layer(x, norm1_g[l], w_in[l], q_norm_g[l], k_norm_g[l], rel_bias[l], sg_ln_g[l],
                   sg_ln_b[l], sg_w[l], sg_b[l], w_branch_a[l], w_branch_b[l], w_out[l],
                   norm2_g[l], router_w[l], router_b[l], w_gate_up[l], b_gate_up[l],
                   w_down[l], b_down[l])
    return x
```

```python
import functools
import math

import numpy as np
import jax
import jax.numpy as jnp
from jax import lax
from jax.experimental import pallas as pl
from jax.experimental.pallas import tpu as pltpu
from jax.experimental.pallas import tpu_sc as plsc

F32 = jnp.float32
BF16 = jnp.bfloat16

D_MODEL = 1024
CHUNK = 64
N_BACK = 8
BAND = (N_BACK + 1) * CHUNK
N_HEADS = 8
HEAD_DIM = 64
ATTN_WIDTH = N_HEADS * HEAD_DIM
REL_CLIP = 256
SG_BLOCK = 128
SG_GROUPS = 4
SG_WIDTH = 512
IN_WIDTH = 3 * ATTN_WIDTH + 2 * SG_WIDTH + 2 * D_MODEL
N_EXPERTS = 32
TOP_K = 4
D_FF = D_MODEL
SWIGLU_LIMIT = 7.0
SWIGLU_ALPHA = 1.702
EPS = 1e-6
NEG = -1e30

LANES = 128
VMEM_BYTES_V7X = 64 * 1024 * 1024

SEQ_TILE = 512
Q_BLOCK = 2 * CHUNK
PREV = N_BACK * CHUNK
WIN = PREV + Q_BLOCK
GATE_CHUNK = 256
FINISH_ROWS = SEQ_TILE
RANK_TILE = 1024
ROW_TILE = 512
MOE_GROUPS = 2

SC_CORES = 2
SC_SUBCORES = 16
SC_LANES = 16
SC_WORKERS = SC_CORES * SC_SUBCORES
SC_SCATTER_CHUNK = 64
SC_COMBINE_CHUNK = 8

_C_Q = 0
_C_K = ATTN_WIDTH
_C_V = 2 * ATTN_WIDTH
_C_U = 3 * ATTN_WIDTH
_C_VS = _C_U + SG_WIDTH
_C_GA = _C_VS + SG_WIDTH
_C_GB = _C_GA + D_MODEL


def _dot(a, b):
    return jnp.dot(a, b, preferred_element_type=F32)


def _pack_bf16_pairs(v):
    w = v.shape[1] // 2
    lo = lax.bitcast_convert_type(v[:, :w].astype(BF16).astype(F32), jnp.uint32)
    hi = lax.bitcast_convert_type(v[:, w:].astype(BF16).astype(F32), jnp.uint32)
    return lax.bitcast_convert_type((lo >> 16) | hi, jnp.int32)


def _unpack_bf16_pairs(p):
    u = lax.bitcast_convert_type(p, jnp.uint32)
    lo = lax.bitcast_convert_type(u << 16, F32)
    hi = lax.bitcast_convert_type(u & jnp.uint32(0xFFFF0000), F32)
    return jnp.concatenate([lo, hi], axis=1).astype(BF16)


def _dot_nt(a, b, precision=None):
    return lax.dot_general(a, b, (((1,), (1,)), ((), ())),
                           preferred_element_type=F32, precision=precision)


def _mixer_kernel(x_ref, g1_ref, win_ref, wvt_ref, qg_ref, kg_ref, hsum_ref, bias_ref, lng_ref,
                  lnb_ref, sgw_ref, sgb_ref, wa_ref, wb_ref, wo_ref, g2_ref, rwt_ref, rb_ref,
                  x1_ref, h2_ref, ti_ref, pr_ref,
                  h_scr, q_scr, k_win, vt_win, ya_scr, u_scr, vs_scr, ysg_scr, ga_scr, mb_scr):
    ts = x_ref.shape[1]
    s_idx = pl.program_id(1)

    @pl.when(s_idx == 0)
    def _():
        k_win[0:PREV, :] = jnp.zeros((PREV, ATTN_WIDTH), BF16)
        vt_win[:, 0:PREV] = jnp.zeros((ATTN_WIDTH, PREV), BF16)

    @pl.when(s_idx > 0)
    def _():
        k_win[0:PREV, :] = k_win[ts:ts + PREV, :]
        vt_win[:, 0:PREV] = vt_win[:, ts:ts + PREV]

    x = x_ref[0]
    ms = jnp.mean(x * x, axis=-1, keepdims=True)
    h_scr[...] = (x * lax.rsqrt(ms + EPS) * g1_ref[...]).astype(BF16)

    def proj(lo, width):
        return _dot(h_scr[...], win_ref[:, lo:lo + width])

    def head_rms(t, g_ref):
        ssq = _dot((t * t).astype(BF16), hsum_ref[...])
        return t * lax.rsqrt(ssq * (1.0 / HEAD_DIM) + EPS) * g_ref[...]

    lane = lax.broadcasted_iota(jnp.int32, (1, ATTN_WIDTH), 1)
    even_head = (lane % (2 * HEAD_DIM)) < HEAD_DIM
    qn = head_rms(proj(_C_Q, ATTN_WIDTH), qg_ref)
    q_even = jnp.where(even_head, qn, 0.0).astype(BF16)
    q_odd = jnp.where(even_head, 0.0, qn).astype(BF16)
    for qb in range(ts // Q_BLOCK):
        rs = slice(qb * Q_BLOCK, (qb + 1) * Q_BLOCK)
        q_scr[qb, 0:Q_BLOCK, :] = q_even[rs]
        q_scr[qb, Q_BLOCK:2 * Q_BLOCK, :] = q_odd[rs]
    k_win[PREV:PREV + ts, :] = head_rms(proj(_C_K, ATTN_WIDTH), kg_ref).astype(BF16)
    vt_win[:, PREV:PREV + ts] = _dot_nt(wvt_ref[...], h_scr[...]).astype(BF16)

    def sg_u():
        u_scr[...] = jax.nn.gelu(proj(_C_U, SG_WIDTH))

    def sg_v():
        vs_scr[...] = jax.nn.gelu(proj(_C_VS, SG_WIDTH))

    gdim = SG_WIDTH // SG_GROUPS

    def sg_group(g):
        cs = slice(g * gdim, (g + 1) * gdim)
        vg = vs_scr[:, cs]
        mu = jnp.mean(vg, axis=-1, keepdims=True)
        xc = vg - mu
        var = jnp.mean(xc * xc, axis=-1, keepdims=True)
        vn = (xc * lax.rsqrt(var + EPS) * lng_ref[:, cs] + lnb_ref[:, cs]).astype(BF16)
        r_i = lax.broadcasted_iota(jnp.int32, (SG_BLOCK, SG_BLOCK), 0)
        c_i = lax.broadcasted_iota(jnp.int32, (SG_BLOCK, SG_BLOCK), 1)
        wm = jnp.where(r_i >= c_i, sgw_ref[g], 0.0).astype(BF16)
        for j in range(ts // SG_BLOCK):
            rs = slice(j * SG_BLOCK, (j + 1) * SG_BLOCK)
            mixed = _dot(wm, vn[rs]) + sgb_ref[:, cs]
            ysg_scr[rs, cs] = (u_scr[rs, cs] * mixed).astype(BF16)

    def gate_a_chunk(c):
        cols = slice(c * GATE_CHUNK, (c + 1) * GATE_CHUNK)
        ga_scr[:, cols] = jax.nn.sigmoid(proj(_C_GA + c * GATE_CHUNK, GATE_CHUNK))

    def gated_b_chunk(c):
        cols = slice(c * GATE_CHUNK, (c + 1) * GATE_CHUNK)
        gate = jax.nn.sigmoid(proj(_C_GB + c * GATE_CHUNK, GATE_CHUNK))
        mb_scr[:, cols] = gate * _dot(ysg_scr[...], wb_ref[:, cols])

    n_chunks = D_MODEL // GATE_CHUNK
    side_work = [sg_u, sg_v] + [functools.partial(sg_group, g) for g in range(SG_GROUPS)]
    side_work += [functools.partial(gate_a_chunk, c) for c in range(n_chunks)]
    side_work += [functools.partial(gated_b_chunk, c) for c in range(n_chunks)]

    def finish(grp):
        rs = slice(grp * FINISH_ROWS, (grp + 1) * FINISH_ROWS)
        merged = ga_scr[rs, :] * _dot(ya_scr[rs, :], wa_ref[...]) + mb_scr[rs, :]
        x1 = x_ref[0, rs, :] + _dot(merged.astype(BF16), wo_ref[...])
        x1_ref[0, rs, :] = x1
        ms2 = jnp.mean(x1 * x1, axis=-1, keepdims=True)
        h2 = x1 * lax.rsqrt(ms2 + EPS) * g2_ref[...]
        h2_ref[0, rs, :] = _pack_bf16_pairs(h2)
        lt = _dot_nt(rwt_ref[...], h2, precision=lax.Precision.HIGHEST) + rb_ref[...]
        e_iota = lax.broadcasted_iota(jnp.int32, lt.shape, 0)
        vals = []
        for k in range(TOP_K):
            m = jnp.max(lt, axis=0, keepdims=True)
            idx = jnp.min(jnp.where(lt == m, e_iota, N_EXPERTS), axis=0, keepdims=True)
            vals.append(m)
            ti_ref[k:k + 1, rs] = idx
            lt = jnp.where(e_iota == idx, -jnp.inf, lt)
        exps = [jnp.exp(v - vals[0]) for v in vals]
        denom = exps[0] + exps[1] + exps[2] + exps[3]
        for k in range(TOP_K):
            pr_ref[k:k + 1, rs] = exps[k] / denom

    pairs = N_HEADS // 2
    n_qb = ts // Q_BLOCK
    n_blocks = n_qb * pairs
    schedule = [[] for _ in range(n_blocks)]
    side_blocks = n_blocks if FINISH_ROWS == ts else n_blocks // 2
    for i, piece in enumerate(side_work):
        schedule[-(-(i + 1) * side_blocks // len(side_work)) - 1].append(piece)
    qb_per_group = FINISH_ROWS // Q_BLOCK
    for grp in range(ts // FINISH_ROWS):
        ready = max(side_blocks, (grp + 1) * qb_per_group * pairs) - 1
        schedule[min(n_blocks - 1, ready + 1)].append(functools.partial(finish, grp))

    key_row = lax.broadcasted_iota(jnp.int32, (WIN, 1), 0)
    for qb in range(ts // Q_BLOCK):
        r0 = qb * Q_BLOCK
        valid = key_row >= (PREV - (s_idx * ts + r0))
        for pair in range(N_HEADS // 2):
            cs = slice(pair * 2 * HEAD_DIM, (pair + 1) * 2 * HEAD_DIM)
            st = _dot_nt(k_win[r0:r0 + WIN, cs], q_scr[qb, :, cs]) + bias_ref[pair]
            st = jnp.where(valid, st, NEG)
            m = jnp.max(st, axis=0, keepdims=True)
            p = jnp.exp(st - m)
            inv = 1.0 / jnp.sum(p, axis=0, keepdims=True)
            ot = _dot(vt_win[cs, r0:r0 + WIN], p.astype(BF16))
            o = jnp.concatenate(
                [ot[0:HEAD_DIM, 0:Q_BLOCK] * inv[:, 0:Q_BLOCK],
                 ot[HEAD_DIM:2 * HEAD_DIM, Q_BLOCK:2 * Q_BLOCK] * inv[:, Q_BLOCK:2 * Q_BLOCK]],
                axis=0)
            ya_scr[r0:r0 + Q_BLOCK, cs] = o.T.astype(BF16)
            for piece in schedule[qb * (N_HEADS // 2) + pair]:
                piece()


def _const_spec(shape):
    zeros = (0,) * len(shape)
    return pl.BlockSpec(shape, lambda b, s: zeros, pipeline_mode=pl.Buffered(1))


def _mixer(x, batch0, bsz, consts):
    _, seq, d = x.shape
    ts = SEQ_TILE
    ns = seq // ts
    t = bsz * seq
    in_spec = pl.BlockSpec((1, ts, d), lambda b, s: (b + batch0, s, 0))
    tok_spec = pl.BlockSpec((1, ts, d), lambda b, s: (b, s, 0))
    packed_spec = pl.BlockSpec((1, ts, d // 2), lambda b, s: (b, s, 0))
    idx_spec = pl.BlockSpec((TOP_K, ts), lambda b, s: (0, b * ns + s))
    return pl.pallas_call(
        _mixer_kernel,
        grid=(bsz, ns),
        in_specs=[in_spec] + [_const_spec(c.shape) for c in consts],
        out_specs=[tok_spec, packed_spec, idx_spec, idx_spec],
        out_shape=[
            jax.ShapeDtypeStruct((bsz, seq, d), F32),
            jax.ShapeDtypeStruct((bsz, seq, d // 2), jnp.int32),
            jax.ShapeDtypeStruct((TOP_K, t), jnp.int32),
            jax.ShapeDtypeStruct((TOP_K, t), F32),
        ],
        scratch_shapes=[
            pltpu.VMEM((ts, d), BF16),
            pltpu.VMEM((ts // Q_BLOCK, 2 * Q_BLOCK, ATTN_WIDTH), BF16),
            pltpu.VMEM((PREV + ts, ATTN_WIDTH), BF16),
            pltpu.VMEM((ATTN_WIDTH, PREV + ts), BF16),
            pltpu.VMEM((ts, ATTN_WIDTH), BF16),
            pltpu.VMEM((ts, SG_WIDTH), F32),
            pltpu.VMEM((ts, SG_WIDTH), F32),
            pltpu.VMEM((ts, SG_WIDTH), BF16),
            pltpu.VMEM((ts, d), F32),
            pltpu.VMEM((ts, d), F32),
        ],
        compiler_params=pltpu.CompilerParams(
            dimension_semantics=("arbitrary", "arbitrary"),
            vmem_limit_bytes=VMEM_BYTES_V7X * 7 // 8),
        name="mixer",
    )(x, *consts)


def _attention_bias(rel_bias):
    i = np.arange(Q_BLOCK)[:, None]
    j = np.arange(WIN)[None, :]
    kk = j - (i // CHUNK) * CHUNK
    inside = (kk >= 0) & (kk < BAND)
    tab = rel_bias.astype(F32)
    far = PREV + Q_BLOCK - 1 - REL_CLIP + 1
    near = Q_BLOCK - 1 + REL_CLIP
    ext = jnp.concatenate(
        [jnp.broadcast_to(tab[:, 2 * REL_CLIP:], (N_HEADS, far)),
         tab[:, 2 * REL_CLIP - near:2 * REL_CLIP][:, ::-1]], axis=1)
    rows = [ext[:, Q_BLOCK - 1 - r:Q_BLOCK - 1 - r + WIN] for r in range(Q_BLOCK)]
    bias = jnp.where(inside[None], jnp.stack(rows, axis=1), NEG)
    return (bias.reshape(N_HEADS // 2, 2 * Q_BLOCK, WIN).transpose(0, 2, 1))


def _rank_kernel(ti_ref, upper_ref, dest_ref, cnt_ref, total_scr, base_scr):
    phase = pl.program_id(0)
    i = pl.program_id(1)
    tt = ti_ref.shape[1]

    @pl.when((phase == 0) & (i == 0))
    def _():
        total_scr[...] = jnp.zeros_like(total_scr)

    ti = ti_ref[...]
    e_iota = lax.broadcasted_iota(jnp.int32, (N_EXPERTS, tt), 0)
    hits = [e_iota == ti[k:k + 1, :] for k in range(TOP_K)]
    cnt = hits[0].astype(F32)
    for k in range(1, TOP_K):
        cnt = cnt + hits[k].astype(F32)
    tile_total = jnp.sum(cnt, axis=1, keepdims=True)

    @pl.when(phase == 0)
    def _():
        total_scr[...] = total_scr[...] + tile_total

    @pl.when((phase == 1) & (i == 0))
    def _():
        cnt_ref[...] = total_scr[...].astype(jnp.int32)
        padded = jnp.floor((total_scr[...] + (ROW_TILE - 1)) * (1.0 / ROW_TILE)) * ROW_TILE
        run = jnp.zeros((1, LANES), F32)
        for e in range(N_EXPERTS):
            base_scr[e:e + 1, :] = run
            run = run + padded[e:e + 1, :]

    @pl.when(phase == 1)
    def _():
        before = _dot(cnt.astype(BF16), upper_ref[...]) + base_scr[:, 0:1]
        for k in range(TOP_K):
            dest_ref[k:k + 1, :] = jnp.sum(jnp.where(hits[k], before, 0.0), axis=0,
                                           keepdims=True).astype(jnp.int32)
        base_scr[...] = base_scr[...] + tile_total


def _rank(top_i):
    t = top_i.shape[1]
    tt = RANK_TILE
    upper = jnp.asarray(np.triu(np.ones((tt, tt), np.float32), 1), BF16)
    return pl.pallas_call(
        _rank_kernel,
        grid=(2, t // tt),
        in_specs=[pl.BlockSpec((TOP_K, tt), lambda p, i: (0, i)),
                  pl.BlockSpec((tt, tt), lambda p, i: (0, 0), pipeline_mode=pl.Buffered(1))],
        out_specs=[pl.BlockSpec((TOP_K, tt), lambda p, i: (0, i * p)),
                   pl.BlockSpec((N_EXPERTS, LANES), lambda p, i: (0, 0))],
        out_shape=[jax.ShapeDtypeStruct((TOP_K, t), jnp.int32),
                   jax.ShapeDtypeStruct((N_EXPERTS, LANES), jnp.int32)],
        scratch_shapes=[pltpu.VMEM((N_EXPERTS, LANES), F32),
                        pltpu.VMEM((N_EXPERTS, LANES), F32)],
        compiler_params=pltpu.CompilerParams(dimension_semantics=("arbitrary", "arbitrary")),
        name="rank",
    )(top_i, upper)


def _plan_kernel(cnt_ref, tile_ref, exp_ref, rows_ref, first_ref):
    n_tiles = exp_ref.shape[0]

    def per_expert(e, w):
        c = cnt_ref[e]
        n = (c + ROW_TILE - 1) // ROW_TILE

        def per_tile(j, _):
            tile_ref[w + j] = w + j
            exp_ref[w + j] = e
            rows_ref[w + j] = jnp.minimum(c - j * ROW_TILE, ROW_TILE)
            first_ref[w + j] = (j == 0).astype(jnp.int32)
            return 0

        lax.fori_loop(0, n, per_tile, 0)
        return w + n

    n_used = lax.fori_loop(0, N_EXPERTS, per_expert, jnp.int32(0))

    def pad(w, _):
        tile_ref[w] = n_used - 1
        exp_ref[w] = exp_ref[n_used - 1]
        rows_ref[w] = 0
        first_ref[w] = 0
        return 0

    lax.fori_loop(n_used, n_tiles, pad, 0)


def _plan(counts, n_tiles):
    smem = pl.BlockSpec(memory_space=pltpu.SMEM)
    vec = jax.ShapeDtypeStruct((n_tiles,), jnp.int32)
    return pl.pallas_call(
        _plan_kernel,
        in_specs=[smem],
        out_specs=[smem, smem, smem, smem],
        out_shape=[vec, vec, vec, vec],
        name="plan",
    )(counts)


def _scatter(h2, dest, n_slots):
    t, d = h2.shape
    per_worker = t // SC_WORKERS
    n = SC_SCATTER_CHUNK
    nch = per_worker // n
    assert nch % 2 == 0
    mesh = plsc.VectorSubcoreMesh(core_axis_name="c", subcore_axis_name="s")

    @functools.partial(
        pl.kernel, mesh=mesh,
        out_type=jax.ShapeDtypeStruct((n_slots, d), h2.dtype),
        scratch_types=[pltpu.VMEM((TOP_K, nch, n), jnp.int32), pltpu.VMEM((2, n, d), h2.dtype),
                       pltpu.SemaphoreType.DMA((2,)), pltpu.SemaphoreType.DMA((2,))],
        name="sc_scatter")
    def scatter_kernel(h_hbm, dest_hbm, xs_hbm, idx_v, rows_v, sem_in, sem_out):
        wid = lax.axis_index("s") * SC_CORES + lax.axis_index("c")
        base = wid * per_worker
        for k in range(TOP_K):
            pltpu.sync_copy(dest_hbm.at[k, pl.ds(wid * nch, nch)], idx_v.at[k])

        def load(c, s):
            return pltpu.make_async_copy(h_hbm.at[pl.ds(base + c * n, n)], rows_v.at[s],
                                         sem_in.at[s])

        def store(c, s, k):
            return pltpu.make_async_copy(rows_v.at[s], xs_hbm.at[idx_v.at[k, c]], sem_out.at[s])

        load(0, 0).start()

        @pl.loop(0, nch, step=2)
        def _(c):
            for s in range(2):
                cc = c + s

                @pl.when(cc >= 1)
                def _():
                    for k in range(TOP_K):
                        store(cc - 1, 1 - s, k).wait()

                @pl.when(cc + 1 < nch)
                def _():
                    load(cc + 1, 1 - s).start()

                load(cc, s).wait()
                for k in range(TOP_K):
                    store(cc, s, k).start()

        for k in range(TOP_K):
            store(nch - 1, (nch - 1) % 2, k).wait()

    return scatter_kernel(h2, dest.reshape(TOP_K, t // n, n))


def _expert_kernel(tile_ref, exp_ref, rows_ref, first_ref,
                   xs_ref, wgu_ref, bgu_ref, wd_ref, bd_ref, y_ref, wgu_bf, wd_bf):
    w = pl.program_id(0)

    @pl.when(first_ref[w] == 1)
    def _():
        wgu_bf[...] = wgu_ref[0].astype(BF16)
        wd_bf[...] = wd_ref[0].astype(BF16)

    n_rows = rows_ref[w]

    @pl.when(n_rows > 0)
    def _():
        row = lax.broadcasted_iota(jnp.int32, xs_ref.shape, 0)
        x = _unpack_bf16_pairs(jnp.where(row < n_rows, xs_ref[...], 0))
        gu = _dot(x, wgu_bf[...]) + bgu_ref[0]
        gate = jnp.minimum(gu[:, :D_FF], SWIGLU_LIMIT)
        up = jnp.clip(gu[:, D_FF:], -SWIGLU_LIMIT, SWIGLU_LIMIT)
        act = (up + 1.0) * (gate * jax.nn.sigmoid(gate * SWIGLU_ALPHA))
        y_ref[...] = _pack_bf16_pairs(_dot(act.astype(BF16), wd_bf[...]) + bd_ref[0])


def _experts(xs, plan, wgu, bgu, wd, bd):
    n_rows, dp = xs.shape
    d = 2 * dp
    w_tile, w_exp, w_rows, w_first = plan
    n_items = w_tile.shape[0]
    tm = ROW_TILE
    by_tile = lambda w, tl, ex, nr, fi: (tl[w], 0)
    by_expert = lambda w, tl, ex, nr, fi: (ex[w], 0, 0)
    grid_spec = pltpu.PrefetchScalarGridSpec(
        num_scalar_prefetch=4,
        grid=(n_items,),
        in_specs=[
            pl.BlockSpec((tm, dp), by_tile),
            pl.BlockSpec((1, d, 2 * D_FF), by_expert),
            pl.BlockSpec((1, 1, 2 * D_FF), by_expert),
            pl.BlockSpec((1, D_FF, d), by_expert),
            pl.BlockSpec((1, 1, d), by_expert),
        ],
        out_specs=pl.BlockSpec((tm, dp), by_tile),
        scratch_shapes=[pltpu.VMEM((d, 2 * D_FF), BF16), pltpu.VMEM((D_FF, d), BF16)],
    )
    return pl.pallas_call(
        _expert_kernel,
        grid_spec=grid_spec,
        out_shape=jax.ShapeDtypeStruct((n_rows, dp), jnp.int32),
        compiler_params=pltpu.CompilerParams(
            dimension_semantics=("arbitrary",),
            vmem_limit_bytes=VMEM_BYTES_V7X * 7 // 8),
        name="experts",
    )(w_tile, w_exp, w_rows, w_first, xs, wgu, bgu, wd, bd)


def _combine(x1, probs, dest, y, out_ref, row0):
    t, d = x1.shape
    per_worker = t // SC_WORKERS
    dp = d // 2
    n = SC_COMBINE_CHUNK
    nch = per_worker // n
    assert nch % 2 == 0
    mesh = plsc.VectorSubcoreMesh(core_axis_name="c", subcore_axis_name="s")
    high_half = -65536

    @functools.partial(
        pl.kernel, mesh=mesh,
        out_type=(),
        scratch_types=[pltpu.VMEM((TOP_K, per_worker), jnp.int32),
                       pltpu.VMEM((2, TOP_K, n, dp), jnp.int32),
                       pltpu.VMEM((2, n, d), F32), pltpu.VMEM((2, n, d), F32),
                       pltpu.VMEM((TOP_K, per_worker), F32),
                       pltpu.SemaphoreType.DMA((2,)), pltpu.SemaphoreType.DMA((2,))],
        compiler_params=pltpu.CompilerParams(needs_layout_passes=False),
        name="sc_combine")
    def combine_kernel(x1_hbm, p_hbm, dest_hbm, y_hbm, out_hbm,
                       idx_v, rows_v, x_v, o_v, p_v, sem_in, sem_out):
        wid = lax.axis_index("s") * SC_CORES + lax.axis_index("c")
        base = wid * per_worker
        for k in range(TOP_K):
            pltpu.sync_copy(dest_hbm.at[k, pl.ds(base, per_worker)], idx_v.at[k])
            pltpu.sync_copy(p_hbm.at[k, pl.ds(base, per_worker)], p_v.at[k])

        def loads(c, s):
            off = base + c * n
            cps = [pltpu.make_async_copy(y_hbm.at[idx_v.at[k, pl.ds(c * n, n)]], rows_v.at[s, k],
                                         sem_in.at[s])
                   for k in range(TOP_K)]
            cps.append(pltpu.make_async_copy(x1_hbm.at[pl.ds(off, n)], x_v.at[s], sem_in.at[s]))
            return cps

        def store(c, s):
            return pltpu.make_async_copy(o_v.at[s], out_hbm.at[pl.ds(row0 + base + c * n, n)],
                                         sem_out.at[s])

        for cp in loads(0, 0):
            cp.start()

        @pl.loop(0, nch, step=2)
        def _(c):
            for s in range(2):
                cc = c + s

                @pl.when(cc + 1 < nch)
                def _():
                    for cp in loads(cc + 1, 1 - s):
                        cp.start()

                for cp in loads(cc, s):
                    cp.wait()

                @pl.when(cc >= 2)
                def _():
                    store(cc - 2, s).wait()

                @pl.loop(0, n)
                def _(r):
                    tok = jnp.full((SC_LANES,), cc * n + r, jnp.int32)
                    w = [plsc.load_gather(p_v, [jnp.full((SC_LANES,), k, jnp.int32), tok])
                         for k in range(TOP_K)]
                    for g in range(dp // SC_LANES):
                        lo_sl = pl.ds(g * SC_LANES, SC_LANES)
                        hi_sl = pl.ds(dp + g * SC_LANES, SC_LANES)
                        lo = x_v[s, r, lo_sl]
                        hi = x_v[s, r, hi_sl]
                        for k in range(TOP_K):
                            word = rows_v[s, k, r, lo_sl]
                            lo = lo + w[k] * plsc.bitcast(word << 16, F32)
                            hi = hi + w[k] * plsc.bitcast(word & high_half, F32)
                        o_v[s, r, lo_sl] = lo
                        o_v[s, r, hi_sl] = hi

                store(cc, s).start()

        store(nch - 2, 0).wait()
        store(nch - 1, 1).wait()

    combine_kernel(x1, probs, dest, y, out_ref)


def _layer(x, norm1_g, w_in, q_norm_g, k_norm_g, rel_bias, sg_ln_g, sg_ln_b, sg_w, sg_b,
           w_branch_a, w_branch_b, w_out, norm2_g, router_w, router_b,
           w_gate_up, b_gate_up, w_down, b_down):
    bsz, seq, d = x.shape
    t = bsz * seq
    row = lambda v: v.reshape(1, -1).astype(F32)
    scale = 1.0 / math.sqrt(HEAD_DIM)
    head_sum = jnp.asarray(np.kron(np.eye(N_HEADS), np.ones((HEAD_DIM, HEAD_DIM))), BF16)
    sgb = jnp.repeat(sg_b.T.astype(F32), SG_WIDTH // SG_GROUPS, axis=1)

    mixer_consts = (
        row(norm1_g), w_in.astype(BF16), w_in[:, _C_V:_C_V + ATTN_WIDTH].T.astype(BF16),
        row(jnp.tile(q_norm_g, N_HEADS)) * scale, row(jnp.tile(k_norm_g, N_HEADS)),
        head_sum, _attention_bias(rel_bias), row(sg_ln_g), row(sg_ln_b),
        sg_w.astype(F32), sgb, w_branch_a.astype(BF16), w_branch_b.astype(BF16),
        w_out.astype(BF16), row(norm2_g), router_w.T.astype(F32),
        router_b.reshape(N_EXPERTS, 1).astype(F32))
    expert_params = (w_gate_up.astype(F32), b_gate_up.reshape(N_EXPERTS, 1, -1).astype(F32),
                     w_down.astype(F32), b_down.reshape(N_EXPERTS, 1, -1).astype(F32))

    groups = MOE_GROUPS if bsz % MOE_GROUPS == 0 else 1
    gb = bsz // groups
    tg = gb * seq
    n_tiles = tg * TOP_K // ROW_TILE + N_EXPERTS
    out_ref = jax.new_ref(lax.empty((t, d), F32))

    staged = []
    for g in range(groups):
        x1, h2, top_i, probs = _mixer(x, g * gb, gb, mixer_consts)
        dest, counts = _rank(top_i)
        plan = _plan(counts[:, 0], n_tiles)
        xs = _scatter(h2.reshape(tg, d // 2), dest, n_tiles * ROW_TILE)
        staged.append((x1.reshape(tg, d), probs, dest, plan, xs))
    for g, (x1, probs, dest, plan, xs) in enumerate(staged):
        y = _experts(xs, plan, *expert_params)
        _combine(x1, probs, dest, y, out_ref, g * tg)
    return jax.freeze(out_ref).reshape(bsz, seq, d)


def kernel(x, norm1_g, w_in, q_norm_g, k_norm_g, rel_bias, sg_ln_g, sg_ln_b, sg_w, sg_b,
           w_branch_a, w_branch_b, w_out, norm2_g, router_w, router_b,
           w_gate_up, b_gate_up, w_down, b_down):
    depth = norm1_g.shape[0]
    for l in range(depth):
        x = _layer(x, norm1_g[l], w_in[l], q_norm_g[l], k_norm_g[l], rel_bias[l], sg_ln_g[l],
                   sg_ln_b[l], sg_w[l], sg_b[l], w_branch_a[l], w_branch_b[l], w_out[l],
                   norm2_g[l], router_w[l], router_b[l], w_gate_up[l], b_gate_up[l],
                   w_down[l], b_down[l])
    return x
```

```python
import functools
import math

import numpy as np
import jax
import jax.numpy as jnp
from jax import lax
from jax.experimental import pallas as pl
from jax.experimental.pallas import tpu as pltpu
from jax.experimental.pallas import tpu_sc as plsc

F32 = jnp.float32
BF16 = jnp.bfloat16

D_MODEL = 1024
CHUNK = 64
N_BACK = 8
BAND = (N_BACK + 1) * CHUNK
N_HEADS = 8
HEAD_DIM = 64
ATTN_WIDTH = N_HEADS * HEAD_DIM
REL_CLIP = 256
SG_BLOCK = 128
SG_GROUPS = 4
SG_WIDTH = 512
IN_WIDTH = 3 * ATTN_WIDTH + 2 * SG_WIDTH + 2 * D_MODEL
N_EXPERTS = 32
TOP_K = 4
D_FF = D_MODEL
SWIGLU_LIMIT = 7.0
SWIGLU_ALPHA = 1.702
EPS = 1e-6
NEG = -1e30

LANES = 128
VMEM_BYTES_V7X = 64 * 1024 * 1024

SEQ_TILE = 512
Q_BLOCK = 2 * CHUNK
PREV = N_BACK * CHUNK
WIN = PREV + Q_BLOCK
GATE_CHUNK = 256
FINISH_ROWS = SEQ_TILE
RANK_TILE = 1024
ROW_TILE = 512
MOE_GROUPS = 1

SC_CORES = 2
SC_SUBCORES = 16
SC_LANES = 16
SC_WORKERS = SC_CORES * SC_SUBCORES
SC_SCATTER_CHUNK = 64
SC_COMBINE_CHUNK = 8

_C_Q = 0
_C_K = ATTN_WIDTH
_C_V = 2 * ATTN_WIDTH
_C_U = 3 * ATTN_WIDTH
_C_VS = _C_U + SG_WIDTH
_C_GA = _C_VS + SG_WIDTH
_C_GB = _C_GA + D_MODEL


def _dot(a, b):
    return jnp.dot(a, b, preferred_element_type=F32)


def _pack_bf16_pairs(v):
    w = v.shape[1] // 2
    lo = lax.bitcast_convert_type(v[:, :w].astype(BF16).astype(F32), jnp.uint32)
    hi = lax.bitcast_convert_type(v[:, w:].astype(BF16).astype(F32), jnp.uint32)
    return lax.bitcast_convert_type((lo >> 16) | hi, jnp.int32)


def _unpack_bf16_pairs(p):
    u = lax.bitcast_convert_type(p, jnp.uint32)
    lo = lax.bitcast_convert_type(u << 16, F32)
    hi = lax.bitcast_convert_type(u & jnp.uint32(0xFFFF0000), F32)
    return jnp.concatenate([lo, hi], axis=1).astype(BF16)


def _dot_nt(a, b, precision=None):
    return lax.dot_general(a, b, (((1,), (1,)), ((), ())),
                           preferred_element_type=F32, precision=precision)


def _mixer_kernel(x_ref, g1_ref, win_ref, wvt_ref, qg_ref, kg_ref, hsum_ref, bias_ref, lng_ref,
                  lnb_ref, sgw_ref, sgb_ref, wa_ref, wb_ref, wo_ref, g2_ref, rwt_ref, rb_ref,
                  x1_ref, h2_ref, ti_ref, pr_ref,
                  h_scr, q_scr, k_win, vt_win, ya_scr, u_scr, vs_scr, ysg_scr, ga_scr, mb_scr):
    ts = x_ref.shape[1]
    s_idx = pl.program_id(1)

    @pl.when(s_idx == 0)
    def _():
        k_win[0:PREV, :] = jnp.zeros((PREV, ATTN_WIDTH), BF16)
        vt_win[:, 0:PREV] = jnp.zeros((ATTN_WIDTH, PREV), BF16)

    @pl.when(s_idx > 0)
    def _():
        k_win[0:PREV, :] = k_win[ts:ts + PREV, :]
        vt_win[:, 0:PREV] = vt_win[:, ts:ts + PREV]

    x = x_ref[0]
    ms = jnp.mean(x * x, axis=-1, keepdims=True)
    h_scr[...] = (x * lax.rsqrt(ms + EPS) * g1_ref[...]).astype(BF16)

    def proj(lo, width):
        return _dot(h_scr[...], win_ref[:, lo:lo + width])

    def head_rms(t, g_ref):
        ssq = _dot((t * t).astype(BF16), hsum_ref[...])
        return t * lax.rsqrt(ssq * (1.0 / HEAD_DIM) + EPS) * g_ref[...]

    lane = lax.broadcasted_iota(jnp.int32, (1, ATTN_WIDTH), 1)
    even_head = (lane % (2 * HEAD_DIM)) < HEAD_DIM
    qn = head_rms(proj(_C_Q, ATTN_WIDTH), qg_ref)
    q_even = jnp.where(even_head, qn, 0.0).astype(BF16)
    q_odd = jnp.where(even_head, 0.0, qn).astype(BF16)
    for qb in range(ts // Q_BLOCK):
        rs = slice(qb * Q_BLOCK, (qb + 1) * Q_BLOCK)
        q_scr[qb, 0:Q_BLOCK, :] = q_even[rs]
        q_scr[qb, Q_BLOCK:2 * Q_BLOCK, :] = q_odd[rs]
    k_win[PREV:PREV + ts, :] = head_rms(proj(_C_K, ATTN_WIDTH), kg_ref).astype(BF16)
    vt_win[:, PREV:PREV + ts] = _dot_nt(wvt_ref[...], h_scr[...]).astype(BF16)

    def sg_u():
        u_scr[...] = jax.nn.gelu(proj(_C_U, SG_WIDTH))

    def sg_v():
        vs_scr[...] = jax.nn.gelu(proj(_C_VS, SG_WIDTH))

    gdim = SG_WIDTH // SG_GROUPS

    def sg_group(g):
        cs = slice(g * gdim, (g + 1) * gdim)
        vg = vs_scr[:, cs]
        mu = jnp.mean(vg, axis=-1, keepdims=True)
        xc = vg - mu
        var = jnp.mean(xc * xc, axis=-1, keepdims=True)
        vn = (xc * lax.rsqrt(var + EPS) * lng_ref[:, cs] + lnb_ref[:, cs]).astype(BF16)
        r_i = lax.broadcasted_iota(jnp.int32, (SG_BLOCK, SG_BLOCK), 0)
        c_i = lax.broadcasted_iota(jnp.int32, (SG_BLOCK, SG_BLOCK), 1)
        wm = jnp.where(r_i >= c_i, sgw_ref[g], 0.0).astype(BF16)
        for j in range(ts // SG_BLOCK):
            rs = slice(j * SG_BLOCK, (j + 1) * SG_BLOCK)
            mixed = _dot(wm, vn[rs]) + sgb_ref[:, cs]
            ysg_scr[rs, cs] = (u_scr[rs, cs] * mixed).astype(BF16)

    def gate_a_chunk(c):
        cols = slice(c * GATE_CHUNK, (c + 1) * GATE_CHUNK)
        ga_scr[:, cols] = jax.nn.sigmoid(proj(_C_GA + c * GATE_CHUNK, GATE_CHUNK))

    def gated_b_chunk(c):
        cols = slice(c * GATE_CHUNK, (c + 1) * GATE_CHUNK)
        gate = jax.nn.sigmoid(proj(_C_GB + c * GATE_CHUNK, GATE_CHUNK))
        mb_scr[:, cols] = gate * _dot(ysg_scr[...], wb_ref[:, cols])

    n_chunks = D_MODEL // GATE_CHUNK
    side_work = [sg_u, sg_v] + [functools.partial(sg_group, g) for g in range(SG_GROUPS)]
    side_work += [functools.partial(gate_a_chunk, c) for c in range(n_chunks)]
    side_work += [functools.partial(gated_b_chunk, c) for c in range(n_chunks)]

    def finish(grp):
        rs = slice(grp * FINISH_ROWS, (grp + 1) * FINISH_ROWS)
        merged = ga_scr[rs, :] * _dot(ya_scr[rs, :], wa_ref[...]) + mb_scr[rs, :]
        x1 = x_ref[0, rs, :] + _dot(merged.astype(BF16), wo_ref[...])
        x1_ref[0, rs, :] = x1
        ms2 = jnp.mean(x1 * x1, axis=-1, keepdims=True)
        h2 = x1 * lax.rsqrt(ms2 + EPS) * g2_ref[...]
        h2_ref[0, rs, :] = _pack_bf16_pairs(h2)
        h2_hi = h2.astype(BF16)
        h2_lo = (h2 - h2_hi.astype(F32)).astype(BF16)
        by_hi = _dot_nt(rwt_ref[...], h2_hi)
        lt = (by_hi[0:N_EXPERTS] + by_hi[N_EXPERTS:2 * N_EXPERTS]
              + _dot_nt(rwt_ref[0:N_EXPERTS, :], h2_lo) + rb_ref[...])
        e_iota = lax.broadcasted_iota(jnp.int32, lt.shape, 0)
        vals = []
        for k in range(TOP_K):
            m = jnp.max(lt, axis=0, keepdims=True)
            idx = jnp.min(jnp.where(lt == m, e_iota, N_EXPERTS), axis=0, keepdims=True)
            vals.append(m)
            ti_ref[k:k + 1, rs] = idx
            lt = jnp.where(e_iota == idx, -jnp.inf, lt)
        exps = [jnp.exp(v - vals[0]) for v in vals]
        denom = exps[0] + exps[1] + exps[2] + exps[3]
        for k in range(TOP_K):
            pr_ref[k:k + 1, rs] = exps[k] / denom

    pairs = N_HEADS // 2
    n_qb = ts // Q_BLOCK
    n_blocks = n_qb * pairs
    schedule = [[] for _ in range(n_blocks)]
    side_blocks = n_blocks if FINISH_ROWS == ts else n_blocks // 2
    for i, piece in enumerate(side_work):
        schedule[-(-(i + 1) * side_blocks // len(side_work)) - 1].append(piece)
    qb_per_group = FINISH_ROWS // Q_BLOCK
    for grp in range(ts // FINISH_ROWS):
        ready = max(side_blocks, (grp + 1) * qb_per_group * pairs) - 1
        schedule[min(n_blocks - 1, ready + 1)].append(functools.partial(finish, grp))

    key_row = lax.broadcasted_iota(jnp.int32, (WIN, 1), 0)
    for qb in range(ts // Q_BLOCK):
        r0 = qb * Q_BLOCK
        valid = key_row >= (PREV - (s_idx * ts + r0))
        for pair in range(N_HEADS // 2):
            cs = slice(pair * 2 * HEAD_DIM, (pair + 1) * 2 * HEAD_DIM)
            st = _dot_nt(k_win[r0:r0 + WIN, cs], q_scr[qb, :, cs]) + bias_ref[pair]
            st = jnp.where(valid, st, NEG)
            m = jnp.max(st, axis=0, keepdims=True)
            p = jnp.exp(st - m)
            inv = 1.0 / jnp.sum(p, axis=0, keepdims=True)
            ot = _dot(vt_win[cs, r0:r0 + WIN], p.astype(BF16))
            o = jnp.concatenate(
                [ot[0:HEAD_DIM, 0:Q_BLOCK] * inv[:, 0:Q_BLOCK],
                 ot[HEAD_DIM:2 * HEAD_DIM, Q_BLOCK:2 * Q_BLOCK] * inv[:, Q_BLOCK:2 * Q_BLOCK]],
                axis=0)
            ya_scr[r0:r0 + Q_BLOCK, cs] = o.T.astype(BF16)
            for piece in schedule[qb * (N_HEADS // 2) + pair]:
                piece()


def _const_spec(shape):
    zeros = (0,) * len(shape)
    return pl.BlockSpec(shape, lambda b, s: zeros, pipeline_mode=pl.Buffered(1))


def _mixer(x, batch0, bsz, consts):
    _, seq, d = x.shape
    ts = SEQ_TILE
    ns = seq // ts
    t = bsz * seq
    in_spec = pl.BlockSpec((1, ts, d), lambda b, s: (b + batch0, s, 0))
    tok_spec = pl.BlockSpec((1, ts, d), lambda b, s: (b, s, 0))
    packed_spec = pl.BlockSpec((1, ts, d // 2), lambda b, s: (b, s, 0))
    idx_spec = pl.BlockSpec((TOP_K, ts), lambda b, s: (0, b * ns + s))
    return pl.pallas_call(
        _mixer_kernel,
        grid=(bsz, ns),
        in_specs=[in_spec] + [_const_spec(c.shape) for c in consts],
        out_specs=[tok_spec, packed_spec, idx_spec, idx_spec],
        out_shape=[
            jax.ShapeDtypeStruct((bsz, seq, d), F32),
            jax.ShapeDtypeStruct((bsz, seq, d // 2), jnp.int32),
            jax.ShapeDtypeStruct((TOP_K, t), jnp.int32),
            jax.ShapeDtypeStruct((TOP_K, t), F32),
        ],
        scratch_shapes=[
            pltpu.VMEM((ts, d), BF16),
            pltpu.VMEM((ts // Q_BLOCK, 2 * Q_BLOCK, ATTN_WIDTH), BF16),
            pltpu.VMEM((PREV + ts, ATTN_WIDTH), BF16),
            pltpu.VMEM((ATTN_WIDTH, PREV + ts), BF16),
            pltpu.VMEM((ts, ATTN_WIDTH), BF16),
            pltpu.VMEM((ts, SG_WIDTH), F32),
            pltpu.VMEM((ts, SG_WIDTH), F32),
            pltpu.VMEM((ts, SG_WIDTH), BF16),
            pltpu.VMEM((ts, d), F32),
            pltpu.VMEM((ts, d), F32),
        ],
        compiler_params=pltpu.CompilerParams(
            dimension_semantics=("arbitrary", "arbitrary"),
            vmem_limit_bytes=VMEM_BYTES_V7X * 7 // 8),
        name="mixer",
    )(x, *consts)


def _attention_bias(rel_bias):
    i = np.arange(Q_BLOCK)[None, :]
    j = np.arange(WIN)[:, None]
    kk = j - (i // CHUNK) * CHUNK
    inside = np.tile((kk >= 0) & (kk < BAND), (1, 2))
    tab = rel_bias.astype(F32)
    far = PREV + Q_BLOCK - 1 - REL_CLIP + 1
    near = Q_BLOCK - 1 + REL_CLIP
    ext = jnp.concatenate(
        [tab[:, 2 * REL_CLIP - near:2 * REL_CLIP],
         jnp.broadcast_to(tab[:, 2 * REL_CLIP:], (N_HEADS, far))], axis=1)
    ext = ext.reshape(N_HEADS // 2, 2, far + near)
    cols = [ext[:, :, WIN - 1 - c:WIN - 1 - c + Q_BLOCK].reshape(N_HEADS // 2, 2 * Q_BLOCK)
            for c in range(WIN)]
    return jnp.where(inside[None], jnp.stack(cols, axis=1), NEG)


def _rank_kernel(ti_ref, upper_ref, dest_ref, cnt_ref, total_scr, base_scr):
    phase = pl.program_id(0)
    i = pl.program_id(1)
    tt = ti_ref.shape[1]

    @pl.when((phase == 0) & (i == 0))
    def _():
        total_scr[...] = jnp.zeros_like(total_scr)

    ti = ti_ref[...]
    e_iota = lax.broadcasted_iota(jnp.int32, (N_EXPERTS, tt), 0)
    hits = [e_iota == ti[k:k + 1, :] for k in range(TOP_K)]
    cnt = hits[0].astype(F32)
    for k in range(1, TOP_K):
        cnt = cnt + hits[k].astype(F32)
    tile_total = jnp.sum(cnt, axis=1, keepdims=True)

    @pl.when(phase == 0)
    def _():
        total_scr[...] = total_scr[...] + tile_total

    @pl.when((phase == 1) & (i == 0))
    def _():
        cnt_ref[...] = total_scr[...].astype(jnp.int32)
        padded = jnp.floor((total_scr[...] + (ROW_TILE - 1)) * (1.0 / ROW_TILE)) * ROW_TILE
        run = jnp.zeros((1, LANES), F32)
        for e in range(N_EXPERTS):
            base_scr[e:e + 1, :] = run
            run = run + padded[e:e + 1, :]

    @pl.when(phase == 1)
    def _():
        before = _dot(cnt.astype(BF16), upper_ref[...]) + base_scr[:, 0:1]
        for k in range(TOP_K):
            dest_ref[k:k + 1, :] = jnp.sum(jnp.where(hits[k], before, 0.0), axis=0,
                                           keepdims=True).astype(jnp.int32)
        base_scr[...] = base_scr[...] + tile_total


def _rank(top_i):
    t = top_i.shape[1]
    tt = RANK_TILE
    upper = jnp.asarray(np.triu(np.ones((tt, tt), np.float32), 1), BF16)
    return pl.pallas_call(
        _rank_kernel,
        grid=(2, t // tt),
        in_specs=[pl.BlockSpec((TOP_K, tt), lambda p, i: (0, i)),
                  pl.BlockSpec((tt, tt), lambda p, i: (0, 0), pipeline_mode=pl.Buffered(1))],
        out_specs=[pl.BlockSpec((TOP_K, tt), lambda p, i: (0, i * p)),
                   pl.BlockSpec((N_EXPERTS, LANES), lambda p, i: (0, 0))],
        out_shape=[jax.ShapeDtypeStruct((TOP_K, t), jnp.int32),
                   jax.ShapeDtypeStruct((N_EXPERTS, LANES), jnp.int32)],
        scratch_shapes=[pltpu.VMEM((N_EXPERTS, LANES), F32),
                        pltpu.VMEM((N_EXPERTS, LANES), F32)],
        compiler_params=pltpu.CompilerParams(dimension_semantics=("arbitrary", "arbitrary")),
        name="rank",
    )(top_i, upper)


def _plan_kernel(cnt_ref, tile_ref, exp_ref, rows_ref, first_ref):
    n_tiles = exp_ref.shape[0]

    def per_expert(e, w):
        c = cnt_ref[e]
        n = (c + ROW_TILE - 1) // ROW_TILE

        def per_tile(j, _):
            tile_ref[w + j] = w + j
            exp_ref[w + j] = e
            rows_ref[w + j] = jnp.minimum(c - j * ROW_TILE, ROW_TILE)
            first_ref[w + j] = (j == 0).astype(jnp.int32)
            return 0

        lax.fori_loop(0, n, per_tile, 0)
        return w + n

    n_used = lax.fori_loop(0, N_EXPERTS, per_expert, jnp.int32(0))

    def pad(w, _):
        tile_ref[w] = n_used - 1
        exp_ref[w] = exp_ref[n_used - 1]
        rows_ref[w] = 0
        first_ref[w] = 0
        return 0

    lax.fori_loop(n_used, n_tiles, pad, 0)


def _plan(counts, n_tiles):
    smem = pl.BlockSpec(memory_space=pltpu.SMEM)
    vec = jax.ShapeDtypeStruct((n_tiles,), jnp.int32)
    return pl.pallas_call(
        _plan_kernel,
        in_specs=[smem],
        out_specs=[smem, smem, smem, smem],
        out_shape=[vec, vec, vec, vec],
        name="plan",
    )(counts)


def _scatter(h2, dest, n_slots):
    t, d = h2.shape
    per_worker = t // SC_WORKERS
    n = SC_SCATTER_CHUNK
    nch = per_worker // n
    assert nch % 2 == 0
    mesh = plsc.VectorSubcoreMesh(core_axis_name="c", subcore_axis_name="s")

    @functools.partial(
        pl.kernel, mesh=mesh,
        out_type=jax.ShapeDtypeStruct((n_slots, d), h2.dtype),
        scratch_types=[pltpu.VMEM((TOP_K, nch, n), jnp.int32), pltpu.VMEM((2, n, d), h2.dtype),
                       pltpu.SemaphoreType.DMA((2,)), pltpu.SemaphoreType.DMA((2,))],
        name="sc_scatter")
    def scatter_kernel(h_hbm, dest_hbm, xs_hbm, idx_v, rows_v, sem_in, sem_out):
        wid = lax.axis_index("s") * SC_CORES + lax.axis_index("c")
        base = wid * per_worker
        for k in range(TOP_K):
            pltpu.sync_copy(dest_hbm.at[k, pl.ds(wid * nch, nch)], idx_v.at[k])

        def load(c, s):
            return pltpu.make_async_copy(h_hbm.at[pl.ds(base + c * n, n)], rows_v.at[s],
                                         sem_in.at[s])

        def store(c, s, k):
            return pltpu.make_async_copy(rows_v.at[s], xs_hbm.at[idx_v.at[k, c]], sem_out.at[s])

        load(0, 0).start()

        @pl.loop(0, nch, step=2)
        def _(c):
            for s in range(2):
                cc = c + s

                @pl.when(cc >= 1)
                def _():
                    for k in range(TOP_K):
                        store(cc - 1, 1 - s, k).wait()

                @pl.when(cc + 1 < nch)
                def _():
                    load(cc + 1, 1 - s).start()

                load(cc, s).wait()
                for k in range(TOP_K):
                    store(cc, s, k).start()

        for k in range(TOP_K):
            store(nch - 1, (nch - 1) % 2, k).wait()

    return scatter_kernel(h2, dest.reshape(TOP_K, t // n, n))


def _expert_kernel(tile_ref, exp_ref, rows_ref, first_ref,
                   xs_ref, wgu_ref, bgu_ref, wd_ref, bd_ref, y_ref, wgu_bf, wd_bf):
    w = pl.program_id(0)

    @pl.when(first_ref[w] == 1)
    def _():
        wgu_bf[...] = wgu_ref[0].astype(BF16)
        wd_bf[...] = wd_ref[0].astype(BF16)

    n_rows = rows_ref[w]

    @pl.when(n_rows > 0)
    def _():
        row = lax.broadcasted_iota(jnp.int32, xs_ref.shape, 0)
        x = _unpack_bf16_pairs(jnp.where(row < n_rows, xs_ref[...], 0))
        gu = _dot(x, wgu_bf[...]) + bgu_ref[0]
        gate = jnp.minimum(gu[:, :D_FF], SWIGLU_LIMIT)
        up = jnp.clip(gu[:, D_FF:], -SWIGLU_LIMIT, SWIGLU_LIMIT)
        act = (up + 1.0) * (gate * jax.nn.sigmoid(gate * SWIGLU_ALPHA))
        y_ref[...] = _pack_bf16_pairs(_dot(act.astype(BF16), wd_bf[...]) + bd_ref[0])


def _experts(xs, plan, wgu, bgu, wd, bd):
    n_rows, dp = xs.shape
    d = 2 * dp
    w_tile, w_exp, w_rows, w_first = plan
    n_items = w_tile.shape[0]
    tm = ROW_TILE
    by_tile = lambda w, tl, ex, nr, fi: (tl[w], 0)
    by_expert = lambda w, tl, ex, nr, fi: (ex[w], 0, 0)
    grid_spec = pltpu.PrefetchScalarGridSpec(
        num_scalar_prefetch=4,
        grid=(n_items,),
        in_specs=[
            pl.BlockSpec((tm, dp), by_tile),
            pl.BlockSpec((1, d, 2 * D_FF), by_expert),
            pl.BlockSpec((1, 1, 2 * D_FF), by_expert),
            pl.BlockSpec((1, D_FF, d), by_expert),
            pl.BlockSpec((1, 1, d), by_expert),
        ],
        out_specs=pl.BlockSpec((tm, dp), by_tile),
        scratch_shapes=[pltpu.VMEM((d, 2 * D_FF), BF16), pltpu.VMEM((D_FF, d), BF16)],
    )
    return pl.pallas_call(
        _expert_kernel,
        grid_spec=grid_spec,
        out_shape=jax.ShapeDtypeStruct((n_rows, dp), jnp.int32),
        compiler_params=pltpu.CompilerParams(
            dimension_semantics=("arbitrary",),
            vmem_limit_bytes=VMEM_BYTES_V7X * 7 // 8),
        name="experts",
    )(w_tile, w_exp, w_rows, w_first, xs, wgu, bgu, wd, bd)


def _combine(x1, probs, dest, y, out_ref, row0):
    t, d = x1.shape
    per_worker = t // SC_WORKERS
    dp = d // 2
    n = SC_COMBINE_CHUNK
    nch = per_worker // n
    assert nch % 2 == 0
    mesh = plsc.VectorSubcoreMesh(core_axis_name="c", subcore_axis_name="s")
    high_half = -65536

    @functools.partial(
        pl.kernel, mesh=mesh,
        out_type=(),
        scratch_types=[pltpu.VMEM((TOP_K, per_worker), jnp.int32),
                       pltpu.VMEM((2, TOP_K, n, dp), jnp.int32),
                       pltpu.VMEM((2, n, d), F32), pltpu.VMEM((2, n, d), F32),
                       pltpu.VMEM((TOP_K, per_worker), F32),
                       pltpu.SemaphoreType.DMA((2,)), pltpu.SemaphoreType.DMA((2,))],
        compiler_params=pltpu.CompilerParams(needs_layout_passes=False),
        name="sc_combine")
    def combine_kernel(x1_hbm, p_hbm, dest_hbm, y_hbm, out_hbm,
                       idx_v, rows_v, x_v, o_v, p_v, sem_in, sem_out):
        wid = lax.axis_index("s") * SC_CORES + lax.axis_index("c")
        base = wid * per_worker
        for k in range(TOP_K):
            pltpu.sync_copy(dest_hbm.at[k, pl.ds(base, per_worker)], idx_v.at[k])
            pltpu.sync_copy(p_hbm.at[k, pl.ds(base, per_worker)], p_v.at[k])

        def loads(c, s):
            off = base + c * n
            cps = [pltpu.make_async_copy(y_hbm.at[idx_v.at[k, pl.ds(c * n, n)]], rows_v.at[s, k],
                                         sem_in.at[s])
                   for k in range(TOP_K)]
            cps.append(pltpu.make_async_copy(x1_hbm.at[pl.ds(off, n)], x_v.at[s], sem_in.at[s]))
            return cps

        def store(c, s):
            return pltpu.make_async_copy(o_v.at[s], out_hbm.at[pl.ds(row0 + base + c * n, n)],
                                         sem_out.at[s])

        for cp in loads(0, 0):
            cp.start()

        @pl.loop(0, nch, step=2)
        def _(c):
            for s in range(2):
                cc = c + s

                @pl.when(cc + 1 < nch)
                def _():
                    for cp in loads(cc + 1, 1 - s):
                        cp.start()

                for cp in loads(cc, s):
                    cp.wait()

                @pl.when(cc >= 2)
                def _():
                    store(cc - 2, s).wait()

                @pl.loop(0, n)
                def _(r):
                    tok = jnp.full((SC_LANES,), cc * n + r, jnp.int32)
                    w = [plsc.load_gather(p_v, [jnp.full((SC_LANES,), k, jnp.int32), tok])
                         for k in range(TOP_K)]
                    for g in range(dp // SC_LANES):
                        lo_sl = pl.ds(g * SC_LANES, SC_LANES)
                        hi_sl = pl.ds(dp + g * SC_LANES, SC_LANES)
                        lo = x_v[s, r, lo_sl]
                        hi = x_v[s, r, hi_sl]
                        for k in range(TOP_K):
                            word = rows_v[s, k, r, lo_sl]
                            lo = lo + w[k] * plsc.bitcast(word << 16, F32)
                            hi = hi + w[k] * plsc.bitcast(word & high_half, F32)
                        o_v[s, r, lo_sl] = lo
                        o_v[s, r, hi_sl] = hi

                store(cc, s).start()

        store(nch - 2, 0).wait()
        store(nch - 1, 1).wait()

    combine_kernel(x1, probs, dest, y, out_ref)


def _layer(x, norm1_g, w_in, q_norm_g, k_norm_g, rel_bias, sg_ln_g, sg_ln_b, sg_w, sg_b,
           w_branch_a, w_branch_b, w_out, norm2_g, router_w, router_b,
           w_gate_up, b_gate_up, w_down, b_down):
    bsz, seq, d = x.shape
    t = bsz * seq
    row = lambda v: v.reshape(1, -1).astype(F32)
    scale = 1.0 / math.sqrt(HEAD_DIM)
    head_sum = jnp.asarray(np.kron(np.eye(N_HEADS), np.ones((HEAD_DIM, HEAD_DIM))), BF16)
    sgb = jnp.repeat(sg_b.T.astype(F32), SG_WIDTH // SG_GROUPS, axis=1)

    rw = router_w.T.astype(F32)
    rw_hi = rw.astype(BF16)
    rw_lo = (rw - rw_hi.astype(F32)).astype(BF16)
    mixer_consts = (
        row(norm1_g), w_in.astype(BF16), w_in[:, _C_V:_C_V + ATTN_WIDTH].T.astype(BF16),
        row(jnp.tile(q_norm_g, N_HEADS)) * scale, row(jnp.tile(k_norm_g, N_HEADS)),
        head_sum, _attention_bias(rel_bias), row(sg_ln_g), row(sg_ln_b),
        sg_w.astype(F32), sgb, w_branch_a.astype(BF16), w_branch_b.astype(BF16),
        w_out.astype(BF16), row(norm2_g), jnp.concatenate([rw_hi, rw_lo], axis=0),
        router_b.reshape(N_EXPERTS, 1).astype(F32))
    expert_params = (w_gate_up.astype(F32), b_gate_up.reshape(N_EXPERTS, 1, -1).astype(F32),
                     w_down.astype(F32), b_down.reshape(N_EXPERTS, 1, -1).astype(F32))

    groups = MOE_GROUPS if bsz % MOE_GROUPS == 0 else 1
    gb = bsz // groups
    tg = gb * seq
    n_tiles = tg * TOP_K // ROW_TILE + N_EXPERTS
    out_ref = jax.new_ref(lax.empty((t, d), F32))

    staged = []
    for g in range(groups):
        x1, h2, top_i, probs = _mixer(x, g * gb, gb, mixer_consts)
        dest, counts = _rank(top_i)
        plan = _plan(counts[:, 0], n_tiles)
        xs = _scatter(h2.reshape(tg, d // 2), dest, n_tiles * ROW_TILE)
        staged.append((x1.reshape(tg, d), probs, dest, plan, xs))
    for g, (x1, probs, dest, plan, xs) in enumerate(staged):
        y = _experts(xs, plan, *expert_params)
        _combine(x1, probs, dest, y, out_ref, g * tg)
    return jax.freeze(out_ref).reshape(bsz, seq, d)


def kernel(x, norm1_g, w_in, q_norm_g, k_norm_g, rel_bias, sg_ln_g, sg_ln_b, sg_w, sg_b,
           w_branch_a, w_branch_b, w_out, norm2_g, router_w, router_b,
           w_gate_up, b_gate_up, w_down, b_down):
    depth = norm1_g.shape[0]
    for l in range(depth):
        x = _layer(x, norm1_g[l], w_in[l], q_norm_g[l], k_norm_g[l], rel_bias[l], sg_ln_g[l],
                   sg_ln_b[l], sg_w[l], sg_b[l], w_branch_a[l], w_branch_b[l], w_out[l],
                   norm2_g[l], router_w[l], router_b[l], w_gate_up[l], b_gate_up[l],
                   w_down[l], b_down[l])
    return x
```

```python
import functools
import math

import numpy as np
import jax
import jax.numpy as jnp
from jax import lax
from jax.experimental import pallas as pl
from jax.experimental.pallas import tpu as pltpu
from jax.experimental.pallas import tpu_sc as plsc

F32 = jnp.float32
BF16 = jnp.bfloat16

D_MODEL = 1024
CHUNK = 64
N_BACK = 8
BAND = (N_BACK + 1) * CHUNK
N_HEADS = 8
HEAD_DIM = 64
ATTN_WIDTH = N_HEADS * HEAD_DIM
REL_CLIP = 256
SG_BLOCK = 128
SG_GROUPS = 4
SG_WIDTH = 512
IN_WIDTH = 3 * ATTN_WIDTH + 2 * SG_WIDTH + 2 * D_MODEL
N_EXPERTS = 32
TOP_K = 4
D_FF = D_MODEL
SWIGLU_LIMIT = 7.0
SWIGLU_ALPHA = 1.702
EPS = 1e-6
NEG = -1e30

LANES = 128
VMEM_BYTES_V7X = 64 * 1024 * 1024

SEQ_TILE = 512
Q_BLOCK = 2 * CHUNK
PREV = N_BACK * CHUNK
WIN = PREV + Q_BLOCK
GATE_CHUNK = 256
FINISH_ROWS = SEQ_TILE
RANK_TILE = 1024
ROW_TILE = 512
MOE_GROUPS = 1

SC_CORES = 2
SC_SUBCORES = 16
SC_LANES = 16
SC_WORKERS = SC_CORES * SC_SUBCORES
SC_SCATTER_CHUNK = 64
SC_COMBINE_CHUNK = 8

_C_Q = 0
_C_K = ATTN_WIDTH
_C_V = 2 * ATTN_WIDTH
_C_U = 3 * ATTN_WIDTH
_C_VS = _C_U + SG_WIDTH
_C_GA = _C_VS + SG_WIDTH
_C_GB = _C_GA + D_MODEL


def _dot(a, b):
    return jnp.dot(a, b, preferred_element_type=F32)


def _pack_bf16_pairs(v):
    w = v.shape[1] // 2
    lo = lax.bitcast_convert_type(v[:, :w].astype(BF16).astype(F32), jnp.uint32)
    hi = lax.bitcast_convert_type(v[:, w:].astype(BF16).astype(F32), jnp.uint32)
    return lax.bitcast_convert_type((lo >> 16) | hi, jnp.int32)


def _unpack_bf16_pairs(p):
    u = lax.bitcast_convert_type(p, jnp.uint32)
    lo = lax.bitcast_convert_type(u << 16, F32)
    hi = lax.bitcast_convert_type(u & jnp.uint32(0xFFFF0000), F32)
    return jnp.concatenate([lo, hi], axis=1).astype(BF16)


def _dot_nt(a, b, precision=None):
    return lax.dot_general(a, b, (((1,), (1,)), ((), ())),
                           preferred_element_type=F32, precision=precision)


def _mixer_kernel(x_ref, g1_ref, win_ref, wvt_ref, qg_ref, kg_ref, bias_ref, lng_ref,
                  lnb_ref, sgw_ref, sgb_ref, wa_ref, wb_ref, wo_ref, g2_ref, rwt_ref, rb_ref,
                  x1_ref, h2_ref, ti_ref, pr_ref,
                  h_scr, q_scr, k_win, vt_win, ya_scr, u_scr, vs_scr, ysg_scr, ga_scr, mb_scr):
    ts = x_ref.shape[1]
    s_idx = pl.program_id(1)

    @pl.when(s_idx == 0)
    def _():
        k_win[0:PREV, :] = jnp.zeros((PREV, ATTN_WIDTH), BF16)
        vt_win[:, 0:PREV] = jnp.zeros((ATTN_WIDTH, PREV), BF16)

    @pl.when(s_idx > 0)
    def _():
        k_win[0:PREV, :] = k_win[ts:ts + PREV, :]
        vt_win[:, 0:PREV] = vt_win[:, ts:ts + PREV]

    x = x_ref[0]
    ms = jnp.mean(x * x, axis=-1, keepdims=True)
    h_scr[...] = (x * lax.rsqrt(ms + EPS) * g1_ref[...]).astype(BF16)

    def proj(lo, width):
        return _dot(h_scr[...], win_ref[:, lo:lo + width])

    def head_rms(t, g_ref):
        low = lax.broadcasted_iota(jnp.int32, (1, 2 * HEAD_DIM), 1) < HEAD_DIM
        cols = []
        for p in range(N_HEADS // 2):
            blk = t[:, p * 2 * HEAD_DIM:(p + 1) * 2 * HEAD_DIM]
            sq = blk * blk
            s_lo = jnp.sum(jnp.where(low, sq, 0.0), axis=-1, keepdims=True)
            s_hi = jnp.sum(jnp.where(low, 0.0, sq), axis=-1, keepdims=True)
            r = lax.rsqrt(jnp.where(low, s_lo, s_hi) * (1.0 / HEAD_DIM) + EPS)
            cols.append(blk * r)
        return jnp.concatenate(cols, axis=1) * g_ref[...]

    lane = lax.broadcasted_iota(jnp.int32, (1, ATTN_WIDTH), 1)
    even_head = (lane % (2 * HEAD_DIM)) < HEAD_DIM
    qn = head_rms(proj(_C_Q, ATTN_WIDTH), qg_ref)
    q_even = jnp.where(even_head, qn, 0.0).astype(BF16)
    q_odd = jnp.where(even_head, 0.0, qn).astype(BF16)
    for qb in range(ts // Q_BLOCK):
        rs = slice(qb * Q_BLOCK, (qb + 1) * Q_BLOCK)
        q_scr[qb, 0:Q_BLOCK, :] = q_even[rs]
        q_scr[qb, Q_BLOCK:2 * Q_BLOCK, :] = q_odd[rs]
    k_win[PREV:PREV + ts, :] = head_rms(proj(_C_K, ATTN_WIDTH), kg_ref).astype(BF16)
    vt_win[:, PREV:PREV + ts] = _dot_nt(wvt_ref[...], h_scr[...]).astype(BF16)

    def sg_u():
        u_scr[...] = jax.nn.gelu(proj(_C_U, SG_WIDTH))

    def sg_v():
        vs_scr[...] = jax.nn.gelu(proj(_C_VS, SG_WIDTH))

    gdim = SG_WIDTH // SG_GROUPS

    def sg_group(g):
        cs = slice(g * gdim, (g + 1) * gdim)
        vg = vs_scr[:, cs]
        mu = jnp.mean(vg, axis=-1, keepdims=True)
        xc = vg - mu
        var = jnp.mean(xc * xc, axis=-1, keepdims=True)
        vn = (xc * lax.rsqrt(var + EPS) * lng_ref[:, cs] + lnb_ref[:, cs]).astype(BF16)
        r_i = lax.broadcasted_iota(jnp.int32, (SG_BLOCK, SG_BLOCK), 0)
        c_i = lax.broadcasted_iota(jnp.int32, (SG_BLOCK, SG_BLOCK), 1)
        wm = jnp.where(r_i >= c_i, sgw_ref[g], 0.0).astype(BF16)
        for j in range(ts // SG_BLOCK):
            rs = slice(j * SG_BLOCK, (j + 1) * SG_BLOCK)
            mixed = _dot(wm, vn[rs]) + sgb_ref[:, cs]
            ysg_scr[rs, cs] = (u_scr[rs, cs] * mixed).astype(BF16)

    def gate_a_chunk(c):
        cols = slice(c * GATE_CHUNK, (c + 1) * GATE_CHUNK)
        ga_scr[:, cols] = jax.nn.sigmoid(proj(_C_GA + c * GATE_CHUNK, GATE_CHUNK))

    def gated_b_chunk(c):
        cols = slice(c * GATE_CHUNK, (c + 1) * GATE_CHUNK)
        gate = jax.nn.sigmoid(proj(_C_GB + c * GATE_CHUNK, GATE_CHUNK))
        mb_scr[:, cols] = gate * _dot(ysg_scr[...], wb_ref[:, cols])

    n_chunks = D_MODEL // GATE_CHUNK
    side_work = [sg_u, sg_v] + [functools.partial(sg_group, g) for g in range(SG_GROUPS)]
    side_work += [functools.partial(gate_a_chunk, c) for c in range(n_chunks)]
    side_work += [functools.partial(gated_b_chunk, c) for c in range(n_chunks)]

    def finish(grp):
        rs = slice(grp * FINISH_ROWS, (grp + 1) * FINISH_ROWS)
        merged = ga_scr[rs, :] * _dot(ya_scr[rs, :], wa_ref[...]) + mb_scr[rs, :]
        x1 = x_ref[0, rs, :] + _dot(merged.astype(BF16), wo_ref[...])
        x1_ref[0, rs, :] = x1
        ms2 = jnp.mean(x1 * x1, axis=-1, keepdims=True)
        h2 = x1 * lax.rsqrt(ms2 + EPS) * g2_ref[...]
        h2_ref[0, rs, :] = _pack_bf16_pairs(h2)
        h2_hi = h2.astype(BF16)
        h2_lo = (h2 - h2_hi.astype(F32)).astype(BF16)
        by_hi = _dot_nt(rwt_ref[...], h2_hi)
        lt = (by_hi[0:N_EXPERTS] + by_hi[N_EXPERTS:2 * N_EXPERTS]
              + _dot_nt(rwt_ref[0:N_EXPERTS, :], h2_lo) + rb_ref[...])
        e_iota = lax.broadcasted_iota(jnp.int32, lt.shape, 0)
        vals = []
        for k in range(TOP_K):
            m = jnp.max(lt, axis=0, keepdims=True)
            idx = jnp.min(jnp.where(lt == m, e_iota, N_EXPERTS), axis=0, keepdims=True)
            vals.append(m)
            ti_ref[k:k + 1, rs] = idx
            lt = jnp.where(e_iota == idx, -jnp.inf, lt)
        exps = [jnp.exp(v - vals[0]) for v in vals]
        denom = exps[0] + exps[1] + exps[2] + exps[3]
        for k in range(TOP_K):
            pr_ref[k:k + 1, rs] = exps[k] / denom

    pairs = N_HEADS // 2
    n_qb = ts // Q_BLOCK
    n_blocks = n_qb * pairs
    schedule = [[] for _ in range(n_blocks)]
    side_blocks = n_blocks if FINISH_ROWS == ts else n_blocks // 2
    for i, piece in enumerate(side_work):
        schedule[-(-(i + 1) * side_blocks // len(side_work)) - 1].append(piece)
    qb_per_group = FINISH_ROWS // Q_BLOCK
    for grp in range(ts // FINISH_ROWS):
        ready = max(side_blocks, (grp + 1) * qb_per_group * pairs) - 1
        schedule[min(n_blocks - 1, ready + 1)].append(functools.partial(finish, grp))

    key_row = lax.broadcasted_iota(jnp.int32, (WIN, 1), 0)
    for qb in range(ts // Q_BLOCK):
        r0 = qb * Q_BLOCK
        valid = key_row >= (PREV - (s_idx * ts + r0))
        for pair in range(N_HEADS // 2):
            cs = slice(pair * 2 * HEAD_DIM, (pair + 1) * 2 * HEAD_DIM)
            st = _dot_nt(k_win[r0:r0 + WIN, cs], q_scr[qb, :, cs]) + bias_ref[pair]
            st = jnp.where(valid, st, NEG)
            m = jnp.max(st, axis=0, keepdims=True)
            p = jnp.exp(st - m)
            inv = 1.0 / jnp.sum(p, axis=0, keepdims=True)
            ot = _dot(vt_win[cs, r0:r0 + WIN], p.astype(BF16))
            o = jnp.concatenate(
                [ot[0:HEAD_DIM, 0:Q_BLOCK] * inv[:, 0:Q_BLOCK],
                 ot[HEAD_DIM:2 * HEAD_DIM, Q_BLOCK:2 * Q_BLOCK] * inv[:, Q_BLOCK:2 * Q_BLOCK]],
                axis=0)
            ya_scr[r0:r0 + Q_BLOCK, cs] = o.T.astype(BF16)
            for piece in schedule[qb * (N_HEADS // 2) + pair]:
                piece()


def _const_spec(shape):
    zeros = (0,) * len(shape)
    return pl.BlockSpec(shape, lambda b, s: zeros, pipeline_mode=pl.Buffered(1))


def _mixer(x, batch0, bsz, consts):
    _, seq, d = x.shape
    ts = SEQ_TILE
    ns = seq // ts
    t = bsz * seq
    in_spec = pl.BlockSpec((1, ts, d), lambda b, s: (b + batch0, s, 0))
    tok_spec = pl.BlockSpec((1, ts, d), lambda b, s: (b, s, 0))
    packed_spec = pl.BlockSpec((1, ts, d // 2), lambda b, s: (b, s, 0))
    idx_spec = pl.BlockSpec((TOP_K, ts), lambda b, s: (0, b * ns + s))
    return pl.pallas_call(
        _mixer_kernel,
        grid=(bsz, ns),
        in_specs=[in_spec] + [_const_spec(c.shape) for c in consts],
        out_specs=[tok_spec, packed_spec, idx_spec, idx_spec],
        out_shape=[
            jax.ShapeDtypeStruct((bsz, seq, d), F32),
            jax.ShapeDtypeStruct((bsz, seq, d // 2), jnp.int32),
            jax.ShapeDtypeStruct((TOP_K, t), jnp.int32),
            jax.ShapeDtypeStruct((TOP_K, t), F32),
        ],
        scratch_shapes=[
            pltpu.VMEM((ts, d), BF16),
            pltpu.VMEM((ts // Q_BLOCK, 2 * Q_BLOCK, ATTN_WIDTH), BF16),
            pltpu.VMEM((PREV + ts, ATTN_WIDTH), BF16),
            pltpu.VMEM((ATTN_WIDTH, PREV + ts), BF16),
            pltpu.VMEM((ts, ATTN_WIDTH), BF16),
            pltpu.VMEM((ts, SG_WIDTH), F32),
            pltpu.VMEM((ts, SG_WIDTH), F32),
            pltpu.VMEM((ts, SG_WIDTH), BF16),
            pltpu.VMEM((ts, d), F32),
            pltpu.VMEM((ts, d), F32),
        ],
        compiler_params=pltpu.CompilerParams(
            dimension_semantics=("arbitrary", "arbitrary"),
            vmem_limit_bytes=VMEM_BYTES_V7X * 7 // 8),
        name="mixer",
    )(x, *consts)


def _attention_bias(rel_bias):
    i = np.arange(Q_BLOCK)[:, None]
    j = np.arange(WIN)[None, :]
    kk = j - (i // CHUNK) * CHUNK
    inside = (kk >= 0) & (kk < BAND)
    tab = rel_bias.astype(F32)
    far = PREV + Q_BLOCK - 1 - REL_CLIP + 1
    near = Q_BLOCK - 1 + REL_CLIP
    ext = jnp.concatenate(
        [jnp.broadcast_to(tab[:, 2 * REL_CLIP:], (N_HEADS, far)),
         tab[:, 2 * REL_CLIP - near:2 * REL_CLIP][:, ::-1]], axis=1)
    rows = [ext[:, Q_BLOCK - 1 - r:Q_BLOCK - 1 - r + WIN] for r in range(Q_BLOCK)]
    bias = jnp.where(inside[None], jnp.stack(rows, axis=1), NEG)
    return (bias.reshape(N_HEADS // 2, 2 * Q_BLOCK, WIN).transpose(0, 2, 1))


def _rank_kernel(ti_ref, upper_ref, dest_ref, cnt_ref, total_scr, base_scr):
    phase = pl.program_id(0)
    i = pl.program_id(1)
    tt = ti_ref.shape[1]

    @pl.when((phase == 0) & (i == 0))
    def _():
        total_scr[...] = jnp.zeros_like(total_scr)

    ti = ti_ref[...]
    e_iota = lax.broadcasted_iota(jnp.int32, (N_EXPERTS, tt), 0)
    hits = [e_iota == ti[k:k + 1, :] for k in range(TOP_K)]
    cnt = hits[0].astype(F32)
    for k in range(1, TOP_K):
        cnt = cnt + hits[k].astype(F32)
    tile_total = jnp.sum(cnt, axis=1, keepdims=True)

    @pl.when(phase == 0)
    def _():
        total_scr[...] = total_scr[...] + tile_total

    @pl.when((phase == 1) & (i == 0))
    def _():
        cnt_ref[...] = total_scr[...].astype(jnp.int32)
        padded = jnp.floor((total_scr[...] + (ROW_TILE - 1)) * (1.0 / ROW_TILE)) * ROW_TILE
        run = jnp.zeros((1, LANES), F32)
        for e in range(N_EXPERTS):
            base_scr[e:e + 1, :] = run
            run = run + padded[e:e + 1, :]

    @pl.when(phase == 1)
    def _():
        before = _dot(cnt.astype(BF16), upper_ref[...]) + base_scr[:, 0:1]
        for k in range(TOP_K):
            dest_ref[k:k + 1, :] = jnp.sum(jnp.where(hits[k], before, 0.0), axis=0,
                                           keepdims=True).astype(jnp.int32)
        base_scr[...] = base_scr[...] + tile_total


def _rank(top_i):
    t = top_i.shape[1]
    tt = RANK_TILE
    upper = jnp.asarray(np.triu(np.ones((tt, tt), np.float32), 1), BF16)
    return pl.pallas_call(
        _rank_kernel,
        grid=(2, t // tt),
        in_specs=[pl.BlockSpec((TOP_K, tt), lambda p, i: (0, i)),
                  pl.BlockSpec((tt, tt), lambda p, i: (0, 0), pipeline_mode=pl.Buffered(1))],
        out_specs=[pl.BlockSpec((TOP_K, tt), lambda p, i: (0, i * p)),
                   pl.BlockSpec((N_EXPERTS, LANES), lambda p, i: (0, 0))],
        out_shape=[jax.ShapeDtypeStruct((TOP_K, t), jnp.int32),
                   jax.ShapeDtypeStruct((N_EXPERTS, LANES), jnp.int32)],
        scratch_shapes=[pltpu.VMEM((N_EXPERTS, LANES), F32),
                        pltpu.VMEM((N_EXPERTS, LANES), F32)],
        compiler_params=pltpu.CompilerParams(dimension_semantics=("arbitrary", "arbitrary")),
        name="rank",
    )(top_i, upper)


def _plan_kernel(cnt_ref, tile_ref, exp_ref, rows_ref, first_ref):
    n_tiles = exp_ref.shape[0]

    def per_expert(e, w):
        c = cnt_ref[e]
        n = (c + ROW_TILE - 1) // ROW_TILE

        def per_tile(j, _):
            tile_ref[w + j] = w + j
            exp_ref[w + j] = e
            rows_ref[w + j] = jnp.minimum(c - j * ROW_TILE, ROW_TILE)
            first_ref[w + j] = (j == 0).astype(jnp.int32)
            return 0

        lax.fori_loop(0, n, per_tile, 0)
        return w + n

    n_used = lax.fori_loop(0, N_EXPERTS, per_expert, jnp.int32(0))

    def pad(w, _):
        tile_ref[w] = n_used - 1
        exp_ref[w] = exp_ref[n_used - 1]
        rows_ref[w] = 0
        first_ref[w] = 0
        return 0

    lax.fori_loop(n_used, n_tiles, pad, 0)


def _plan(counts, n_tiles):
    smem = pl.BlockSpec(memory_space=pltpu.SMEM)
    vec = jax.ShapeDtypeStruct((n_tiles,), jnp.int32)
    return pl.pallas_call(
        _plan_kernel,
        in_specs=[smem],
        out_specs=[smem, smem, smem, smem],
        out_shape=[vec, vec, vec, vec],
        name="plan",
    )(counts)


def _scatter(h2, dest, n_slots):
    t, d = h2.shape
    per_worker = t // SC_WORKERS
    n = SC_SCATTER_CHUNK
    nch = per_worker // n
    assert nch % 2 == 0
    mesh = plsc.VectorSubcoreMesh(core_axis_name="c", subcore_axis_name="s")

    @functools.partial(
        pl.kernel, mesh=mesh,
        out_type=jax.ShapeDtypeStruct((n_slots, d), h2.dtype),
        scratch_types=[pltpu.VMEM((TOP_K, nch, n), jnp.int32), pltpu.VMEM((2, n, d), h2.dtype),
                       pltpu.SemaphoreType.DMA((2,)), pltpu.SemaphoreType.DMA((2,))],
        name="sc_scatter")
    def scatter_kernel(h_hbm, dest_hbm, xs_hbm, idx_v, rows_v, sem_in, sem_out):
        wid = lax.axis_index("s") * SC_CORES + lax.axis_index("c")
        base = wid * per_worker
        for k in range(TOP_K):
            pltpu.sync_copy(dest_hbm.at[k, pl.ds(wid * nch, nch)], idx_v.at[k])

        def load(c, s):
            return pltpu.make_async_copy(h_hbm.at[pl.ds(base + c * n, n)], rows_v.at[s],
                                         sem_in.at[s])

        def store(c, s, k):
            return pltpu.make_async_copy(rows_v.at[s], xs_hbm.at[idx_v.at[k, c]], sem_out.at[s])

        load(0, 0).start()

        @pl.loop(0, nch, step=2)
        def _(c):
            for s in range(2):
                cc = c + s

                @pl.when(cc >= 1)
                def _():
                    for k in range(TOP_K):
                        store(cc - 1, 1 - s, k).wait()

                @pl.when(cc + 1 < nch)
                def _():
                    load(cc + 1, 1 - s).start()

                load(cc, s).wait()
                for k in range(TOP_K):
                    store(cc, s, k).start()

        for k in range(TOP_K):
            store(nch - 1, (nch - 1) % 2, k).wait()

    return scatter_kernel(h2, dest.reshape(TOP_K, t // n, n))


def _expert_kernel(tile_ref, exp_ref, rows_ref, first_ref,
                   xs_ref, wgu_ref, bgu_ref, wd_ref, bd_ref, y_ref, wgu_bf, wd_bf):
    w = pl.program_id(0)

    @pl.when(first_ref[w] == 1)
    def _():
        wgu_bf[...] = wgu_ref[0].astype(BF16)
        wd_bf[...] = wd_ref[0].astype(BF16)

    n_rows = rows_ref[w]

    @pl.when(n_rows > 0)
    def _():
        row = lax.broadcasted_iota(jnp.int32, xs_ref.shape, 0)
        x = _unpack_bf16_pairs(jnp.where(row < n_rows, xs_ref[...], 0))
        gu = _dot(x, wgu_bf[...]) + bgu_ref[0]
        gate = jnp.minimum(gu[:, :D_FF], SWIGLU_LIMIT)
        up = jnp.clip(gu[:, D_FF:], -SWIGLU_LIMIT, SWIGLU_LIMIT)
        act = (up + 1.0) * (gate * jax.nn.sigmoid(gate * SWIGLU_ALPHA))
        y_ref[...] = _pack_bf16_pairs(_dot(act.astype(BF16), wd_bf[...]) + bd_ref[0])


def _experts(xs, plan, wgu, bgu, wd, bd):
    n_rows, dp = xs.shape
    d = 2 * dp
    w_tile, w_exp, w_rows, w_first = plan
    n_items = w_tile.shape[0]
    tm = ROW_TILE
    by_tile = lambda w, tl, ex, nr, fi: (tl[w], 0)
    by_expert = lambda w, tl, ex, nr, fi: (ex[w], 0, 0)
    grid_spec = pltpu.PrefetchScalarGridSpec(
        num_scalar_prefetch=4,
        grid=(n_items,),
        in_specs=[
            pl.BlockSpec((tm, dp), by_tile),
            pl.BlockSpec((1, d, 2 * D_FF), by_expert),
            pl.BlockSpec((1, 1, 2 * D_FF), by_expert),
            pl.BlockSpec((1, D_FF, d), by_expert),
            pl.BlockSpec((1, 1, d), by_expert),
        ],
        out_specs=pl.BlockSpec((tm, dp), by_tile),
        scratch_shapes=[pltpu.VMEM((d, 2 * D_FF), BF16), pltpu.VMEM((D_FF, d), BF16)],
    )
    return pl.pallas_call(
        _expert_kernel,
        grid_spec=grid_spec,
        out_shape=jax.ShapeDtypeStruct((n_rows, dp), jnp.int32),
        compiler_params=pltpu.CompilerParams(
            dimension_semantics=("arbitrary",),
            vmem_limit_bytes=VMEM_BYTES_V7X * 7 // 8),
        name="experts",
    )(w_tile, w_exp, w_rows, w_first, xs, wgu, bgu, wd, bd)


def _combine(x1, probs, dest, y, out_ref, row0):
    t, d = x1.shape
    per_worker = t // SC_WORKERS
    dp = d // 2
    n = SC_COMBINE_CHUNK
    nch = per_worker // n
    assert nch % 2 == 0
    mesh = plsc.VectorSubcoreMesh(core_axis_name="c", subcore_axis_name="s")
    high_half = -65536

    @functools.partial(
        pl.kernel, mesh=mesh,
        out_type=(),
        scratch_types=[pltpu.VMEM((TOP_K, per_worker), jnp.int32),
                       pltpu.VMEM((2, TOP_K, n, dp), jnp.int32),
                       pltpu.VMEM((2, n, d), F32), pltpu.VMEM((2, n, d), F32),
                       pltpu.VMEM((TOP_K, per_worker), F32),
                       pltpu.SemaphoreType.DMA((2,)), pltpu.SemaphoreType.DMA((2,))],
        compiler_params=pltpu.CompilerParams(needs_layout_passes=False),
        name="sc_combine")
    def combine_kernel(x1_hbm, p_hbm, dest_hbm, y_hbm, out_hbm,
                       idx_v, rows_v, x_v, o_v, p_v, sem_in, sem_out):
        wid = lax.axis_index("s") * SC_CORES + lax.axis_index("c")
        base = wid * per_worker
        for k in range(TOP_K):
            pltpu.sync_copy(dest_hbm.at[k, pl.ds(base, per_worker)], idx_v.at[k])
            pltpu.sync_copy(p_hbm.at[k, pl.ds(base, per_worker)], p_v.at[k])

        def loads(c, s):
            off = base + c * n
            cps = [pltpu.make_async_copy(y_hbm.at[idx_v.at[k, pl.ds(c * n, n)]], rows_v.at[s, k],
                                         sem_in.at[s])
                   for k in range(TOP_K)]
            cps.append(pltpu.make_async_copy(x1_hbm.at[pl.ds(off, n)], x_v.at[s], sem_in.at[s]))
            return cps

        def store(c, s):
            return pltpu.make_async_copy(o_v.at[s], out_hbm.at[pl.ds(row0 + base + c * n, n)],
                                         sem_out.at[s])

        for cp in loads(0, 0):
            cp.start()

        @pl.loop(0, nch, step=2)
        def _(c):
            for s in range(2):
                cc = c + s

                @pl.when(cc + 1 < nch)
                def _():
                    for cp in loads(cc + 1, 1 - s):
                        cp.start()

                for cp in loads(cc, s):
                    cp.wait()

                @pl.when(cc >= 2)
                def _():
                    store(cc - 2, s).wait()

                @pl.loop(0, n)
                def _(r):
                    tok = jnp.full((SC_LANES,), cc * n + r, jnp.int32)
                    w = [plsc.load_gather(p_v, [jnp.full((SC_LANES,), k, jnp.int32), tok])
                         for k in range(TOP_K)]
                    for g in range(dp // SC_LANES):
                        lo_sl = pl.ds(g * SC_LANES, SC_LANES)
                        hi_sl = pl.ds(dp + g * SC_LANES, SC_LANES)
                        lo = x_v[s, r, lo_sl]
                        hi = x_v[s, r, hi_sl]
                        for k in range(TOP_K):
                            word = rows_v[s, k, r, lo_sl]
                            lo = lo + w[k] * plsc.bitcast(word << 16, F32)
                            hi = hi + w[k] * plsc.bitcast(word & high_half, F32)
                        o_v[s, r, lo_sl] = lo
                        o_v[s, r, hi_sl] = hi

                store(cc, s).start()

        store(nch - 2, 0).wait()
        store(nch - 1, 1).wait()

    combine_kernel(x1, probs, dest, y, out_ref)


def _layer(x, norm1_g, w_in, q_norm_g, k_norm_g, rel_bias, sg_ln_g, sg_ln_b, sg_w, sg_b,
           w_branch_a, w_branch_b, w_out, norm2_g, router_w, router_b,
           w_gate_up, b_gate_up, w_down, b_down):
    bsz, seq, d = x.shape
    t = bsz * seq
    row = lambda v: v.reshape(1, -1).astype(F32)
    scale = 1.0 / math.sqrt(HEAD_DIM)
    sgb = jnp.repeat(sg_b.T.astype(F32), SG_WIDTH // SG_GROUPS, axis=1)

    rw = router_w.T.astype(F32)
    rw_hi = rw.astype(BF16)
    rw_lo = (rw - rw_hi.astype(F32)).astype(BF16)
    mixer_consts = (
        row(norm1_g), w_in.astype(BF16), w_in[:, _C_V:_C_V + ATTN_WIDTH].T.astype(BF16),
        row(jnp.tile(q_norm_g, N_HEADS)) * scale, row(jnp.tile(k_norm_g, N_HEADS)),
        _attention_bias(rel_bias), row(sg_ln_g), row(sg_ln_b),
        sg_w.astype(F32), sgb, w_branch_a.astype(BF16), w_branch_b.astype(BF16),
        w_out.astype(BF16), row(norm2_g), jnp.concatenate([rw_hi, rw_lo], axis=0),
        router_b.reshape(N_EXPERTS, 1).astype(F32))
    expert_params = (w_gate_up.astype(F32), b_gate_up.reshape(N_EXPERTS, 1, -1).astype(F32),
                     w_down.astype(F32), b_down.reshape(N_EXPERTS, 1, -1).astype(F32))

    groups = MOE_GROUPS if bsz % MOE_GROUPS == 0 else 1
    gb = bsz // groups
    tg = gb * seq
    n_tiles = tg * TOP_K // ROW_TILE + N_EXPERTS
    out_ref = jax.new_ref(lax.empty((t, d), F32))

    staged = []
    for g in range(groups):
        x1, h2, top_i, probs = _mixer(x, g * gb, gb, mixer_consts)
        dest, counts = _rank(top_i)
        plan = _plan(counts[:, 0], n_tiles)
        xs = _scatter(h2.reshape(tg, d // 2), dest, n_tiles * ROW_TILE)
        staged.append((x1.reshape(tg, d), probs, dest, plan, xs))
    for g, (x1, probs, dest, plan, xs) in enumerate(staged):
        y = _experts(xs, plan, *expert_params)
        _combine(x1, probs, dest, y, out_ref, g * tg)
    return jax.freeze(out_ref).reshape(bsz, seq, d)


def kernel(x, norm1_g, w_in, q_norm_g, k_norm_g, rel_bias, sg_ln_g, sg_ln_b, sg_w, sg_b,
           w_branch_a, w_branch_b, w_out, norm2_g, router_w, router_b,
           w_gate_up, b_gate_up, w_down, b_down):
    depth = norm1_g.shape[0]
    for l in range(depth):
        x = _layer(x, norm1_g[l], w_in[l], q_norm_g[l], k_norm_g[l], rel_bias[l], sg_ln_g[l],
                   sg_ln_b[l], sg_w[l], sg_b[l], w_branch_a[l], w_branch_b[l], w_out[l],
                   norm2_g[l], router_w[l], router_b[l], w_gate_up[l], b_gate_up[l],
                   w_down[l], b_down[l])
    return x
```

```python
import functools
import math

import numpy as np
import jax
import jax.numpy as jnp
from jax import lax
from jax.experimental import pallas as pl
from jax.experimental.pallas import tpu as pltpu
from jax.experimental.pallas import tpu_sc as plsc

F32 = jnp.float32
BF16 = jnp.bfloat16

D_MODEL = 1024
CHUNK = 64
N_BACK = 8
BAND = (N_BACK + 1) * CHUNK
N_HEADS = 8
HEAD_DIM = 64
ATTN_WIDTH = N_HEADS * HEAD_DIM
REL_CLIP = 256
SG_BLOCK = 128
SG_GROUPS = 4
SG_WIDTH = 512
IN_WIDTH = 3 * ATTN_WIDTH + 2 * SG_WIDTH + 2 * D_MODEL
N_EXPERTS = 32
TOP_K = 4
D_FF = D_MODEL
SWIGLU_LIMIT = 7.0
SWIGLU_ALPHA = 1.702
EPS = 1e-6
NEG = -1e30

LANES = 128
VMEM_BYTES_V7X = 64 * 1024 * 1024

SEQ_TILE = 512
Q_BLOCK = 2 * CHUNK
PREV = N_BACK * CHUNK
WIN = PREV + Q_BLOCK
GATE_CHUNK = 256
RANK_TILE = 1024
ROW_TILE = 512
MOE_GROUPS = 1

SC_CORES = 2
SC_SUBCORES = 16
SC_LANES = 16
SC_WORKERS = SC_CORES * SC_SUBCORES
SC_SCATTER_CHUNK = 64
SC_COMBINE_CHUNK = 8

_C_Q = 0
_C_K = ATTN_WIDTH
_C_V = 2 * ATTN_WIDTH
_C_U = 3 * ATTN_WIDTH
_C_VS = _C_U + SG_WIDTH
_C_GA = _C_VS + SG_WIDTH
_C_GB = _C_GA + D_MODEL


def _dot(a, b):
    return jnp.dot(a, b, preferred_element_type=F32)


def _pack_bf16_pairs(v):
    w = v.shape[1] // 2
    lo = lax.bitcast_convert_type(v[:, :w].astype(BF16).astype(F32), jnp.uint32)
    hi = lax.bitcast_convert_type(v[:, w:].astype(BF16).astype(F32), jnp.uint32)
    return lax.bitcast_convert_type((lo >> 16) | hi, jnp.int32)


def _unpack_bf16_pairs(p):
    u = lax.bitcast_convert_type(p, jnp.uint32)
    lo = lax.bitcast_convert_type(u << 16, F32)
    hi = lax.bitcast_convert_type(u & jnp.uint32(0xFFFF0000), F32)
    return jnp.concatenate([lo, hi], axis=1).astype(BF16)


def _dot_nt(a, b, precision=None):
    return lax.dot_general(a, b, (((1,), (1,)), ((), ())),
                           preferred_element_type=F32, precision=precision)


def _mixer_kernel(x_ref, g1_ref, win_ref, wvt_ref, qg_ref, kg_ref, bias_ref, lng_ref,
                  lnb_ref, sgw_ref, sgb_ref, wa_ref, wb_ref, wo_ref, g2_ref, rwt_ref, rb_ref,
                  x1_ref, h2_ref, ti_ref, pr_ref,
                  h_scr, q_scr, k_win, vt_win, ya_scr, u_scr, vs_scr, ysg_scr, ga_scr, mb_scr):
    ts = x_ref.shape[1]
    s_idx = pl.program_id(1)

    @pl.when(s_idx == 0)
    def _():
        k_win[0:PREV, :] = jnp.zeros((PREV, ATTN_WIDTH), BF16)
        vt_win[:, 0:PREV] = jnp.zeros((ATTN_WIDTH, PREV), BF16)

    @pl.when(s_idx > 0)
    def _():
        k_win[0:PREV, :] = k_win[ts:ts + PREV, :]
        vt_win[:, 0:PREV] = vt_win[:, ts:ts + PREV]

    x = x_ref[0]
    ms = jnp.mean(x * x, axis=-1, keepdims=True)
    h_scr[...] = (x * lax.rsqrt(ms + EPS) * g1_ref[...]).astype(BF16)

    def proj(lo, width):
        return _dot(h_scr[...], win_ref[:, lo:lo + width])

    def head_rms(t, g_ref):
        low = lax.broadcasted_iota(jnp.int32, (1, 2 * HEAD_DIM), 1) < HEAD_DIM
        cols = []
        for p in range(N_HEADS // 2):
            blk = t[:, p * 2 * HEAD_DIM:(p + 1) * 2 * HEAD_DIM]
            sq = blk * blk
            s_lo = jnp.sum(jnp.where(low, sq, 0.0), axis=-1, keepdims=True)
            s_hi = jnp.sum(jnp.where(low, 0.0, sq), axis=-1, keepdims=True)
            r = lax.rsqrt(jnp.where(low, s_lo, s_hi) * (1.0 / HEAD_DIM) + EPS)
            cols.append(blk * r)
        return jnp.concatenate(cols, axis=1) * g_ref[...]

    lane = lax.broadcasted_iota(jnp.int32, (1, ATTN_WIDTH), 1)
    even_head = (lane % (2 * HEAD_DIM)) < HEAD_DIM
    q_raw = proj(_C_Q, ATTN_WIDTH)
    k_raw = proj(_C_K, ATTN_WIDTH)
    vt_win[:, PREV:PREV + ts] = _dot_nt(wvt_ref[...], h_scr[...]).astype(BF16)
    qn = head_rms(q_raw, qg_ref)
    q_even = jnp.where(even_head, qn, 0.0).astype(BF16)
    q_odd = jnp.where(even_head, 0.0, qn).astype(BF16)
    for qb in range(ts // Q_BLOCK):
        rs = slice(qb * Q_BLOCK, (qb + 1) * Q_BLOCK)
        q_scr[qb, 0:Q_BLOCK, :] = q_even[rs]
        q_scr[qb, Q_BLOCK:2 * Q_BLOCK, :] = q_odd[rs]
    k_win[PREV:PREV + ts, :] = head_rms(k_raw, kg_ref).astype(BF16)

    def sg_u():
        u_scr[...] = jax.nn.gelu(proj(_C_U, SG_WIDTH))

    def sg_v():
        vs_scr[...] = jax.nn.gelu(proj(_C_VS, SG_WIDTH))

    gdim = SG_WIDTH // SG_GROUPS

    def sg_group(g):
        cs = slice(g * gdim, (g + 1) * gdim)
        vg = vs_scr[:, cs]
        mu = jnp.mean(vg, axis=-1, keepdims=True)
        xc = vg - mu
        var = jnp.mean(xc * xc, axis=-1, keepdims=True)
        vn = (xc * lax.rsqrt(var + EPS) * lng_ref[:, cs] + lnb_ref[:, cs]).astype(BF16)
        r_i = lax.broadcasted_iota(jnp.int32, (SG_BLOCK, SG_BLOCK), 0)
        c_i = lax.broadcasted_iota(jnp.int32, (SG_BLOCK, SG_BLOCK), 1)
        wm = jnp.where(r_i >= c_i, sgw_ref[g], 0.0).astype(BF16)
        for j in range(ts // SG_BLOCK):
            rs = slice(j * SG_BLOCK, (j + 1) * SG_BLOCK)
            mixed = _dot(wm, vn[rs]) + sgb_ref[:, cs]
            ysg_scr[rs, cs] = (u_scr[rs, cs] * mixed).astype(BF16)

    def gate_a_chunk(c):
        cols = slice(c * GATE_CHUNK, (c + 1) * GATE_CHUNK)
        ga_scr[:, cols] = jax.nn.sigmoid(proj(_C_GA + c * GATE_CHUNK, GATE_CHUNK))

    def gated_b_chunk(c):
        cols = slice(c * GATE_CHUNK, (c + 1) * GATE_CHUNK)
        gate = jax.nn.sigmoid(proj(_C_GB + c * GATE_CHUNK, GATE_CHUNK))
        mb_scr[:, cols] = gate * _dot(ysg_scr[...], wb_ref[:, cols])

    n_chunks = D_MODEL // GATE_CHUNK
    side_work = [sg_u, sg_v] + [functools.partial(sg_group, g) for g in range(SG_GROUPS)]
    side_work += [functools.partial(gate_a_chunk, c) for c in range(n_chunks)]
    side_work += [functools.partial(gated_b_chunk, c) for c in range(n_chunks)]

    n_blocks = (ts // Q_BLOCK) * (N_HEADS // 2)
    schedule = [[] for _ in range(n_blocks)]
    for i, piece in enumerate(side_work):
        schedule[-(-(i + 1) * n_blocks // len(side_work)) - 1].append(piece)

    key_row = lax.broadcasted_iota(jnp.int32, (WIN, 1), 0)
    pairs = N_HEADS // 2

    def scores(block):
        qb, pair = divmod(block, pairs)
        r0 = qb * Q_BLOCK
        cs = slice(pair * 2 * HEAD_DIM, (pair + 1) * 2 * HEAD_DIM)
        st = _dot_nt(k_win[r0:r0 + WIN, cs], q_scr[qb, :, cs]) + bias_ref[pair]
        return jnp.where(key_row >= (PREV - (s_idx * ts + r0)), st, NEG)

    def weighted_values(block, p, inv):
        qb, pair = divmod(block, pairs)
        r0 = qb * Q_BLOCK
        cs = slice(pair * 2 * HEAD_DIM, (pair + 1) * 2 * HEAD_DIM)
        ot = _dot(vt_win[cs, r0:r0 + WIN], p)
        o = jnp.concatenate(
            [ot[0:HEAD_DIM, 0:Q_BLOCK] * inv[:, 0:Q_BLOCK],
             ot[HEAD_DIM:2 * HEAD_DIM, Q_BLOCK:2 * Q_BLOCK] * inv[:, Q_BLOCK:2 * Q_BLOCK]],
            axis=0)
        ya_scr[r0:r0 + Q_BLOCK, cs] = o.T.astype(BF16)

    st_next = scores(0)
    pending = None
    for block in range(n_blocks):
        st = st_next
        if block + 1 < n_blocks:
            st_next = scores(block + 1)
        if pending is not None:
            weighted_values(*pending)
        m = jnp.max(st, axis=0, keepdims=True)
        p = jnp.exp(st - m)
        inv = 1.0 / jnp.sum(p, axis=0, keepdims=True)
        pending = (block, p.astype(BF16), inv)
        for piece in schedule[block]:
            piece()
    weighted_values(*pending)

    merged = ga_scr[...] * _dot(ya_scr[...], wa_ref[...]) + mb_scr[...]
    x1 = x_ref[0] + _dot(merged.astype(BF16), wo_ref[...])
    x1_ref[0] = x1

    ms2 = jnp.mean(x1 * x1, axis=-1, keepdims=True)
    h2 = x1 * lax.rsqrt(ms2 + EPS) * g2_ref[...]
    h2_ref[0] = _pack_bf16_pairs(h2)
    h2_hi = h2.astype(BF16)
    h2_lo = (h2 - h2_hi.astype(F32)).astype(BF16)
    by_hi = _dot_nt(rwt_ref[...], h2_hi)
    lt = (by_hi[0:N_EXPERTS] + by_hi[N_EXPERTS:2 * N_EXPERTS]
          + _dot_nt(rwt_ref[0:N_EXPERTS, :], h2_lo) + rb_ref[...])
    e_iota = lax.broadcasted_iota(jnp.int32, lt.shape, 0)
    vals = []
    for k in range(TOP_K):
        m = jnp.max(lt, axis=0, keepdims=True)
        idx = jnp.min(jnp.where(lt == m, e_iota, N_EXPERTS), axis=0, keepdims=True)
        vals.append(m)
        ti_ref[k:k + 1, :] = idx
        lt = jnp.where(e_iota == idx, -jnp.inf, lt)
    exps = [jnp.exp(v - vals[0]) for v in vals]
    denom = exps[0] + exps[1] + exps[2] + exps[3]
    for k in range(TOP_K):
        pr_ref[k:k + 1, :] = exps[k] / denom


def _const_spec(shape):
    zeros = (0,) * len(shape)
    return pl.BlockSpec(shape, lambda b, s: zeros, pipeline_mode=pl.Buffered(1))


def _mixer(x, batch0, bsz, consts):
    _, seq, d = x.shape
    ts = SEQ_TILE
    ns = seq // ts
    t = bsz * seq
    in_spec = pl.BlockSpec((1, ts, d), lambda b, s: (b + batch0, s, 0))
    tok_spec = pl.BlockSpec((1, ts, d), lambda b, s: (b, s, 0))
    packed_spec = pl.BlockSpec((1, ts, d // 2), lambda b, s: (b, s, 0))
    idx_spec = pl.BlockSpec((TOP_K, ts), lambda b, s: (0, b * ns + s))
    return pl.pallas_call(
        _mixer_kernel,
        grid=(bsz, ns),
        in_specs=[in_spec] + [_const_spec(c.shape) for c in consts],
        out_specs=[tok_spec, packed_spec, idx_spec, idx_spec],
        out_shape=[
            jax.ShapeDtypeStruct((bsz, seq, d), F32),
            jax.ShapeDtypeStruct((bsz, seq, d // 2), jnp.int32),
            jax.ShapeDtypeStruct((TOP_K, t), jnp.int32),
            jax.ShapeDtypeStruct((TOP_K, t), F32),
        ],
        scratch_shapes=[
            pltpu.VMEM((ts, d), BF16),
            pltpu.VMEM((ts // Q_BLOCK, 2 * Q_BLOCK, ATTN_WIDTH), BF16),
            pltpu.VMEM((PREV + ts, ATTN_WIDTH), BF16),
            pltpu.VMEM((ATTN_WIDTH, PREV + ts), BF16),
            pltpu.VMEM((ts, ATTN_WIDTH), BF16),
            pltpu.VMEM((ts, SG_WIDTH), F32),
            pltpu.VMEM((ts, SG_WIDTH), F32),
            pltpu.VMEM((ts, SG_WIDTH), BF16),
            pltpu.VMEM((ts, d), F32),
            pltpu.VMEM((ts, d), F32),
        ],
        compiler_params=pltpu.CompilerParams(
            dimension_semantics=("arbitrary", "arbitrary"),
            vmem_limit_bytes=VMEM_BYTES_V7X * 7 // 8),
        name="mixer",
    )(x, *consts)


def _attention_bias(rel_bias):
    i = np.arange(Q_BLOCK)[:, None]
    j = np.arange(WIN)[None, :]
    kk = j - (i // CHUNK) * CHUNK
    inside = (kk >= 0) & (kk < BAND)
    tab = rel_bias.astype(F32)
    far = PREV + Q_BLOCK - 1 - REL_CLIP + 1
    near = Q_BLOCK - 1 + REL_CLIP
    ext = jnp.concatenate(
        [jnp.broadcast_to(tab[:, 2 * REL_CLIP:], (N_HEADS, far)),
         tab[:, 2 * REL_CLIP - near:2 * REL_CLIP][:, ::-1]], axis=1)
    rows = [ext[:, Q_BLOCK - 1 - r:Q_BLOCK - 1 - r + WIN] for r in range(Q_BLOCK)]
    bias = jnp.where(inside[None], jnp.stack(rows, axis=1), NEG)
    return (bias.reshape(N_HEADS // 2, 2 * Q_BLOCK, WIN).transpose(0, 2, 1))


def _rank_kernel(ti_ref, upper_ref, dest_ref, cnt_ref, total_scr, base_scr):
    phase = pl.program_id(0)
    i = pl.program_id(1)
    tt = ti_ref.shape[1]

    @pl.when((phase == 0) & (i == 0))
    def _():
        total_scr[...] = jnp.zeros_like(total_scr)

    ti = ti_ref[...]
    e_iota = lax.broadcasted_iota(jnp.int32, (N_EXPERTS, tt), 0)
    hits = [e_iota == ti[k:k + 1, :] for k in range(TOP_K)]
    cnt = hits[0].astype(F32)
    for k in range(1, TOP_K):
        cnt = cnt + hits[k].astype(F32)
    tile_total = jnp.sum(cnt, axis=1, keepdims=True)

    @pl.when(phase == 0)
    def _():
        total_scr[...] = total_scr[...] + tile_total

    @pl.when((phase == 1) & (i == 0))
    def _():
        cnt_ref[...] = total_scr[...].astype(jnp.int32)
        padded = jnp.floor((total_scr[...] + (ROW_TILE - 1)) * (1.0 / ROW_TILE)) * ROW_TILE
        run = jnp.zeros((1, LANES), F32)
        for e in range(N_EXPERTS):
            base_scr[e:e + 1, :] = run
            run = run + padded[e:e + 1, :]

    @pl.when(phase == 1)
    def _():
        before = _dot(cnt.astype(BF16), upper_ref[...]) + base_scr[:, 0:1]
        for k in range(TOP_K):
            dest_ref[k:k + 1, :] = jnp.sum(jnp.where(hits[k], before, 0.0), axis=0,
                                           keepdims=True).astype(jnp.int32)
        base_scr[...] = base_scr[...] + tile_total


def _rank(top_i):
    t = top_i.shape[1]
    tt = RANK_TILE
    upper = jnp.asarray(np.triu(np.ones((tt, tt), np.float32), 1), BF16)
    return pl.pallas_call(
        _rank_kernel,
        grid=(2, t // tt),
        in_specs=[pl.BlockSpec((TOP_K, tt), lambda p, i: (0, i)),
                  pl.BlockSpec((tt, tt), lambda p, i: (0, 0), pipeline_mode=pl.Buffered(1))],
        out_specs=[pl.BlockSpec((TOP_K, tt), lambda p, i: (0, i * p)),
                   pl.BlockSpec((N_EXPERTS, LANES), lambda p, i: (0, 0))],
        out_shape=[jax.ShapeDtypeStruct((TOP_K, t), jnp.int32),
                   jax.ShapeDtypeStruct((N_EXPERTS, LANES), jnp.int32)],
        scratch_shapes=[pltpu.VMEM((N_EXPERTS, LANES), F32),
                        pltpu.VMEM((N_EXPERTS, LANES), F32)],
        compiler_params=pltpu.CompilerParams(dimension_semantics=("arbitrary", "arbitrary")),
        name="rank",
    )(top_i, upper)


def _plan_kernel(cnt_ref, tile_ref, exp_ref, rows_ref, first_ref):
    n_tiles = exp_ref.shape[0]

    def per_expert(e, w):
        c = cnt_ref[e]
        n = (c + ROW_TILE - 1) // ROW_TILE

        def per_tile(j, _):
            tile_ref[w + j] = w + j
            exp_ref[w + j] = e
            rows_ref[w + j] = jnp.minimum(c - j * ROW_TILE, ROW_TILE)
            first_ref[w + j] = (j == 0).astype(jnp.int32)
            return 0

        lax.fori_loop(0, n, per_tile, 0)
        return w + n

    n_used = lax.fori_loop(0, N_EXPERTS, per_expert, jnp.int32(0))

    def pad(w, _):
        tile_ref[w] = n_used - 1
        exp_ref[w] = exp_ref[n_used - 1]
        rows_ref[w] = 0
        first_ref[w] = 0
        return 0

    lax.fori_loop(n_used, n_tiles, pad, 0)


def _plan(counts, n_tiles):
    smem = pl.BlockSpec(memory_space=pltpu.SMEM)
    vec = jax.ShapeDtypeStruct((n_tiles,), jnp.int32)
    return pl.pallas_call(
        _plan_kernel,
        in_specs=[smem],
        out_specs=[smem, smem, smem, smem],
        out_shape=[vec, vec, vec, vec],
        name="plan",
    )(counts)


def _scatter(h2, dest, n_slots):
    t, d = h2.shape
    per_worker = t // SC_WORKERS
    n = SC_SCATTER_CHUNK
    nch = per_worker // n
    assert nch % 2 == 0
    mesh = plsc.VectorSubcoreMesh(core_axis_name="c", subcore_axis_name="s")

    @functools.partial(
        pl.kernel, mesh=mesh,
        out_type=jax.ShapeDtypeStruct((n_slots, d), h2.dtype),
        scratch_types=[pltpu.VMEM((TOP_K, nch, n), jnp.int32), pltpu.VMEM((2, n, d), h2.dtype),
                       pltpu.SemaphoreType.DMA((2,)), pltpu.SemaphoreType.DMA((2,))],
        name="sc_scatter")
    def scatter_kernel(h_hbm, dest_hbm, xs_hbm, idx_v, rows_v, sem_in, sem_out):
        wid = lax.axis_index("s") * SC_CORES + lax.axis_index("c")
        base = wid * per_worker
        for k in range(TOP_K):
            pltpu.sync_copy(dest_hbm.at[k, pl.ds(wid * nch, nch)], idx_v.at[k])

        def load(c, s):
            return pltpu.make_async_copy(h_hbm.at[pl.ds(base + c * n, n)], rows_v.at[s],
                                         sem_in.at[s])

        def store(c, s, k):
            return pltpu.make_async_copy(rows_v.at[s], xs_hbm.at[idx_v.at[k, c]], sem_out.at[s])

        load(0, 0).start()

        @pl.loop(0, nch, step=2)
        def _(c):
            for s in range(2):
                cc = c + s

                @pl.when(cc >= 1)
                def _():
                    for k in range(TOP_K):
                        store(cc - 1, 1 - s, k).wait()

                @pl.when(cc + 1 < nch)
                def _():
                    load(cc + 1, 1 - s).start()

                load(cc, s).wait()
                for k in range(TOP_K):
                    store(cc, s, k).start()

        for k in range(TOP_K):
            store(nch - 1, (nch - 1) % 2, k).wait()

    return scatter_kernel(h2, dest.reshape(TOP_K, t // n, n))


def _expert_kernel(tile_ref, exp_ref, rows_ref, first_ref,
                   xs_ref, wgu_ref, bgu_ref, wd_ref, bd_ref, y_ref, wgu_bf, wd_bf):
    w = pl.program_id(0)

    @pl.when(first_ref[w] == 1)
    def _():
        wgu_bf[...] = wgu_ref[0].astype(BF16)
        wd_bf[...] = wd_ref[0].astype(BF16)

    n_rows = rows_ref[w]

    @pl.when(n_rows > 0)
    def _():
        row = lax.broadcasted_iota(jnp.int32, xs_ref.shape, 0)
        x = _unpack_bf16_pairs(jnp.where(row < n_rows, xs_ref[...], 0))
        gu = _dot(x, wgu_bf[...]) + bgu_ref[0]
        gate = jnp.minimum(gu[:, :D_FF], SWIGLU_LIMIT)
        up = jnp.clip(gu[:, D_FF:], -SWIGLU_LIMIT, SWIGLU_LIMIT)
        act = (up + 1.0) * (gate * jax.nn.sigmoid(gate * SWIGLU_ALPHA))
        y_ref[...] = _pack_bf16_pairs(_dot(act.astype(BF16), wd_bf[...]) + bd_ref[0])


def _experts(xs, plan, wgu, bgu, wd, bd):
    n_rows, dp = xs.shape
    d = 2 * dp
    w_tile, w_exp, w_rows, w_first = plan
    n_items = w_tile.shape[0]
    tm = ROW_TILE
    by_tile = lambda w, tl, ex, nr, fi: (tl[w], 0)
    by_expert = lambda w, tl, ex, nr, fi: (ex[w], 0, 0)
    grid_spec = pltpu.PrefetchScalarGridSpec(
        num_scalar_prefetch=4,
        grid=(n_items,),
        in_specs=[
            pl.BlockSpec((tm, dp), by_tile),
            pl.BlockSpec((1, d, 2 * D_FF), by_expert),
            pl.BlockSpec((1, 1, 2 * D_FF), by_expert),
            pl.BlockSpec((1, D_FF, d), by_expert),
            pl.BlockSpec((1, 1, d), by_expert),
        ],
        out_specs=pl.BlockSpec((tm, dp), by_tile),
        scratch_shapes=[pltpu.VMEM((d, 2 * D_FF), BF16), pltpu.VMEM((D_FF, d), BF16)],
    )
    return pl.pallas_call(
        _expert_kernel,
        grid_spec=grid_spec,
        out_shape=jax.ShapeDtypeStruct((n_rows, dp), jnp.int32),
        compiler_params=pltpu.CompilerParams(
            dimension_semantics=("arbitrary",),
            vmem_limit_bytes=VMEM_BYTES_V7X * 7 // 8),
        name="experts",
    )(w_tile, w_exp, w_rows, w_first, xs, wgu, bgu, wd, bd)


def _combine(x1, probs, dest, y, out_ref, row0):
    t, d = x1.shape
    per_worker = t // SC_WORKERS
    dp = d // 2
    n = SC_COMBINE_CHUNK
    nch = per_worker // n
    assert nch % 2 == 0
    mesh = plsc.VectorSubcoreMesh(core_axis_name="c", subcore_axis_name="s")
    high_half = -65536

    @functools.partial(
        pl.kernel, mesh=mesh,
        out_type=(),
        scratch_types=[pltpu.VMEM((TOP_K, per_worker), jnp.int32),
                       pltpu.VMEM((2, TOP_K, n, dp), jnp.int32),
                       pltpu.VMEM((2, n, d), F32), pltpu.VMEM((2, n, d), F32),
                       pltpu.VMEM((TOP_K, per_worker), F32),
                       pltpu.SemaphoreType.DMA((2,)), pltpu.SemaphoreType.DMA((2,))],
        compiler_params=pltpu.CompilerParams(needs_layout_passes=False),
        name="sc_combine")
    def combine_kernel(x1_hbm, p_hbm, dest_hbm, y_hbm, out_hbm,
                       idx_v, rows_v, x_v, o_v, p_v, sem_in, sem_out):
        wid = lax.axis_index("s") * SC_CORES + lax.axis_index("c")
        base = wid * per_worker
        for k in range(TOP_K):
            pltpu.sync_copy(dest_hbm.at[k, pl.ds(base, per_worker)], idx_v.at[k])
            pltpu.sync_copy(p_hbm.at[k, pl.ds(base, per_worker)], p_v.at[k])

        def loads(c, s):
            off = base + c * n
            cps = [pltpu.make_async_copy(y_hbm.at[idx_v.at[k, pl.ds(c * n, n)]], rows_v.at[s, k],
                                         sem_in.at[s])
                   for k in range(TOP_K)]
            cps.append(pltpu.make_async_copy(x1_hbm.at[pl.ds(off, n)], x_v.at[s], sem_in.at[s]))
            return cps

        def store(c, s):
            return pltpu.make_async_copy(o_v.at[s], out_hbm.at[pl.ds(row0 + base + c * n, n)],
                                         sem_out.at[s])

        for cp in loads(0, 0):
            cp.start()

        @pl.loop(0, nch, step=2)
        def _(c):
            for s in range(2):
                cc = c + s

                @pl.when(cc + 1 < nch)
                def _():
                    for cp in loads(cc + 1, 1 - s):
                        cp.start()

                for cp in loads(cc, s):
                    cp.wait()

                @pl.when(cc >= 2)
                def _():
                    store(cc - 2, s).wait()

                @pl.loop(0, n)
                def _(r):
                    tok = jnp.full((SC_LANES,), cc * n + r, jnp.int32)
                    w = [plsc.load_gather(p_v, [jnp.full((SC_LANES,), k, jnp.int32), tok])
                         for k in range(TOP_K)]
                    for g in range(dp // SC_LANES):
                        lo_sl = pl.ds(g * SC_LANES, SC_LANES)
                        hi_sl = pl.ds(dp + g * SC_LANES, SC_LANES)
                        lo = x_v[s, r, lo_sl]
                        hi = x_v[s, r, hi_sl]
                        for k in range(TOP_K):
                            word = rows_v[s, k, r, lo_sl]
                            lo = lo + w[k] * plsc.bitcast(word << 16, F32)
                            hi = hi + w[k] * plsc.bitcast(word & high_half, F32)
                        o_v[s, r, lo_sl] = lo
                        o_v[s, r, hi_sl] = hi

                store(cc, s).start()

        store(nch - 2, 0).wait()
        store(nch - 1, 1).wait()

    combine_kernel(x1, probs, dest, y, out_ref)


def _layer(x, norm1_g, w_in, q_norm_g, k_norm_g, rel_bias, sg_ln_g, sg_ln_b, sg_w, sg_b,
           w_branch_a, w_branch_b, w_out, norm2_g, router_w, router_b,
           w_gate_up, b_gate_up, w_down, b_down):
    bsz, seq, d = x.shape
    t = bsz * seq
    row = lambda v: v.reshape(1, -1).astype(F32)
    scale = 1.0 / math.sqrt(HEAD_DIM)
    sgb = jnp.repeat(sg_b.T.astype(F32), SG_WIDTH // SG_GROUPS, axis=1)

    rw = router_w.T.astype(F32)
    rw_hi = rw.astype(BF16)
    rw_lo = (rw - rw_hi.astype(F32)).astype(BF16)
    mixer_consts = (
        row(norm1_g), w_in.astype(BF16), w_in[:, _C_V:_C_V + ATTN_WIDTH].T.astype(BF16),
        row(jnp.tile(q_norm_g, N_HEADS)) * scale, row(jnp.tile(k_norm_g, N_HEADS)),
        _attention_bias(rel_bias), row(sg_ln_g), row(sg_ln_b),
        sg_w.astype(F32), sgb, w_branch_a.astype(BF16), w_branch_b.astype(BF16),
        w_out.astype(BF16), row(norm2_g), jnp.concatenate([rw_hi, rw_lo], axis=0),
        router_b.reshape(N_EXPERTS, 1).astype(F32))
    expert_params = (w_gate_up.astype(F32), b_gate_up.reshape(N_EXPERTS, 1, -1).astype(F32),
                     w_down.astype(F32), b_down.reshape(N_EXPERTS, 1, -1).astype(F32))

    groups = MOE_GROUPS if bsz % MOE_GROUPS == 0 else 1
    gb = bsz // groups
    tg = gb * seq
    n_tiles = tg * TOP_K // ROW_TILE + N_EXPERTS
    out_ref = jax.new_ref(lax.empty((t, d), F32))

    staged = []
    for g in range(groups):
        x1, h2, top_i, probs = _mixer(x, g * gb, gb, mixer_consts)
        dest, counts = _rank(top_i)
        plan = _plan(counts[:, 0], n_tiles)
        xs = _scatter(h2.reshape(tg, d // 2), dest, n_tiles * ROW_TILE)
        staged.append((x1.reshape(tg, d), probs, dest, plan, xs))
    for g, (x1, probs, dest, plan, xs) in enumerate(staged):
        y = _experts(xs, plan, *expert_params)
        _combine(x1, probs, dest, y, out_ref, g * tg)
    return jax.freeze(out_ref).reshape(bsz, seq, d)


def kernel(x, norm1_g, w_in, q_norm_g, k_norm_g, rel_bias, sg_ln_g, sg_ln_b, sg_w, sg_b,
           w_branch_a, w_branch_b, w_out, norm2_g, router_w, router_b,
           w_gate_up, b_gate_up, w_down, b_down):
    depth = norm1_g.shape[0]
    for l in range(depth):
        x = _layer(x, norm1_g[l], w_in[l], q_norm_g[l], k_norm_g[l], rel_bias[l], sg_ln_g[l],
                   sg_ln_b[l], sg_w[l], sg_b[l], w_branch_a[l], w_branch_b[l], w_out[l],
                   norm2_g[l], router_w[l], router_b[l], w_gate_up[l], b_gate_up[l],
                   w_down[l], b_down[l])
    return x
```

```python
import functools
import math

import numpy as np
import jax
import jax.numpy as jnp
from jax import lax
from jax.experimental import pallas as pl
from jax.experimental.pallas import tpu as pltpu
from jax.experimental.pallas import tpu_sc as plsc

F32 = jnp.float32
BF16 = jnp.bfloat16

D_MODEL = 1024
CHUNK = 64
N_BACK = 8
BAND = (N_BACK + 1) * CHUNK
N_HEADS = 8
HEAD_DIM = 64
ATTN_WIDTH = N_HEADS * HEAD_DIM
REL_CLIP = 256
SG_BLOCK = 128
SG_GROUPS = 4
SG_WIDTH = 512
IN_WIDTH = 3 * ATTN_WIDTH + 2 * SG_WIDTH + 2 * D_MODEL
N_EXPERTS = 32
TOP_K = 4
D_FF = D_MODEL
SWIGLU_LIMIT = 7.0
SWIGLU_ALPHA = 1.702
EPS = 1e-6
NEG = -1e30

LANES = 128
VMEM_BYTES_V7X = 64 * 1024 * 1024

SEQ_TILE = 512
Q_BLOCK = 2 * CHUNK
PREV = N_BACK * CHUNK
WIN = PREV + Q_BLOCK
GATE_CHUNK = 256
RANK_TILE = 1024
ROW_TILE = 512
MOE_GROUPS = 1

SC_CORES = 2
SC_SUBCORES = 16
SC_LANES = 16
SC_WORKERS = SC_CORES * SC_SUBCORES
SC_SCATTER_CHUNK = 64
SC_COMBINE_CHUNK = 8

_C_Q = 0
_C_K = ATTN_WIDTH
_C_V = 2 * ATTN_WIDTH
_C_U = 3 * ATTN_WIDTH
_C_VS = _C_U + SG_WIDTH
_C_GA = _C_VS + SG_WIDTH
_C_GB = _C_GA + D_MODEL


def _dot(a, b):
    return jnp.dot(a, b, preferred_element_type=F32)


def _pack_bf16_pairs(v):
    w = v.shape[1] // 2
    lo = lax.bitcast_convert_type(v[:, :w].astype(BF16).astype(F32), jnp.uint32)
    hi = lax.bitcast_convert_type(v[:, w:].astype(BF16).astype(F32), jnp.uint32)
    return lax.bitcast_convert_type((lo >> 16) | hi, jnp.int32)


def _unpack_bf16_pairs(p):
    u = lax.bitcast_convert_type(p, jnp.uint32)
    lo = lax.bitcast_convert_type(u << 16, F32)
    hi = lax.bitcast_convert_type(u & jnp.uint32(0xFFFF0000), F32)
    return jnp.concatenate([lo, hi], axis=1).astype(BF16)


def _dot_nt(a, b, precision=None):
    return lax.dot_general(a, b, (((1,), (1,)), ((), ())),
                           preferred_element_type=F32, precision=precision)


def _mixer_kernel(x_ref, g1_ref, win_ref, wvt_ref, qg_ref, kg_ref, bias_ref, lng_ref,
                  lnb_ref, sgw_ref, sgb_ref, wa_ref, wb_ref, wo_ref, g2_ref, rwt_ref, rb_ref,
                  x1_ref, h2_ref, ti_ref, pr_ref,
                  h_scr, q_scr, k_win, vt_win, ya_scr, u_scr, vs_scr, ysg_scr, ga_scr, mb_scr):
    ts = x_ref.shape[1]
    s_idx = pl.program_id(1)

    @pl.when(s_idx == 0)
    def _():
        k_win[0:PREV, :] = jnp.zeros((PREV, ATTN_WIDTH), BF16)
        vt_win[:, 0:PREV] = jnp.zeros((ATTN_WIDTH, PREV), BF16)

    @pl.when(s_idx > 0)
    def _():
        k_win[0:PREV, :] = k_win[ts:ts + PREV, :]
        vt_win[:, 0:PREV] = vt_win[:, ts:ts + PREV]

    x = x_ref[0]
    ms = jnp.mean(x * x, axis=-1, keepdims=True)
    h_scr[...] = (x * lax.rsqrt(ms + EPS) * g1_ref[...]).astype(BF16)

    def proj(lo, width):
        return _dot(h_scr[...], win_ref[:, lo:lo + width])

    def head_rms(t, g_ref):
        low = lax.broadcasted_iota(jnp.int32, (1, 2 * HEAD_DIM), 1) < HEAD_DIM
        cols = []
        for p in range(N_HEADS // 2):
            blk = t[:, p * 2 * HEAD_DIM:(p + 1) * 2 * HEAD_DIM]
            sq = blk * blk
            s_lo = jnp.sum(jnp.where(low, sq, 0.0), axis=-1, keepdims=True)
            s_hi = jnp.sum(jnp.where(low, 0.0, sq), axis=-1, keepdims=True)
            r = lax.rsqrt(jnp.where(low, s_lo, s_hi) * (1.0 / HEAD_DIM) + EPS)
            cols.append(blk * r)
        return jnp.concatenate(cols, axis=1) * g_ref[...]

    lane = lax.broadcasted_iota(jnp.int32, (1, ATTN_WIDTH), 1)
    even_head = (lane % (2 * HEAD_DIM)) < HEAD_DIM
    q_raw = proj(_C_Q, ATTN_WIDTH)
    k_raw = proj(_C_K, ATTN_WIDTH)
    vt_win[:, PREV:PREV + ts] = _dot_nt(wvt_ref[...], h_scr[...]).astype(BF16)
    qn = head_rms(q_raw, qg_ref)
    q_even = jnp.where(even_head, qn, 0.0).astype(BF16)
    q_odd = jnp.where(even_head, 0.0, qn).astype(BF16)
    for qb in range(ts // Q_BLOCK):
        rs = slice(qb * Q_BLOCK, (qb + 1) * Q_BLOCK)
        q_scr[qb, 0:Q_BLOCK, :] = q_even[rs]
        q_scr[qb, Q_BLOCK:2 * Q_BLOCK, :] = q_odd[rs]
    k_win[PREV:PREV + ts, :] = head_rms(k_raw, kg_ref).astype(BF16)

    def sg_u():
        u_scr[...] = jax.nn.gelu(proj(_C_U, SG_WIDTH))

    def sg_v():
        vs_scr[...] = jax.nn.gelu(proj(_C_VS, SG_WIDTH))

    gdim = SG_WIDTH // SG_GROUPS

    def sg_group(g):
        cs = slice(g * gdim, (g + 1) * gdim)
        vg = vs_scr[:, cs]
        mu = jnp.mean(vg, axis=-1, keepdims=True)
        xc = vg - mu
        var = jnp.mean(xc * xc, axis=-1, keepdims=True)
        vn = (xc * lax.rsqrt(var + EPS) * lng_ref[:, cs] + lnb_ref[:, cs]).astype(BF16)
        r_i = lax.broadcasted_iota(jnp.int32, (SG_BLOCK, SG_BLOCK), 0)
        c_i = lax.broadcasted_iota(jnp.int32, (SG_BLOCK, SG_BLOCK), 1)
        wm = jnp.where(r_i >= c_i, sgw_ref[g], 0.0).astype(BF16)
        for j in range(ts // SG_BLOCK):
            rs = slice(j * SG_BLOCK, (j + 1) * SG_BLOCK)
            mixed = _dot(wm, vn[rs]) + sgb_ref[:, cs]
            ysg_scr[rs, cs] = (u_scr[rs, cs] * mixed).astype(BF16)

    def gate_a_chunk(c):
        cols = slice(c * GATE_CHUNK, (c + 1) * GATE_CHUNK)
        ga_scr[:, cols] = jax.nn.sigmoid(proj(_C_GA + c * GATE_CHUNK, GATE_CHUNK))

    def gated_b_chunk(c):
        cols = slice(c * GATE_CHUNK, (c + 1) * GATE_CHUNK)
        gate = jax.nn.sigmoid(proj(_C_GB + c * GATE_CHUNK, GATE_CHUNK))
        mb_scr[:, cols] = gate * _dot(ysg_scr[...], wb_ref[:, cols])

    n_chunks = D_MODEL // GATE_CHUNK
    side_work = [sg_u, sg_v] + [functools.partial(sg_group, g) for g in range(SG_GROUPS)]
    side_work += [functools.partial(gate_a_chunk, c) for c in range(n_chunks)]
    side_work += [functools.partial(gated_b_chunk, c) for c in range(n_chunks)]

    n_blocks = (ts // Q_BLOCK) * (N_HEADS // 2)
    schedule = [[] for _ in range(n_blocks)]
    for i, piece in enumerate(side_work):
        schedule[-(-(i + 1) * n_blocks // len(side_work)) - 1].append(piece)

    key_row = lax.broadcasted_iota(jnp.int32, (WIN, 1), 0)
    pairs = N_HEADS // 2

    def scores(block):
        qb, pair = divmod(block, pairs)
        r0 = qb * Q_BLOCK
        cs = slice(pair * 2 * HEAD_DIM, (pair + 1) * 2 * HEAD_DIM)
        st = _dot_nt(k_win[r0:r0 + WIN, cs], q_scr[qb, :, cs]) + bias_ref[pair]
        return jnp.where(key_row >= (PREV - (s_idx * ts + r0)), st, NEG)

    def weighted_values(block, p, inv):
        qb, pair = divmod(block, pairs)
        r0 = qb * Q_BLOCK
        cs = slice(pair * 2 * HEAD_DIM, (pair + 1) * 2 * HEAD_DIM)
        ot = _dot(vt_win[cs, r0:r0 + WIN], p)
        o = jnp.concatenate(
            [ot[0:HEAD_DIM, 0:Q_BLOCK] * inv[:, 0:Q_BLOCK],
             ot[HEAD_DIM:2 * HEAD_DIM, Q_BLOCK:2 * Q_BLOCK] * inv[:, Q_BLOCK:2 * Q_BLOCK]],
            axis=0)
        ya_scr[r0:r0 + Q_BLOCK, cs] = o.T.astype(BF16)

    st_next = scores(0)
    pending = None
    for block in range(n_blocks):
        st = st_next
        if block + 1 < n_blocks:
            st_next = scores(block + 1)
        if pending is not None:
            weighted_values(*pending)
        m = jnp.max(st, axis=0, keepdims=True)
        p = jnp.exp(st - m)
        inv = 1.0 / jnp.sum(p, axis=0, keepdims=True)
        pending = (block, p.astype(BF16), inv)
        for piece in schedule[block]:
            piece()
    weighted_values(*pending)

    merged = ga_scr[...] * _dot(ya_scr[...], wa_ref[...]) + mb_scr[...]
    x1 = x_ref[0] + _dot(merged.astype(BF16), wo_ref[...])
    x1_ref[0] = x1

    ms2 = jnp.mean(x1 * x1, axis=-1, keepdims=True)
    h2 = x1 * lax.rsqrt(ms2 + EPS) * g2_ref[...]
    h2_ref[0] = _pack_bf16_pairs(h2)
    h2_hi = h2.astype(BF16)
    h2_lo = (h2 - h2_hi.astype(F32)).astype(BF16)
    by_hi = _dot_nt(rwt_ref[...], h2_hi)
    lt = (by_hi[0:N_EXPERTS] + by_hi[N_EXPERTS:2 * N_EXPERTS]
          + _dot_nt(rwt_ref[0:N_EXPERTS, :], h2_lo) + rb_ref[...])
    e_iota = lax.broadcasted_iota(jnp.int32, lt.shape, 0)
    vals = []
    for k in range(TOP_K):
        m = jnp.max(lt, axis=0, keepdims=True)
        idx = jnp.min(jnp.where(lt == m, e_iota, N_EXPERTS), axis=0, keepdims=True)
        vals.append(m)
        ti_ref[k:k + 1, :] = idx
        lt = jnp.where(e_iota == idx, -jnp.inf, lt)
    exps = [jnp.exp(v - vals[0]) for v in vals]
    denom = exps[0] + exps[1] + exps[2] + exps[3]
    for k in range(TOP_K):
        pr_ref[k:k + 1, :] = exps[k] / denom


def _const_spec(shape):
    zeros = (0,) * len(shape)
    return pl.BlockSpec(shape, lambda b, s: zeros, pipeline_mode=pl.Buffered(1))


def _mixer(x, batch0, bsz, consts):
    _, seq, d = x.shape
    ts = SEQ_TILE
    ns = seq // ts
    t = bsz * seq
    in_spec = pl.BlockSpec((1, ts, d), lambda b, s: (b + batch0, s, 0))
    tok_spec = pl.BlockSpec((1, ts, d), lambda b, s: (b, s, 0))
    packed_spec = pl.BlockSpec((1, ts, d // 2), lambda b, s: (b, s, 0))
    idx_spec = pl.BlockSpec((TOP_K, ts), lambda b, s: (0, b * ns + s))
    return pl.pallas_call(
        _mixer_kernel,
        grid=(bsz, ns),
        in_specs=[in_spec] + [_const_spec(c.shape) for c in consts],
        out_specs=[tok_spec, packed_spec, idx_spec, idx_spec],
        out_shape=[
            jax.ShapeDtypeStruct((bsz, seq, d), F32),
            jax.ShapeDtypeStruct((bsz, seq, d // 2), jnp.int32),
            jax.ShapeDtypeStruct((TOP_K, t), jnp.int32),
            jax.ShapeDtypeStruct((TOP_K, t), F32),
        ],
        scratch_shapes=[
            pltpu.VMEM((ts, d), BF16),
            pltpu.VMEM((ts // Q_BLOCK, 2 * Q_BLOCK, ATTN_WIDTH), BF16),
            pltpu.VMEM((PREV + ts, ATTN_WIDTH), BF16),
            pltpu.VMEM((ATTN_WIDTH, PREV + ts), BF16),
            pltpu.VMEM((ts, ATTN_WIDTH), BF16),
            pltpu.VMEM((ts, SG_WIDTH), F32),
            pltpu.VMEM((ts, SG_WIDTH), F32),
            pltpu.VMEM((ts, SG_WIDTH), BF16),
            pltpu.VMEM((ts, d), F32),
            pltpu.VMEM((ts, d), F32),
        ],
        compiler_params=pltpu.CompilerParams(
            dimension_semantics=("arbitrary", "arbitrary"),
            vmem_limit_bytes=VMEM_BYTES_V7X * 7 // 8),
        name="mixer",
    )(x, *consts)


def _attention_bias(rel_bias):
    i = np.arange(Q_BLOCK)[:, None]
    j = np.arange(WIN)[None, :]
    kk = j - (i // CHUNK) * CHUNK
    inside = (kk >= 0) & (kk < BAND)
    tab = rel_bias.astype(F32)
    far = PREV + Q_BLOCK - 1 - REL_CLIP + 1
    near = Q_BLOCK - 1 + REL_CLIP
    ext = jnp.concatenate(
        [jnp.broadcast_to(tab[:, 2 * REL_CLIP:], (N_HEADS, far)),
         tab[:, 2 * REL_CLIP - near:2 * REL_CLIP][:, ::-1]], axis=1)
    rows = [ext[:, Q_BLOCK - 1 - r:Q_BLOCK - 1 - r + WIN] for r in range(Q_BLOCK)]
    bias = jnp.where(inside[None], jnp.stack(rows, axis=1), NEG)
    return (bias.reshape(N_HEADS // 2, 2 * Q_BLOCK, WIN).transpose(0, 2, 1))


def _rank_kernel(ti_ref, upper_ref, dest_ref, cnt_ref, total_scr, base_scr):
    phase = pl.program_id(0)
    i = pl.program_id(1)
    tt = ti_ref.shape[1]

    @pl.when((phase == 0) & (i == 0))
    def _():
        total_scr[...] = jnp.zeros_like(total_scr)

    ti = ti_ref[...]
    e_iota = lax.broadcasted_iota(jnp.int32, (N_EXPERTS, tt), 0)
    hits = [e_iota == ti[k:k + 1, :] for k in range(TOP_K)]
    cnt = hits[0].astype(F32)
    for k in range(1, TOP_K):
        cnt = cnt + hits[k].astype(F32)
    tile_total = jnp.sum(cnt, axis=1, keepdims=True)

    @pl.when(phase == 0)
    def _():
        total_scr[...] = total_scr[...] + tile_total

    @pl.when((phase == 1) & (i == 0))
    def _():
        cnt_ref[...] = total_scr[...].astype(jnp.int32)
        padded = jnp.floor((total_scr[...] + (ROW_TILE - 1)) * (1.0 / ROW_TILE)) * ROW_TILE
        run = jnp.zeros((1, LANES), F32)
        for e in range(N_EXPERTS):
            base_scr[e:e + 1, :] = run
            run = run + padded[e:e + 1, :]

    @pl.when(phase == 1)
    def _():
        before = _dot(cnt.astype(BF16), upper_ref[...]) + base_scr[:, 0:1]
        for k in range(TOP_K):
            dest_ref[k:k + 1, :] = jnp.sum(jnp.where(hits[k], before, 0.0), axis=0,
                                           keepdims=True).astype(jnp.int32)
        base_scr[...] = base_scr[...] + tile_total


def _rank(top_i):
    t = top_i.shape[1]
    tt = RANK_TILE
    upper = jnp.asarray(np.triu(np.ones((tt, tt), np.float32), 1), BF16)
    return pl.pallas_call(
        _rank_kernel,
        grid=(2, t // tt),
        in_specs=[pl.BlockSpec((TOP_K, tt), lambda p, i: (0, i)),
                  pl.BlockSpec((tt, tt), lambda p, i: (0, 0), pipeline_mode=pl.Buffered(1))],
        out_specs=[pl.BlockSpec((TOP_K, tt), lambda p, i: (0, i * p)),
                   pl.BlockSpec((N_EXPERTS, LANES), lambda p, i: (0, 0))],
        out_shape=[jax.ShapeDtypeStruct((TOP_K, t), jnp.int32),
                   jax.ShapeDtypeStruct((N_EXPERTS, LANES), jnp.int32)],
        scratch_shapes=[pltpu.VMEM((N_EXPERTS, LANES), F32),
                        pltpu.VMEM((N_EXPERTS, LANES), F32)],
        compiler_params=pltpu.CompilerParams(dimension_semantics=("arbitrary", "arbitrary")),
        name="rank",
    )(top_i, upper)


def _plan_kernel(cnt_ref, start_ref):
    def per_expert(e, tile):
        start_ref[e] = tile
        return tile + (cnt_ref[e] + ROW_TILE - 1) // ROW_TILE

    lax.fori_loop(0, N_EXPERTS, per_expert, jnp.int32(0))


def _plan(counts):
    smem = pl.BlockSpec(memory_space=pltpu.SMEM)
    return pl.pallas_call(
        _plan_kernel,
        in_specs=[smem],
        out_specs=smem,
        out_shape=jax.ShapeDtypeStruct((N_EXPERTS,), jnp.int32),
        name="plan",
    )(counts)


def _scatter(h2, dest, n_slots):
    t, d = h2.shape
    per_worker = t // SC_WORKERS
    n = SC_SCATTER_CHUNK
    nch = per_worker // n
    assert nch % 2 == 0
    mesh = plsc.VectorSubcoreMesh(core_axis_name="c", subcore_axis_name="s")

    @functools.partial(
        pl.kernel, mesh=mesh,
        out_type=jax.ShapeDtypeStruct((n_slots, d), h2.dtype),
        scratch_types=[pltpu.VMEM((TOP_K, nch, n), jnp.int32), pltpu.VMEM((2, n, d), h2.dtype),
                       pltpu.SemaphoreType.DMA((2,)), pltpu.SemaphoreType.DMA((2,))],
        name="sc_scatter")
    def scatter_kernel(h_hbm, dest_hbm, xs_hbm, idx_v, rows_v, sem_in, sem_out):
        wid = lax.axis_index("s") * SC_CORES + lax.axis_index("c")
        base = wid * per_worker
        for k in range(TOP_K):
            pltpu.sync_copy(dest_hbm.at[k, pl.ds(wid * nch, nch)], idx_v.at[k])

        def load(c, s):
            return pltpu.make_async_copy(h_hbm.at[pl.ds(base + c * n, n)], rows_v.at[s],
                                         sem_in.at[s])

        def store(c, s, k):
            return pltpu.make_async_copy(rows_v.at[s], xs_hbm.at[idx_v.at[k, c]], sem_out.at[s])

        load(0, 0).start()

        @pl.loop(0, nch, step=2)
        def _(c):
            for s in range(2):
                cc = c + s

                @pl.when(cc >= 1)
                def _():
                    for k in range(TOP_K):
                        store(cc - 1, 1 - s, k).wait()

                @pl.when(cc + 1 < nch)
                def _():
                    load(cc + 1, 1 - s).start()

                load(cc, s).wait()
                for k in range(TOP_K):
                    store(cc, s, k).start()

        for k in range(TOP_K):
            store(nch - 1, (nch - 1) % 2, k).wait()

    return scatter_kernel(h2, dest.reshape(TOP_K, t // n, n))


def _expert_kernel(start_ref, cnt_ref, xs_hbm, wgu_ref, bgu_ref, wd_ref, bd_ref, y_hbm,
                   wgu_bf, wd_bf, x_raw, x_bf, y_out, sem_in, sem_out):
    e = pl.program_id(0)
    n_rows = cnt_ref[e]
    n_tiles = (n_rows + ROW_TILE - 1) // ROW_TILE
    tile0 = start_ref[e]

    def fetch(i, slot):
        tile = tile0 + jnp.minimum(i, n_tiles - 1)
        return pltpu.make_async_copy(xs_hbm.at[pl.ds(tile * ROW_TILE, ROW_TILE)],
                                     x_raw.at[slot], sem_in.at[slot])

    def emit(i, slot):
        return pltpu.make_async_copy(y_out.at[slot],
                                     y_hbm.at[pl.ds((tile0 + i) * ROW_TILE, ROW_TILE)],
                                     sem_out.at[slot])

    def unpack(i, slot):
        valid = n_rows - jnp.minimum(i, n_tiles - 1) * ROW_TILE
        row = lax.broadcasted_iota(jnp.int32, x_raw.shape[1:], 0)
        x_bf[slot] = _unpack_bf16_pairs(jnp.where(row < valid, x_raw[slot], 0))

    @pl.when(n_tiles > 0)
    def _():
        wgu_bf[...] = wgu_ref[0].astype(BF16)
        wd_bf[...] = wd_ref[0].astype(BF16)
        fetch(0, 0).start()
        fetch(1, 1).start()
        fetch(0, 0).wait()
        unpack(0, 0)

        def tile_step(i, carry):
            slot = i % 2
            other = 1 - slot

            @pl.when(i >= 2)
            def _():
                emit(i - 2, slot).wait()

            fetch(i + 1, other).wait()
            unpack(i + 1, other)
            fetch(i + 2, slot).start()

            gu = _dot(x_bf[slot], wgu_bf[...]) + bgu_ref[0]
            gate = jnp.minimum(gu[:, :D_FF], SWIGLU_LIMIT)
            up = jnp.clip(gu[:, D_FF:], -SWIGLU_LIMIT, SWIGLU_LIMIT)
            act = (up + 1.0) * (gate * jax.nn.sigmoid(gate * SWIGLU_ALPHA))
            y_out[slot] = _pack_bf16_pairs(_dot(act.astype(BF16), wd_bf[...]) + bd_ref[0])
            emit(i, slot).start()
            return carry

        lax.fori_loop(0, n_tiles, tile_step, 0)

        fetch(n_tiles + 1, (n_tiles - 1) % 2).wait()

        @pl.when(n_tiles >= 2)
        def _():
            emit(n_tiles - 2, n_tiles % 2).wait()

        emit(n_tiles - 1, (n_tiles - 1) % 2).wait()

    @pl.when(e == N_EXPERTS - 1)
    def _():
        y_out[0] = jnp.zeros(y_out.shape[1:], y_out.dtype)

        def fill(tile, carry):
            blank = pltpu.make_async_copy(y_out.at[0], y_hbm.at[pl.ds(tile * ROW_TILE, ROW_TILE)],
                                          sem_out.at[0])
            blank.start()
            blank.wait()
            return carry

        lax.fori_loop(tile0 + n_tiles, y_hbm.shape[0] // ROW_TILE, fill, 0)


def _experts(xs, start_tile, counts, wgu, bgu, wd, bd):
    n_rows, dp = xs.shape
    d = 2 * dp
    tm = ROW_TILE
    by_expert = lambda e, st, ct: (e, 0, 0)
    grid_spec = pltpu.PrefetchScalarGridSpec(
        num_scalar_prefetch=2,
        grid=(N_EXPERTS,),
        in_specs=[
            pl.BlockSpec(memory_space=pl.ANY),
            pl.BlockSpec((1, d, 2 * D_FF), by_expert),
            pl.BlockSpec((1, 1, 2 * D_FF), by_expert),
            pl.BlockSpec((1, D_FF, d), by_expert),
            pl.BlockSpec((1, 1, d), by_expert),
        ],
        out_specs=pl.BlockSpec(memory_space=pl.ANY),
        scratch_shapes=[pltpu.VMEM((d, 2 * D_FF), BF16), pltpu.VMEM((D_FF, d), BF16),
                        pltpu.VMEM((2, tm, dp), jnp.int32), pltpu.VMEM((2, tm, d), BF16),
                        pltpu.VMEM((2, tm, dp), jnp.int32),
                        pltpu.SemaphoreType.DMA((2,)), pltpu.SemaphoreType.DMA((2,))],
    )
    return pl.pallas_call(
        _expert_kernel,
        grid_spec=grid_spec,
        out_shape=jax.ShapeDtypeStruct((n_rows, dp), jnp.int32),
        compiler_params=pltpu.CompilerParams(
            dimension_semantics=("arbitrary",),
            vmem_limit_bytes=VMEM_BYTES_V7X * 7 // 8),
        name="experts",
    )(start_tile, counts, xs, wgu, bgu, wd, bd)


def _combine(x1, probs, dest, y, out_ref, row0):
    t, d = x1.shape
    per_worker = t // SC_WORKERS
    dp = d // 2
    n = SC_COMBINE_CHUNK
    nch = per_worker // n
    assert nch % 2 == 0
    mesh = plsc.VectorSubcoreMesh(core_axis_name="c", subcore_axis_name="s")
    high_half = -65536

    @functools.partial(
        pl.kernel, mesh=mesh,
        out_type=(),
        scratch_types=[pltpu.VMEM((TOP_K, per_worker), jnp.int32),
                       pltpu.VMEM((2, TOP_K, n, dp), jnp.int32),
                       pltpu.VMEM((2, n, d), F32), pltpu.VMEM((2, n, d), F32),
                       pltpu.VMEM((TOP_K, per_worker), F32),
                       pltpu.SemaphoreType.DMA((2,)), pltpu.SemaphoreType.DMA((2,))],
        compiler_params=pltpu.CompilerParams(needs_layout_passes=False),
        name="sc_combine")
    def combine_kernel(x1_hbm, p_hbm, dest_hbm, y_hbm, out_hbm,
                       idx_v, rows_v, x_v, o_v, p_v, sem_in, sem_out):
        wid = lax.axis_index("s") * SC_CORES + lax.axis_index("c")
        base = wid * per_worker
        for k in range(TOP_K):
            pltpu.sync_copy(dest_hbm.at[k, pl.ds(base, per_worker)], idx_v.at[k])
            pltpu.sync_copy(p_hbm.at[k, pl.ds(base, per_worker)], p_v.at[k])

        def loads(c, s):
            off = base + c * n
            cps = [pltpu.make_async_copy(y_hbm.at[idx_v.at[k, pl.ds(c * n, n)]], rows_v.at[s, k],
                                         sem_in.at[s])
                   for k in range(TOP_K)]
            cps.append(pltpu.make_async_copy(x1_hbm.at[pl.ds(off, n)], x_v.at[s], sem_in.at[s]))
            return cps

        def store(c, s):
            return pltpu.make_async_copy(o_v.at[s], out_hbm.at[pl.ds(row0 + base + c * n, n)],
                                         sem_out.at[s])

        for cp in loads(0, 0):
            cp.start()

        @pl.loop(0, nch, step=2)
        def _(c):
            for s in range(2):
                cc = c + s

                @pl.when(cc + 1 < nch)
                def _():
                    for cp in loads(cc + 1, 1 - s):
                        cp.start()

                for cp in loads(cc, s):
                    cp.wait()

                @pl.when(cc >= 2)
                def _():
                    store(cc - 2, s).wait()

                @pl.loop(0, n)
                def _(r):
                    tok = jnp.full((SC_LANES,), cc * n + r, jnp.int32)
                    w = [plsc.load_gather(p_v, [jnp.full((SC_LANES,), k, jnp.int32), tok])
                         for k in range(TOP_K)]
                    for g in range(dp // SC_LANES):
                        lo_sl = pl.ds(g * SC_LANES, SC_LANES)
                        hi_sl = pl.ds(dp + g * SC_LANES, SC_LANES)
                        lo = x_v[s, r, lo_sl]
                        hi = x_v[s, r, hi_sl]
                        for k in range(TOP_K):
                            word = rows_v[s, k, r, lo_sl]
                            lo = lo + w[k] * plsc.bitcast(word << 16, F32)
                            hi = hi + w[k] * plsc.bitcast(word & high_half, F32)
                        o_v[s, r, lo_sl] = lo
                        o_v[s, r, hi_sl] = hi

                store(cc, s).start()

        store(nch - 2, 0).wait()
        store(nch - 1, 1).wait()

    combine_kernel(x1, probs, dest, y, out_ref)


def _layer(x, norm1_g, w_in, q_norm_g, k_norm_g, rel_bias, sg_ln_g, sg_ln_b, sg_w, sg_b,
           w_branch_a, w_branch_b, w_out, norm2_g, router_w, router_b,
           w_gate_up, b_gate_up, w_down, b_down):
    bsz, seq, d = x.shape
    t = bsz * seq
    row = lambda v: v.reshape(1, -1).astype(F32)
    scale = 1.0 / math.sqrt(HEAD_DIM)
    sgb = jnp.repeat(sg_b.T.astype(F32), SG_WIDTH // SG_GROUPS, axis=1)

    rw = router_w.T.astype(F32)
    rw_hi = rw.astype(BF16)
    rw_lo = (rw - rw_hi.astype(F32)).astype(BF16)
    mixer_consts = (
        row(norm1_g), w_in.astype(BF16), w_in[:, _C_V:_C_V + ATTN_WIDTH].T.astype(BF16),
        row(jnp.tile(q_norm_g, N_HEADS)) * scale, row(jnp.tile(k_norm_g, N_HEADS)),
        _attention_bias(rel_bias), row(sg_ln_g), row(sg_ln_b),
        sg_w.astype(F32), sgb, w_branch_a.astype(BF16), w_branch_b.astype(BF16),
        w_out.astype(BF16), row(norm2_g), jnp.concatenate([rw_hi, rw_lo], axis=0),
        router_b.reshape(N_EXPERTS, 1).astype(F32))
    expert_params = (w_gate_up.astype(F32), b_gate_up.reshape(N_EXPERTS, 1, -1).astype(F32),
                     w_down.astype(F32), b_down.reshape(N_EXPERTS, 1, -1).astype(F32))

    groups = MOE_GROUPS if bsz % MOE_GROUPS == 0 else 1
    gb = bsz // groups
    tg = gb * seq
    n_tiles = tg * TOP_K // ROW_TILE + N_EXPERTS
    out_ref = jax.new_ref(lax.empty((t, d), F32))

    staged = []
    for g in range(groups):
        x1, h2, top_i, probs = _mixer(x, g * gb, gb, mixer_consts)
        dest, counts = _rank(top_i)
        counts = counts[:, 0]
        xs = _scatter(h2.reshape(tg, d // 2), dest, n_tiles * ROW_TILE)
        staged.append((x1.reshape(tg, d), probs, dest, counts, xs))
    for g, (x1, probs, dest, counts, xs) in enumerate(staged):
        y = _experts(xs, _plan(counts), counts, *expert_params)
        _combine(x1, probs, dest, y, out_ref, g * tg)
    return jax.freeze(out_ref).reshape(bsz, seq, d)


def kernel(x, norm1_g, w_in, q_norm_g, k_norm_g, rel_bias, sg_ln_g, sg_ln_b, sg_w, sg_b,
           w_branch_a, w_branch_b, w_out, norm2_g, router_w, router_b,
           w_gate_up, b_gate_up, w_down, b_down):
    depth = norm1_g.shape[0]
    for l in range(depth):
        x = _layer(x, norm1_g[l], w_in[l], q_norm_g[l], k_norm_g[l], rel_bias[l], sg_ln_g[l],
                   sg_ln_b[l], sg_w[l], sg_b[l], w_branch_a[l], w_branch_b[l], w_out[l],
                   norm2_g[l], router_w[l], router_b[l], w_gate_up[l], b_gate_up[l],
                   w_down[l], b_down[l])
    return x
```

```python
import functools
import math

import numpy as np
import jax
import jax.numpy as jnp
from jax import lax
from jax.experimental import pallas as pl
from jax.experimental.pallas import tpu as pltpu
from jax.experimental.pallas import tpu_sc as plsc

F32 = jnp.float32
BF16 = jnp.bfloat16

D_MODEL = 1024
CHUNK = 64
N_BACK = 8
BAND = (N_BACK + 1) * CHUNK
N_HEADS = 8
HEAD_DIM = 64
ATTN_WIDTH = N_HEADS * HEAD_DIM
REL_CLIP = 256
SG_BLOCK = 128
SG_GROUPS = 4
SG_WIDTH = 512
N_EXPERTS = 32
TOP_K = 4
D_FF = D_MODEL
SWIGLU_LIMIT = 7.0
SWIGLU_ALPHA = 1.702
EPS = 1e-6
NEG = -1e30

LANES = 128
VMEM_BYTES_V7X = 64 * 1024 * 1024
VMEM_LIMIT_BYTES = VMEM_BYTES_V7X * 7 // 8

SEQ_TILE = 512
Q_BLOCK = 2 * CHUNK
PREV = N_BACK * CHUNK
WIN = PREV + Q_BLOCK
GATE_CHUNK = 256
RANK_TILE = 1024
ROW_TILE = 512

SC_CORES = 2
SC_SUBCORES = 16
SC_LANES = 16
SC_WORKERS = SC_CORES * SC_SUBCORES
SC_SCATTER_CHUNK = 64
SC_COMBINE_CHUNK = 8

_C_Q = 0
_C_K = ATTN_WIDTH
_C_V = 2 * ATTN_WIDTH
_C_U = 3 * ATTN_WIDTH
_C_VS = _C_U + SG_WIDTH
_C_GA = _C_VS + SG_WIDTH
_C_GB = _C_GA + D_MODEL


def _dot(a, b):
    return jnp.dot(a, b, preferred_element_type=F32)


def _pack_bf16_pairs(v):
    w = v.shape[1] // 2
    lo = lax.bitcast_convert_type(v[:, :w].astype(BF16).astype(F32), jnp.uint32)
    hi = lax.bitcast_convert_type(v[:, w:].astype(BF16).astype(F32), jnp.uint32)
    return lax.bitcast_convert_type((lo >> 16) | hi, jnp.int32)


def _unpack_bf16_pairs(p):
    u = lax.bitcast_convert_type(p, jnp.uint32)
    lo = lax.bitcast_convert_type(u << 16, F32)
    hi = lax.bitcast_convert_type(u & jnp.uint32(0xFFFF0000), F32)
    return jnp.concatenate([lo, hi], axis=1).astype(BF16)


def _dot_nt(a, b, precision=None):
    return lax.dot_general(a, b, (((1,), (1,)), ((), ())),
                           preferred_element_type=F32, precision=precision)


def _mixer_kernel(x_ref, g1_ref, win_ref, wvt_ref, qg_ref, kg_ref, bias_ref, lng_ref,
                  lnb_ref, sgw_ref, sgb_ref, wa_ref, wb_ref, wo_ref, g2_ref, rwt_ref, rb_ref,
                  x1_ref, h2_ref, ti_ref, pr_ref,
                  h_scr, q_scr, k_win, vt_win, ya_scr, u_scr, vs_scr, ysg_scr, ga_scr, mb_scr):
    ts = x_ref.shape[1]
    s_idx = pl.program_id(1)

    @pl.when(s_idx == 0)
    def _():
        k_win[0:PREV, :] = jnp.zeros((PREV, ATTN_WIDTH), BF16)
        vt_win[:, 0:PREV] = jnp.zeros((ATTN_WIDTH, PREV), BF16)

    @pl.when(s_idx > 0)
    def _():
        k_win[0:PREV, :] = k_win[ts:ts + PREV, :]
        vt_win[:, 0:PREV] = vt_win[:, ts:ts + PREV]

    x = x_ref[0]
    ms = jnp.mean(x * x, axis=-1, keepdims=True)
    h_scr[...] = (x * lax.rsqrt(ms + EPS) * g1_ref[...]).astype(BF16)

    def proj(lo, width):
        return _dot(h_scr[...], win_ref[:, lo:lo + width])

    def head_rms(t, g_ref):
        low = lax.broadcasted_iota(jnp.int32, (1, 2 * HEAD_DIM), 1) < HEAD_DIM
        cols = []
        for p in range(N_HEADS // 2):
            blk = t[:, p * 2 * HEAD_DIM:(p + 1) * 2 * HEAD_DIM]
            sq = blk * blk
            s_lo = jnp.sum(jnp.where(low, sq, 0.0), axis=-1, keepdims=True)
            s_hi = jnp.sum(jnp.where(low, 0.0, sq), axis=-1, keepdims=True)
            r = lax.rsqrt(jnp.where(low, s_lo, s_hi) * (1.0 / HEAD_DIM) + EPS)
            cols.append(blk * r)
        return jnp.concatenate(cols, axis=1) * g_ref[...]

    lane = lax.broadcasted_iota(jnp.int32, (1, ATTN_WIDTH), 1)
    even_head = (lane % (2 * HEAD_DIM)) < HEAD_DIM
    q_raw = proj(_C_Q, ATTN_WIDTH)
    k_raw = proj(_C_K, ATTN_WIDTH)
    vt_win[:, PREV:PREV + ts] = _dot_nt(wvt_ref[...], h_scr[...]).astype(BF16)
    qn = head_rms(q_raw, qg_ref)
    q_even = jnp.where(even_head, qn, 0.0).astype(BF16)
    q_odd = jnp.where(even_head, 0.0, qn).astype(BF16)
    for qb in range(ts // Q_BLOCK):
        rs = slice(qb * Q_BLOCK, (qb + 1) * Q_BLOCK)
        q_scr[qb, 0:Q_BLOCK, :] = q_even[rs]
        q_scr[qb, Q_BLOCK:2 * Q_BLOCK, :] = q_odd[rs]
    k_win[PREV:PREV + ts, :] = head_rms(k_raw, kg_ref).astype(BF16)

    def sg_u():
        u_scr[...] = jax.nn.gelu(proj(_C_U, SG_WIDTH))

    def sg_v():
        vs_scr[...] = jax.nn.gelu(proj(_C_VS, SG_WIDTH))

    gdim = SG_WIDTH // SG_GROUPS

    def sg_group(g):
        cs = slice(g * gdim, (g + 1) * gdim)
        vg = vs_scr[:, cs]
        mu = jnp.mean(vg, axis=-1, keepdims=True)
        xc = vg - mu
        var = jnp.mean(xc * xc, axis=-1, keepdims=True)
        vn = (xc * lax.rsqrt(var + EPS) * lng_ref[:, cs] + lnb_ref[:, cs]).astype(BF16)
        r_i = lax.broadcasted_iota(jnp.int32, (SG_BLOCK, SG_BLOCK), 0)
        c_i = lax.broadcasted_iota(jnp.int32, (SG_BLOCK, SG_BLOCK), 1)
        wm = jnp.where(r_i >= c_i, sgw_ref[g], 0.0).astype(BF16)
        for j in range(ts // SG_BLOCK):
            rs = slice(j * SG_BLOCK, (j + 1) * SG_BLOCK)
            mixed = _dot(wm, vn[rs]) + sgb_ref[:, cs]
            ysg_scr[rs, cs] = (u_scr[rs, cs] * mixed).astype(BF16)

    def gate_a_chunk(c):
        cols = slice(c * GATE_CHUNK, (c + 1) * GATE_CHUNK)
        ga_scr[:, cols] = jax.nn.sigmoid(proj(_C_GA + c * GATE_CHUNK, GATE_CHUNK))

    def gated_b_chunk(c):
        cols = slice(c * GATE_CHUNK, (c + 1) * GATE_CHUNK)
        gate = jax.nn.sigmoid(proj(_C_GB + c * GATE_CHUNK, GATE_CHUNK))
        mb_scr[:, cols] = gate * _dot(ysg_scr[...], wb_ref[:, cols])

    n_chunks = D_MODEL // GATE_CHUNK
    side_work = [sg_u, sg_v] + [functools.partial(sg_group, g) for g in range(SG_GROUPS)]
    side_work += [functools.partial(gate_a_chunk, c) for c in range(n_chunks)]
    side_work += [functools.partial(gated_b_chunk, c) for c in range(n_chunks)]

    n_blocks = (ts // Q_BLOCK) * (N_HEADS // 2)
    schedule = [[] for _ in range(n_blocks)]
    for i, piece in enumerate(side_work):
        schedule[-(-(i + 1) * n_blocks // len(side_work)) - 1].append(piece)

    key_row = lax.broadcasted_iota(jnp.int32, (WIN, 1), 0)
    pairs = N_HEADS // 2

    def scores(block):
        qb, pair = divmod(block, pairs)
        r0 = qb * Q_BLOCK
        cs = slice(pair * 2 * HEAD_DIM, (pair + 1) * 2 * HEAD_DIM)
        st = _dot_nt(k_win[r0:r0 + WIN, cs], q_scr[qb, :, cs]) + bias_ref[pair]
        return jnp.where(key_row >= (PREV - (s_idx * ts + r0)), st, NEG)

    def weighted_values(block, p, inv):
        qb, pair = divmod(block, pairs)
        r0 = qb * Q_BLOCK
        cs = slice(pair * 2 * HEAD_DIM, (pair + 1) * 2 * HEAD_DIM)
        ot = _dot(vt_win[cs, r0:r0 + WIN], p)
        o = jnp.concatenate(
            [ot[0:HEAD_DIM, 0:Q_BLOCK] * inv[:, 0:Q_BLOCK],
             ot[HEAD_DIM:2 * HEAD_DIM, Q_BLOCK:2 * Q_BLOCK] * inv[:, Q_BLOCK:2 * Q_BLOCK]],
            axis=0)
        ya_scr[r0:r0 + Q_BLOCK, cs] = o.T.astype(BF16)

    st_next = scores(0)
    pending = None
    for block in range(n_blocks):
        st = st_next
        if block + 1 < n_blocks:
            st_next = scores(block + 1)
        if pending is not None:
            weighted_values(*pending)
        m = jnp.max(st, axis=0, keepdims=True)
        p = jnp.exp(st - m)
        inv = 1.0 / jnp.sum(p, axis=0, keepdims=True)
        pending = (block, p.astype(BF16), inv)
        for piece in schedule[block]:
            piece()
    weighted_values(*pending)

    merged = ga_scr[...] * _dot(ya_scr[...], wa_ref[...]) + mb_scr[...]
    x1 = x_ref[0] + _dot(merged.astype(BF16), wo_ref[...])
    x1_ref[0] = x1

    ms2 = jnp.mean(x1 * x1, axis=-1, keepdims=True)
    h2 = x1 * lax.rsqrt(ms2 + EPS) * g2_ref[...]
    h2_ref[0] = _pack_bf16_pairs(h2)
    h2_hi = h2.astype(BF16)
    h2_lo = (h2 - h2_hi.astype(F32)).astype(BF16)
    by_hi = _dot_nt(rwt_ref[...], h2_hi)
    lt = (by_hi[0:N_EXPERTS] + by_hi[N_EXPERTS:2 * N_EXPERTS]
          + _dot_nt(rwt_ref[0:N_EXPERTS, :], h2_lo) + rb_ref[...])
    e_iota = lax.broadcasted_iota(jnp.int32, lt.shape, 0)
    vals = []
    for k in range(TOP_K):
        m = jnp.max(lt, axis=0, keepdims=True)
        idx = jnp.min(jnp.where(lt == m, e_iota, N_EXPERTS), axis=0, keepdims=True)
        vals.append(m)
        ti_ref[k:k + 1, :] = idx
        lt = jnp.where(e_iota == idx, -jnp.inf, lt)
    exps = [jnp.exp(v - vals[0]) for v in vals]
    denom = exps[0] + exps[1] + exps[2] + exps[3]
    for k in range(TOP_K):
        pr_ref[k:k + 1, :] = exps[k] / denom


def _const_spec(shape):
    zeros = (0,) * len(shape)
    return pl.BlockSpec(shape, lambda b, s: zeros, pipeline_mode=pl.Buffered(1))


def _mixer(x, consts):
    bsz, seq, d = x.shape
    ts = SEQ_TILE
    ns = seq // ts
    t = bsz * seq
    tok_spec = pl.BlockSpec((1, ts, d), lambda b, s: (b, s, 0))
    packed_spec = pl.BlockSpec((1, ts, d // 2), lambda b, s: (b, s, 0))
    idx_spec = pl.BlockSpec((TOP_K, ts), lambda b, s: (0, b * ns + s))
    return pl.pallas_call(
        _mixer_kernel,
        grid=(bsz, ns),
        in_specs=[tok_spec] + [_const_spec(c.shape) for c in consts],
        out_specs=[tok_spec, packed_spec, idx_spec, idx_spec],
        out_shape=[
            jax.ShapeDtypeStruct((bsz, seq, d), F32),
            jax.ShapeDtypeStruct((bsz, seq, d // 2), jnp.int32),
            jax.ShapeDtypeStruct((TOP_K, t), jnp.int32),
            jax.ShapeDtypeStruct((TOP_K, t), F32),
        ],
        scratch_shapes=[
            pltpu.VMEM((ts, d), BF16),
            pltpu.VMEM((ts // Q_BLOCK, 2 * Q_BLOCK, ATTN_WIDTH), BF16),
            pltpu.VMEM((PREV + ts, ATTN_WIDTH), BF16),
            pltpu.VMEM((ATTN_WIDTH, PREV + ts), BF16),
            pltpu.VMEM((ts, ATTN_WIDTH), BF16),
            pltpu.VMEM((ts, SG_WIDTH), F32),
            pltpu.VMEM((ts, SG_WIDTH), F32),
            pltpu.VMEM((ts, SG_WIDTH), BF16),
            pltpu.VMEM((ts, d), F32),
            pltpu.VMEM((ts, d), F32),
        ],
        compiler_params=pltpu.CompilerParams(
            dimension_semantics=("arbitrary", "arbitrary"),
            vmem_limit_bytes=VMEM_LIMIT_BYTES),
        name="mixer",
    )(x, *consts)


def _attention_bias(rel_bias):
    i = np.arange(Q_BLOCK)[:, None]
    j = np.arange(WIN)[None, :]
    kk = j - (i // CHUNK) * CHUNK
    inside = (kk >= 0) & (kk < BAND)
    tab = rel_bias.astype(F32)
    far = PREV + Q_BLOCK - 1 - REL_CLIP + 1
    near = Q_BLOCK - 1 + REL_CLIP
    ext = jnp.concatenate(
        [jnp.broadcast_to(tab[:, 2 * REL_CLIP:], (N_HEADS, far)),
         tab[:, 2 * REL_CLIP - near:2 * REL_CLIP][:, ::-1]], axis=1)
    rows = [ext[:, Q_BLOCK - 1 - r:Q_BLOCK - 1 - r + WIN] for r in range(Q_BLOCK)]
    bias = jnp.where(inside[None], jnp.stack(rows, axis=1), NEG)
    return (bias.reshape(N_HEADS // 2, 2 * Q_BLOCK, WIN).transpose(0, 2, 1))


def _rank_kernel(ti_ref, upper_ref, dest_ref, cnt_ref, total_scr, base_scr):
    phase = pl.program_id(0)
    i = pl.program_id(1)
    tt = ti_ref.shape[1]

    @pl.when((phase == 0) & (i == 0))
    def _():
        total_scr[...] = jnp.zeros_like(total_scr)

    ti = ti_ref[...]
    e_iota = lax.broadcasted_iota(jnp.int32, (N_EXPERTS, tt), 0)
    hits = [e_iota == ti[k:k + 1, :] for k in range(TOP_K)]
    cnt = hits[0].astype(F32)
    for k in range(1, TOP_K):
        cnt = cnt + hits[k].astype(F32)
    tile_total = jnp.sum(cnt, axis=1, keepdims=True)

    @pl.when(phase == 0)
    def _():
        total_scr[...] = total_scr[...] + tile_total

    @pl.when((phase == 1) & (i == 0))
    def _():
        cnt_ref[...] = total_scr[...].astype(jnp.int32)
        padded = jnp.floor((total_scr[...] + (ROW_TILE - 1)) * (1.0 / ROW_TILE)) * ROW_TILE
        run = jnp.zeros((1, LANES), F32)
        for e in range(N_EXPERTS):
            base_scr[e:e + 1, :] = run
            run = run + padded[e:e + 1, :]

    @pl.when(phase == 1)
    def _():
        before = _dot(cnt.astype(BF16), upper_ref[...]) + base_scr[:, 0:1]
        for k in range(TOP_K):
            dest_ref[k:k + 1, :] = jnp.sum(jnp.where(hits[k], before, 0.0), axis=0,
                                           keepdims=True).astype(jnp.int32)
        base_scr[...] = base_scr[...] + tile_total


def _rank(top_i):
    t = top_i.shape[1]
    tt = RANK_TILE
    upper = jnp.asarray(np.triu(np.ones((tt, tt), np.float32), 1), BF16)
    return pl.pallas_call(
        _rank_kernel,
        grid=(2, t // tt),
        in_specs=[pl.BlockSpec((TOP_K, tt), lambda p, i: (0, i)),
                  pl.BlockSpec((tt, tt), lambda p, i: (0, 0), pipeline_mode=pl.Buffered(1))],
        out_specs=[pl.BlockSpec((TOP_K, tt), lambda p, i: (0, i * p)),
                   pl.BlockSpec((N_EXPERTS, LANES), lambda p, i: (0, 0))],
        out_shape=[jax.ShapeDtypeStruct((TOP_K, t), jnp.int32),
                   jax.ShapeDtypeStruct((N_EXPERTS, LANES), jnp.int32)],
        scratch_shapes=[pltpu.VMEM((N_EXPERTS, LANES), F32),
                        pltpu.VMEM((N_EXPERTS, LANES), F32)],
        compiler_params=pltpu.CompilerParams(dimension_semantics=("arbitrary", "arbitrary")),
        name="rank",
    )(top_i, upper)


def _plan_kernel(cnt_ref, tile_ref, exp_ref, rows_ref, first_ref):
    n_tiles = exp_ref.shape[0]

    def per_expert(e, w):
        c = cnt_ref[e]
        n = (c + ROW_TILE - 1) // ROW_TILE

        def per_tile(j, _):
            tile_ref[w + j] = w + j
            exp_ref[w + j] = e
            rows_ref[w + j] = jnp.minimum(c - j * ROW_TILE, ROW_TILE)
            first_ref[w + j] = (j == 0).astype(jnp.int32)
            return 0

        lax.fori_loop(0, n, per_tile, 0)
        return w + n

    n_used = lax.fori_loop(0, N_EXPERTS, per_expert, jnp.int32(0))

    def pad(w, _):
        tile_ref[w] = n_used - 1
        exp_ref[w] = exp_ref[n_used - 1]
        rows_ref[w] = 0
        first_ref[w] = 0
        return 0

    lax.fori_loop(n_used, n_tiles, pad, 0)


def _plan(counts, n_tiles):
    smem = pl.BlockSpec(memory_space=pltpu.SMEM)
    vec = jax.ShapeDtypeStruct((n_tiles,), jnp.int32)
    return pl.pallas_call(
        _plan_kernel,
        in_specs=[smem],
        out_specs=[smem, smem, smem, smem],
        out_shape=[vec, vec, vec, vec],
        name="plan",
    )(counts)


def _scatter(h2, dest, n_slots):
    t, d = h2.shape
    per_worker = t // SC_WORKERS
    n = SC_SCATTER_CHUNK
    nch = per_worker // n
    assert nch % 2 == 0
    mesh = plsc.VectorSubcoreMesh(core_axis_name="c", subcore_axis_name="s")

    @functools.partial(
        pl.kernel, mesh=mesh,
        out_type=jax.ShapeDtypeStruct((n_slots, d), h2.dtype),
        scratch_types=[pltpu.VMEM((TOP_K, nch, n), jnp.int32), pltpu.VMEM((2, n, d), h2.dtype),
                       pltpu.SemaphoreType.DMA((2,)), pltpu.SemaphoreType.DMA((2,))],
        name="sc_scatter")
    def scatter_kernel(h_hbm, dest_hbm, xs_hbm, idx_v, rows_v, sem_in, sem_out):
        wid = lax.axis_index("s") * SC_CORES + lax.axis_index("c")
        base = wid * per_worker
        for k in range(TOP_K):
            pltpu.sync_copy(dest_hbm.at[k, pl.ds(wid * nch, nch)], idx_v.at[k])

        def load(c, s):
            return pltpu.make_async_copy(h_hbm.at[pl.ds(base + c * n, n)], rows_v.at[s],
                                         sem_in.at[s])

        def store(c, s, k):
            return pltpu.make_async_copy(rows_v.at[s], xs_hbm.at[idx_v.at[k, c]], sem_out.at[s])

        load(0, 0).start()

        @pl.loop(0, nch, step=2)
        def _(c):
            for s in range(2):
                cc = c + s

                @pl.when(cc >= 1)
                def _():
                    for k in range(TOP_K):
                        store(cc - 1, 1 - s, k).wait()

                @pl.when(cc + 1 < nch)
                def _():
                    load(cc + 1, 1 - s).start()

                load(cc, s).wait()
                for k in range(TOP_K):
                    store(cc, s, k).start()

        for k in range(TOP_K):
            store(nch - 1, (nch - 1) % 2, k).wait()

    return scatter_kernel(h2, dest.reshape(TOP_K, t // n, n))


def _expert_kernel(tile_ref, exp_ref, rows_ref, first_ref,
                   xs_ref, wgu_ref, bgu_ref, wd_ref, bd_ref, y_ref, wgu_bf, wd_bf):
    w = pl.program_id(0)

    @pl.when(first_ref[w] == 1)
    def _():
        wgu_bf[...] = wgu_ref[0].astype(BF16)
        wd_bf[...] = wd_ref[0].astype(BF16)

    n_rows = rows_ref[w]

    @pl.when(n_rows > 0)
    def _():
        row = lax.broadcasted_iota(jnp.int32, xs_ref.shape, 0)
        x = _unpack_bf16_pairs(jnp.where(row < n_rows, xs_ref[...], 0))
        gu = _dot(x, wgu_bf[...]) + bgu_ref[0]
        gate = jnp.minimum(gu[:, :D_FF], SWIGLU_LIMIT)
        up = jnp.clip(gu[:, D_FF:], -SWIGLU_LIMIT, SWIGLU_LIMIT)
        act = (up + 1.0) * (gate * jax.nn.sigmoid(gate * SWIGLU_ALPHA))
        y_ref[...] = _pack_bf16_pairs(_dot(act.astype(BF16), wd_bf[...]) + bd_ref[0])


def _experts(xs, plan, wgu, bgu, wd, bd):
    n_rows, dp = xs.shape
    d = 2 * dp
    w_tile, w_exp, w_rows, w_first = plan
    n_items = w_tile.shape[0]
    tm = ROW_TILE
    by_tile = lambda w, tl, ex, nr, fi: (tl[w], 0)
    by_expert = lambda w, tl, ex, nr, fi: (ex[w], 0, 0)
    grid_spec = pltpu.PrefetchScalarGridSpec(
        num_scalar_prefetch=4,
        grid=(n_items,),
        in_specs=[
            pl.BlockSpec((tm, dp), by_tile),
            pl.BlockSpec((1, d, 2 * D_FF), by_expert),
            pl.BlockSpec((1, 1, 2 * D_FF), by_expert),
            pl.BlockSpec((1, D_FF, d), by_expert),
            pl.BlockSpec((1, 1, d), by_expert),
        ],
        out_specs=pl.BlockSpec((tm, dp), by_tile),
        scratch_shapes=[pltpu.VMEM((d, 2 * D_FF), BF16), pltpu.VMEM((D_FF, d), BF16)],
    )
    return pl.pallas_call(
        _expert_kernel,
        grid_spec=grid_spec,
        out_shape=jax.ShapeDtypeStruct((n_rows, dp), jnp.int32),
        compiler_params=pltpu.CompilerParams(
            dimension_semantics=("arbitrary",),
            vmem_limit_bytes=VMEM_LIMIT_BYTES),
        name="experts",
    )(w_tile, w_exp, w_rows, w_first, xs, wgu, bgu, wd, bd)


def _combine(x1, probs, dest, y):
    t, d = x1.shape
    per_worker = t // SC_WORKERS
    dp = d // 2
    n = SC_COMBINE_CHUNK
    nch = per_worker // n
    assert nch % 2 == 0
    mesh = plsc.VectorSubcoreMesh(core_axis_name="c", subcore_axis_name="s")
    high_half = -65536

    @functools.partial(
        pl.kernel, mesh=mesh,
        out_type=jax.ShapeDtypeStruct((t, d), F32),
        scratch_types=[pltpu.VMEM((TOP_K, per_worker), jnp.int32),
                       pltpu.VMEM((2, TOP_K, n, dp), jnp.int32),
                       pltpu.VMEM((2, n, d), F32), pltpu.VMEM((2, n, d), F32),
                       pltpu.VMEM((TOP_K, per_worker), F32),
                       pltpu.SemaphoreType.DMA((2,)), pltpu.SemaphoreType.DMA((2,))],
        compiler_params=pltpu.CompilerParams(needs_layout_passes=False),
        name="sc_combine")
    def combine_kernel(x1_hbm, p_hbm, dest_hbm, y_hbm, out_hbm,
                       idx_v, rows_v, x_v, o_v, p_v, sem_in, sem_out):
        wid = lax.axis_index("s") * SC_CORES + lax.axis_index("c")
        base = wid * per_worker
        for k in range(TOP_K):
            pltpu.sync_copy(dest_hbm.at[k, pl.ds(base, per_worker)], idx_v.at[k])
            pltpu.sync_copy(p_hbm.at[k, pl.ds(base, per_worker)], p_v.at[k])

        def loads(c, s):
            off = base + c * n
            cps = [pltpu.make_async_copy(y_hbm.at[idx_v.at[k, pl.ds(c * n, n)]], rows_v.at[s, k],
                                         sem_in.at[s])
                   for k in range(TOP_K)]
            cps.append(pltpu.make_async_copy(x1_hbm.at[pl.ds(off, n)], x_v.at[s], sem_in.at[s]))
            return cps

        def store(c, s):
            return pltpu.make_async_copy(o_v.at[s], out_hbm.at[pl.ds(base + c * n, n)],
                                         sem_out.at[s])

        for cp in loads(0, 0):
            cp.start()

        @pl.loop(0, nch, step=2)
        def _(c):
            for s in range(2):
                cc = c + s

                @pl.when(cc + 1 < nch)
                def _():
                    for cp in loads(cc + 1, 1 - s):
                        cp.start()

                for cp in loads(cc, s):
                    cp.wait()

                @pl.when(cc >= 2)
                def _():
                    store(cc - 2, s).wait()

                @pl.loop(0, n)
                def _(r):
                    tok = jnp.full((SC_LANES,), cc * n + r, jnp.int32)
                    w = [plsc.load_gather(p_v, [jnp.full((SC_LANES,), k, jnp.int32), tok])
                         for k in range(TOP_K)]
                    for g in range(dp // SC_LANES):
                        lo_sl = pl.ds(g * SC_LANES, SC_LANES)
                        hi_sl = pl.ds(dp + g * SC_LANES, SC_LANES)
                        lo = x_v[s, r, lo_sl]
                        hi = x_v[s, r, hi_sl]
                        for k in range(TOP_K):
                            word = rows_v[s, k, r, lo_sl]
                            lo = lo + w[k] * plsc.bitcast(word << 16, F32)
                            hi = hi + w[k] * plsc.bitcast(word & high_half, F32)
                        o_v[s, r, lo_sl] = lo
                        o_v[s, r, hi_sl] = hi

                store(cc, s).start()

        store(nch - 2, 0).wait()
        store(nch - 1, 1).wait()

    return combine_kernel(x1, probs, dest, y)


def _layer(x, norm1_g, w_in, q_norm_g, k_norm_g, rel_bias, sg_ln_g, sg_ln_b, sg_w, sg_b,
           w_branch_a, w_branch_b, w_out, norm2_g, router_w, router_b,
           w_gate_up, b_gate_up, w_down, b_down):
    bsz, seq, d = x.shape
    t = bsz * seq
    row = lambda v: v.reshape(1, -1).astype(F32)
    scale = 1.0 / math.sqrt(HEAD_DIM)
    sgb = jnp.repeat(sg_b.T.astype(F32), SG_WIDTH // SG_GROUPS, axis=1)

    rw = router_w.T.astype(F32)
    rw_hi = rw.astype(BF16)
    rw_lo = (rw - rw_hi.astype(F32)).astype(BF16)
    mixer_consts = (
        row(norm1_g), w_in.astype(BF16), w_in[:, _C_V:_C_V + ATTN_WIDTH].T.astype(BF16),
        row(jnp.tile(q_norm_g, N_HEADS)) * scale, row(jnp.tile(k_norm_g, N_HEADS)),
        _attention_bias(rel_bias), row(sg_ln_g), row(sg_ln_b),
        sg_w.astype(F32), sgb, w_branch_a.astype(BF16), w_branch_b.astype(BF16),
        w_out.astype(BF16), row(norm2_g), jnp.concatenate([rw_hi, rw_lo], axis=0),
        router_b.reshape(N_EXPERTS, 1).astype(F32))
    expert_params = (w_gate_up.astype(F32), b_gate_up.reshape(N_EXPERTS, 1, -1).astype(F32),
                     w_down.astype(F32), b_down.reshape(N_EXPERTS, 1, -1).astype(F32))

    x1, h2, top_i, probs = _mixer(x, mixer_consts)
    dest, counts = _rank(top_i)
    n_tiles = t * TOP_K // ROW_TILE + N_EXPERTS
    plan = _plan(counts[:, 0], n_tiles)
    xs = _scatter(h2.reshape(t, d // 2), dest, n_tiles * ROW_TILE)
    y = _experts(xs, plan, *expert_params)
    return _combine(x1.reshape(t, d), probs, dest, y).reshape(bsz, seq, d)


def kernel(x, norm1_g, w_in, q_norm_g, k_norm_g, rel_bias, sg_ln_g, sg_ln_b, sg_w, sg_b,
           w_branch_a, w_branch_b, w_out, norm2_g, router_w, router_b,
           w_gate_up, b_gate_up, w_down, b_down):
    depth = norm1_g.shape[0]
    for l in range(depth):
        x = _layer(x, norm1_g[l], w_in[l], q_norm_g[l], k_norm_g[l], rel_bias[l], sg_ln_g[l],
                   sg_ln_b[l], sg_w[l], sg_b[l], w_branch_a[l], w_branch_b[l], w_out[l],
                   norm2_g[l], router_w[l], router_b[l], w_gate_up[l], b_gate_up[l],
                   w_down[l], b_down[l])
    return x
```

```python
import functools
import math

import numpy as np
import jax
import jax.numpy as jnp
from jax import lax
from jax.experimental import pallas as pl
from jax.experimental.pallas import tpu as pltpu
from jax.experimental.pallas import tpu_sc as plsc

F32 = jnp.float32
BF16 = jnp.bfloat16

D_MODEL = 1024
CHUNK = 64
N_BACK = 8
BAND = (N_BACK + 1) * CHUNK
N_HEADS = 8
HEAD_DIM = 64
ATTN_WIDTH = N_HEADS * HEAD_DIM
REL_CLIP = 256
SG_BLOCK = 128
SG_GROUPS = 4
SG_WIDTH = 512
N_EXPERTS = 32
TOP_K = 4
D_FF = D_MODEL
SWIGLU_LIMIT = 7.0
SWIGLU_ALPHA = 1.702
EPS = 1e-6
NEG = -1e30

LANES = 128
VMEM_BYTES_V7X = 64 * 1024 * 1024
VMEM_LIMIT_BYTES = VMEM_BYTES_V7X * 7 // 8

SEQ_TILE = 512
Q_BLOCK = 2 * CHUNK
PREV = N_BACK * CHUNK
WIN = PREV + Q_BLOCK
GATE_CHUNK = 256
RANK_TILE = 1024
ROW_TILE = 512

SC_CORES = 2
SC_SUBCORES = 16
SC_LANES = 16
SC_WORKERS = SC_CORES * SC_SUBCORES
SC_SCATTER_CHUNK = 64
SC_COMBINE_CHUNK = 8

_C_Q = 0
_C_K = ATTN_WIDTH
_C_V = 2 * ATTN_WIDTH
_C_U = 3 * ATTN_WIDTH
_C_VS = _C_U + SG_WIDTH
_C_GA = _C_VS + SG_WIDTH
_C_GB = _C_GA + D_MODEL


def _dot(a, b):
    return jnp.dot(a, b, preferred_element_type=F32)


def _pack_bf16_pairs(v):
    w = v.shape[1] // 2
    lo = lax.bitcast_convert_type(v[:, :w].astype(BF16).astype(F32), jnp.uint32)
    hi = lax.bitcast_convert_type(v[:, w:].astype(BF16).astype(F32), jnp.uint32)
    return lax.bitcast_convert_type((lo >> 16) | hi, jnp.int32)


def _unpack_bf16_pairs(p):
    u = lax.bitcast_convert_type(p, jnp.uint32)
    lo = lax.bitcast_convert_type(u << 16, F32)
    hi = lax.bitcast_convert_type(u & jnp.uint32(0xFFFF0000), F32)
    return jnp.concatenate([lo, hi], axis=1).astype(BF16)


def _dot_nt(a, b, precision=None):
    return lax.dot_general(a, b, (((1,), (1,)), ((), ())),
                           preferred_element_type=F32, precision=precision)


def _mixer_kernel(x_ref, g1_ref, win_ref, qg_ref, kg_ref, bias_ref, lng_ref,
                  lnb_ref, sgw_ref, sgb_ref, wa_ref, wb_ref, wo_ref, g2_ref, rwt_ref, rb_ref,
                  x1_ref, h2_ref, ti_ref, pr_ref, cnt_ref,
                  h_scr, q_scr, k_win, vt_win, ya_scr, u_scr, vs_scr, ysg_scr, ga_scr, mb_scr,
                  cnt_scr):
    ts = x_ref.shape[1]
    s_idx = pl.program_id(1)

    @pl.when(s_idx == 0)
    def _():
        k_win[0:PREV, :] = jnp.zeros((PREV, ATTN_WIDTH), BF16)
        vt_win[:, 0:PREV] = jnp.zeros((ATTN_WIDTH, PREV), BF16)

    @pl.when(s_idx > 0)
    def _():
        k_win[0:PREV, :] = k_win[ts:ts + PREV, :]
        vt_win[:, 0:PREV] = vt_win[:, ts:ts + PREV]

    x = x_ref[0]
    ms = jnp.mean(x * x, axis=-1, keepdims=True)
    h_scr[...] = (x * lax.rsqrt(ms + EPS) * g1_ref[...]).astype(BF16)

    def proj(lo, width):
        return _dot(h_scr[...], win_ref[:, lo:lo + width])

    def head_rms(t, g_ref):
        low = lax.broadcasted_iota(jnp.int32, (1, 2 * HEAD_DIM), 1) < HEAD_DIM
        cols = []
        for p in range(N_HEADS // 2):
            blk = t[:, p * 2 * HEAD_DIM:(p + 1) * 2 * HEAD_DIM]
            sq = blk * blk
            s_lo = jnp.sum(jnp.where(low, sq, 0.0), axis=-1, keepdims=True)
            s_hi = jnp.sum(jnp.where(low, 0.0, sq), axis=-1, keepdims=True)
            r = lax.rsqrt(jnp.where(low, s_lo, s_hi) * (1.0 / HEAD_DIM) + EPS)
            cols.append(blk * r)
        return jnp.concatenate(cols, axis=1) * g_ref[...]

    lane = lax.broadcasted_iota(jnp.int32, (1, ATTN_WIDTH), 1)
    even_head = (lane % (2 * HEAD_DIM)) < HEAD_DIM
    q_raw = proj(_C_Q, ATTN_WIDTH)
    k_raw = proj(_C_K, ATTN_WIDTH)
    vt_win[:, PREV:PREV + ts] = lax.dot_general(
        win_ref[:, _C_V:_C_V + ATTN_WIDTH], h_scr[...], (((0,), (1,)), ((), ())),
        preferred_element_type=F32).astype(BF16)
    qn = head_rms(q_raw, qg_ref)
    q_even = jnp.where(even_head, qn, 0.0).astype(BF16)
    q_odd = jnp.where(even_head, 0.0, qn).astype(BF16)
    for qb in range(ts // Q_BLOCK):
        rs = slice(qb * Q_BLOCK, (qb + 1) * Q_BLOCK)
        q_scr[qb, 0:Q_BLOCK, :] = q_even[rs]
        q_scr[qb, Q_BLOCK:2 * Q_BLOCK, :] = q_odd[rs]
    k_win[PREV:PREV + ts, :] = head_rms(k_raw, kg_ref).astype(BF16)

    def sg_u():
        u_scr[...] = jax.nn.gelu(proj(_C_U, SG_WIDTH))

    def sg_v():
        vs_scr[...] = jax.nn.gelu(proj(_C_VS, SG_WIDTH))

    gdim = SG_WIDTH // SG_GROUPS

    def sg_group(g):
        cs = slice(g * gdim, (g + 1) * gdim)
        vg = vs_scr[:, cs]
        mu = jnp.mean(vg, axis=-1, keepdims=True)
        xc = vg - mu
        var = jnp.mean(xc * xc, axis=-1, keepdims=True)
        vn = (xc * lax.rsqrt(var + EPS) * lng_ref[:, cs] + lnb_ref[:, cs]).astype(BF16)
        r_i = lax.broadcasted_iota(jnp.int32, (SG_BLOCK, SG_BLOCK), 0)
        c_i = lax.broadcasted_iota(jnp.int32, (SG_BLOCK, SG_BLOCK), 1)
        wm = jnp.where(r_i >= c_i, sgw_ref[g], 0.0).astype(BF16)
        for j in range(ts // SG_BLOCK):
            rs = slice(j * SG_BLOCK, (j + 1) * SG_BLOCK)
            mixed = _dot(wm, vn[rs]) + sgb_ref[:, cs]
            ysg_scr[rs, cs] = (u_scr[rs, cs] * mixed).astype(BF16)

    def gate_a_chunk(c):
        cols = slice(c * GATE_CHUNK, (c + 1) * GATE_CHUNK)
        ga_scr[:, cols] = jax.nn.sigmoid(proj(_C_GA + c * GATE_CHUNK, GATE_CHUNK))

    def gated_b_chunk(c):
        cols = slice(c * GATE_CHUNK, (c + 1) * GATE_CHUNK)
        gate = jax.nn.sigmoid(proj(_C_GB + c * GATE_CHUNK, GATE_CHUNK))
        mb_scr[:, cols] = gate * _dot(ysg_scr[...], wb_ref[:, cols])

    n_chunks = D_MODEL // GATE_CHUNK
    side_work = [sg_u, sg_v] + [functools.partial(sg_group, g) for g in range(SG_GROUPS)]
    side_work += [functools.partial(gate_a_chunk, c) for c in range(n_chunks)]
    side_work += [functools.partial(gated_b_chunk, c) for c in range(n_chunks)]

    n_blocks = (ts // Q_BLOCK) * (N_HEADS // 2)
    schedule = [[] for _ in range(n_blocks)]
    for i, piece in enumerate(side_work):
        schedule[-(-(i + 1) * n_blocks // len(side_work)) - 1].append(piece)

    key_row = lax.broadcasted_iota(jnp.int32, (WIN, 1), 0)
    pairs = N_HEADS // 2

    def scores(block):
        qb, pair = divmod(block, pairs)
        r0 = qb * Q_BLOCK
        cs = slice(pair * 2 * HEAD_DIM, (pair + 1) * 2 * HEAD_DIM)
        st = _dot_nt(k_win[r0:r0 + WIN, cs], q_scr[qb, :, cs]) + bias_ref[pair]
        return jnp.where(key_row >= (PREV - (s_idx * ts + r0)), st, NEG)

    def weighted_values(block, p, inv):
        qb, pair = divmod(block, pairs)
        r0 = qb * Q_BLOCK
        cs = slice(pair * 2 * HEAD_DIM, (pair + 1) * 2 * HEAD_DIM)
        ot = _dot(vt_win[cs, r0:r0 + WIN], p)
        o = jnp.concatenate(
            [ot[0:HEAD_DIM, 0:Q_BLOCK] * inv[:, 0:Q_BLOCK],
             ot[HEAD_DIM:2 * HEAD_DIM, Q_BLOCK:2 * Q_BLOCK] * inv[:, Q_BLOCK:2 * Q_BLOCK]],
            axis=0)
        ya_scr[r0:r0 + Q_BLOCK, cs] = o.T.astype(BF16)

    st_next = scores(0)
    pending = None
    for block in range(n_blocks):
        st = st_next
        if block + 1 < n_blocks:
            st_next = scores(block + 1)
        if pending is not None:
            weighted_values(*pending)
        m = jnp.max(st, axis=0, keepdims=True)
        p = jnp.exp(st - m)
        inv = 1.0 / jnp.sum(p, axis=0, keepdims=True)
        pending = (block, p.astype(BF16), inv)
        for piece in schedule[block]:
            piece()
    weighted_values(*pending)

    merged = ga_scr[...] * _dot(ya_scr[...], wa_ref[...]) + mb_scr[...]
    x1 = x_ref[0] + _dot(merged.astype(BF16), wo_ref[...])
    x1_ref[0] = x1

    ms2 = jnp.mean(x1 * x1, axis=-1, keepdims=True)
    h2 = x1 * lax.rsqrt(ms2 + EPS) * g2_ref[...]
    h2_ref[0] = _pack_bf16_pairs(h2)
    h2_hi = h2.astype(BF16)
    h2_lo = (h2 - h2_hi.astype(F32)).astype(BF16)
    by_hi = _dot_nt(rwt_ref[...], h2_hi)
    lt = (by_hi[0:N_EXPERTS] + by_hi[N_EXPERTS:2 * N_EXPERTS]
          + _dot_nt(rwt_ref[0:N_EXPERTS, :], h2_lo) + rb_ref[...])
    e_iota = lax.broadcasted_iota(jnp.int32, lt.shape, 0)
    vals = []
    chosen = jnp.zeros(lt.shape, F32)
    for k in range(TOP_K):
        m = jnp.max(lt, axis=0, keepdims=True)
        idx = jnp.min(jnp.where(lt == m, e_iota, N_EXPERTS), axis=0, keepdims=True)
        vals.append(m)
        ti_ref[k:k + 1, :] = idx
        hit = e_iota == idx
        chosen = chosen + hit.astype(F32)
        lt = jnp.where(hit, -jnp.inf, lt)
    exps = [jnp.exp(v - vals[0]) for v in vals]
    denom = exps[0] + exps[1] + exps[2] + exps[3]
    for k in range(TOP_K):
        pr_ref[k:k + 1, :] = exps[k] / denom

    @pl.when((pl.program_id(0) == 0) & (s_idx == 0))
    def _():
        cnt_scr[...] = jnp.zeros_like(cnt_scr)

    cnt_scr[...] = cnt_scr[...] + jnp.sum(chosen, axis=1, keepdims=True)
    cnt_ref[...] = cnt_scr[...].astype(jnp.int32)


def _const_spec(shape):
    zeros = (0,) * len(shape)
    return pl.BlockSpec(shape, lambda b, s: zeros, pipeline_mode=pl.Buffered(1))


def _mixer(x, consts):
    bsz, seq, d = x.shape
    ts = SEQ_TILE
    ns = seq // ts
    t = bsz * seq
    tok_spec = pl.BlockSpec((1, ts, d), lambda b, s: (b, s, 0))
    packed_spec = pl.BlockSpec((1, ts, d // 2), lambda b, s: (b, s, 0))
    idx_spec = pl.BlockSpec((TOP_K, ts), lambda b, s: (0, b * ns + s))
    return pl.pallas_call(
        _mixer_kernel,
        grid=(bsz, ns),
        in_specs=[tok_spec] + [_const_spec(c.shape) for c in consts],
        out_specs=[tok_spec, packed_spec, idx_spec, idx_spec,
                   pl.BlockSpec((N_EXPERTS, LANES), lambda b, s: (0, 0))],
        out_shape=[
            jax.ShapeDtypeStruct((bsz, seq, d), F32),
            jax.ShapeDtypeStruct((bsz, seq, d // 2), jnp.int32),
            jax.ShapeDtypeStruct((TOP_K, t), jnp.int32),
            jax.ShapeDtypeStruct((TOP_K, t), F32),
            jax.ShapeDtypeStruct((N_EXPERTS, LANES), jnp.int32),
        ],
        scratch_shapes=[
            pltpu.VMEM((ts, d), BF16),
            pltpu.VMEM((ts // Q_BLOCK, 2 * Q_BLOCK, ATTN_WIDTH), BF16),
            pltpu.VMEM((PREV + ts, ATTN_WIDTH), BF16),
            pltpu.VMEM((ATTN_WIDTH, PREV + ts), BF16),
            pltpu.VMEM((ts, ATTN_WIDTH), BF16),
            pltpu.VMEM((ts, SG_WIDTH), F32),
            pltpu.VMEM((ts, SG_WIDTH), F32),
            pltpu.VMEM((ts, SG_WIDTH), BF16),
            pltpu.VMEM((ts, d), F32),
            pltpu.VMEM((ts, d), F32),
            pltpu.VMEM((N_EXPERTS, LANES), F32),
        ],
        compiler_params=pltpu.CompilerParams(
            dimension_semantics=("arbitrary", "arbitrary"),
            vmem_limit_bytes=VMEM_LIMIT_BYTES),
        name="mixer",
    )(x, *consts)


def _attention_bias(rel_bias):
    i = np.arange(Q_BLOCK)[:, None]
    j = np.arange(WIN)[None, :]
    kk = j - (i // CHUNK) * CHUNK
    inside = (kk >= 0) & (kk < BAND)
    tab = rel_bias.astype(F32)
    far = PREV + Q_BLOCK - 1 - REL_CLIP + 1
    near = Q_BLOCK - 1 + REL_CLIP
    ext = jnp.concatenate(
        [jnp.broadcast_to(tab[:, 2 * REL_CLIP:], (N_HEADS, far)),
         tab[:, 2 * REL_CLIP - near:2 * REL_CLIP][:, ::-1]], axis=1)
    rows = [ext[:, Q_BLOCK - 1 - r:Q_BLOCK - 1 - r + WIN] for r in range(Q_BLOCK)]
    bias = jnp.where(inside[None], jnp.stack(rows, axis=1), NEG)
    return (bias.reshape(N_HEADS // 2, 2 * Q_BLOCK, WIN).transpose(0, 2, 1))


def _rank_kernel(ti_ref, upper_ref, cnt_ref, dest_ref, base_scr):
    i = pl.program_id(0)
    tt = ti_ref.shape[1]

    @pl.when(i == 0)
    def _():
        total = cnt_ref[...].astype(F32)
        padded = jnp.floor((total + (ROW_TILE - 1)) * (1.0 / ROW_TILE)) * ROW_TILE
        run = jnp.zeros((1, LANES), F32)
        for e in range(N_EXPERTS):
            base_scr[e:e + 1, :] = run
            run = run + padded[e:e + 1, :]

    ti = ti_ref[...]
    e_iota = lax.broadcasted_iota(jnp.int32, (N_EXPERTS, tt), 0)
    hits = [e_iota == ti[k:k + 1, :] for k in range(TOP_K)]
    cnt = hits[0].astype(F32)
    for k in range(1, TOP_K):
        cnt = cnt + hits[k].astype(F32)
    before = _dot(cnt.astype(BF16), upper_ref[...]) + base_scr[:, 0:1]
    for k in range(TOP_K):
        dest_ref[k:k + 1, :] = jnp.sum(jnp.where(hits[k], before, 0.0), axis=0,
                                       keepdims=True).astype(jnp.int32)
    base_scr[...] = base_scr[...] + jnp.sum(cnt, axis=1, keepdims=True)


def _rank(top_i, counts):
    t = top_i.shape[1]
    tt = RANK_TILE
    upper = jnp.asarray(np.triu(np.ones((tt, tt), np.float32), 1), BF16)
    return pl.pallas_call(
        _rank_kernel,
        grid=(t // tt,),
        in_specs=[pl.BlockSpec((TOP_K, tt), lambda i: (0, i)),
                  pl.BlockSpec((tt, tt), lambda i: (0, 0), pipeline_mode=pl.Buffered(1)),
                  pl.BlockSpec((N_EXPERTS, LANES), lambda i: (0, 0))],
        out_specs=pl.BlockSpec((TOP_K, tt), lambda i: (0, i)),
        out_shape=jax.ShapeDtypeStruct((TOP_K, t), jnp.int32),
        scratch_shapes=[pltpu.VMEM((N_EXPERTS, LANES), F32)],
        compiler_params=pltpu.CompilerParams(dimension_semantics=("arbitrary",)),
        name="rank",
    )(top_i, upper, counts)


def _plan_kernel(cnt_ref, tile_ref, exp_ref, rows_ref, first_ref):
    n_tiles = exp_ref.shape[0]

    def per_expert(e, w):
        c = cnt_ref[e]
        n = (c + ROW_TILE - 1) // ROW_TILE

        def per_tile(j, _):
            tile_ref[w + j] = w + j
            exp_ref[w + j] = e
            rows_ref[w + j] = jnp.minimum(c - j * ROW_TILE, ROW_TILE)
            first_ref[w + j] = (j == 0).astype(jnp.int32)
            return 0

        lax.fori_loop(0, n, per_tile, 0)
        return w + n

    n_used = lax.fori_loop(0, N_EXPERTS, per_expert, jnp.int32(0))

    def pad(w, _):
        tile_ref[w] = n_used - 1
        exp_ref[w] = exp_ref[n_used - 1]
        rows_ref[w] = 0
        first_ref[w] = 0
        return 0

    lax.fori_loop(n_used, n_tiles, pad, 0)


def _plan(counts, n_tiles):
    smem = pl.BlockSpec(memory_space=pltpu.SMEM)
    vec = jax.ShapeDtypeStruct((n_tiles,), jnp.int32)
    return pl.pallas_call(
        _plan_kernel,
        in_specs=[smem],
        out_specs=[smem, smem, smem, smem],
        out_shape=[vec, vec, vec, vec],
        name="plan",
    )(counts)


def _scatter(h2, dest, n_slots):
    t, d = h2.shape
    per_worker = t // SC_WORKERS
    n = SC_SCATTER_CHUNK
    nch = per_worker // n
    assert t == SC_WORKERS * nch * n and nch % 2 == 0
    mesh = plsc.VectorSubcoreMesh(core_axis_name="c", subcore_axis_name="s")

    @functools.partial(
        pl.kernel, mesh=mesh,
        out_type=jax.ShapeDtypeStruct((n_slots, d), h2.dtype),
        scratch_types=[pltpu.VMEM((TOP_K, nch, n), jnp.int32), pltpu.VMEM((2, n, d), h2.dtype),
                       pltpu.SemaphoreType.DMA((2,)), pltpu.SemaphoreType.DMA((2,))],
        name="sc_scatter")
    def scatter_kernel(h_hbm, dest_hbm, xs_hbm, idx_v, rows_v, sem_in, sem_out):
        wid = lax.axis_index("s") * SC_CORES + lax.axis_index("c")
        base = wid * per_worker
        for k in range(TOP_K):
            pltpu.sync_copy(dest_hbm.at[k, pl.ds(wid * nch, nch)], idx_v.at[k])

        def load(c, s):
            return pltpu.make_async_copy(h_hbm.at[pl.ds(base + c * n, n)], rows_v.at[s],
                                         sem_in.at[s])

        def store(c, s, k):
            return pltpu.make_async_copy(rows_v.at[s], xs_hbm.at[idx_v.at[k, c]], sem_out.at[s])

        load(0, 0).start()

        @pl.loop(0, nch, step=2)
        def _(c):
            for s in range(2):
                cc = c + s

                @pl.when(cc >= 1)
                def _():
                    for k in range(TOP_K):
                        store(cc - 1, 1 - s, k).wait()

                @pl.when(cc + 1 < nch)
                def _():
                    load(cc + 1, 1 - s).start()

                load(cc, s).wait()
                for k in range(TOP_K):
                    store(cc, s, k).start()

        for k in range(TOP_K):
            store(nch - 1, (nch - 1) % 2, k).wait()

    return scatter_kernel(h2, dest.reshape(TOP_K, t // n, n))


def _expert_kernel(tile_ref, exp_ref, rows_ref, first_ref,
                   xs_ref, wgu_ref, bgu_ref, wd_ref, bd_ref, y_ref, wgu_bf, wd_bf):
    w = pl.program_id(0)

    @pl.when(first_ref[w] == 1)
    def _():
        wgu_bf[...] = wgu_ref[0].astype(BF16)
        wd_bf[...] = wd_ref[0].astype(BF16)

    n_rows = rows_ref[w]

    @pl.when(n_rows > 0)
    def _():
        row = lax.broadcasted_iota(jnp.int32, xs_ref.shape, 0)
        x = _unpack_bf16_pairs(jnp.where(row < n_rows, xs_ref[...], 0))
        gu = _dot(x, wgu_bf[...]) + bgu_ref[0]
        gate = jnp.minimum(gu[:, :D_FF], SWIGLU_LIMIT)
        up = jnp.clip(gu[:, D_FF:], -SWIGLU_LIMIT, SWIGLU_LIMIT)
        act = (up + 1.0) * (gate * jax.nn.sigmoid(gate * SWIGLU_ALPHA))
        y_ref[...] = _pack_bf16_pairs(_dot(act.astype(BF16), wd_bf[...]) + bd_ref[0])


def _experts(xs, plan, wgu, bgu, wd, bd):
    n_rows, dp = xs.shape
    d = 2 * dp
    w_tile, w_exp, w_rows, w_first = plan
    n_items = w_tile.shape[0]
    tm = ROW_TILE
    by_tile = lambda w, tl, ex, nr, fi: (tl[w], 0)
    by_expert = lambda w, tl, ex, nr, fi: (ex[w], 0, 0)
    grid_spec = pltpu.PrefetchScalarGridSpec(
        num_scalar_prefetch=4,
        grid=(n_items,),
        in_specs=[
            pl.BlockSpec((tm, dp), by_tile),
            pl.BlockSpec((1, d, 2 * D_FF), by_expert),
            pl.BlockSpec((1, 1, 2 * D_FF), by_expert),
            pl.BlockSpec((1, D_FF, d), by_expert),
            pl.BlockSpec((1, 1, d), by_expert),
        ],
        out_specs=pl.BlockSpec((tm, dp), by_tile),
        scratch_shapes=[pltpu.VMEM((d, 2 * D_FF), BF16), pltpu.VMEM((D_FF, d), BF16)],
    )
    return pl.pallas_call(
        _expert_kernel,
        grid_spec=grid_spec,
        out_shape=jax.ShapeDtypeStruct((n_rows, dp), jnp.int32),
        compiler_params=pltpu.CompilerParams(
            dimension_semantics=("arbitrary",),
            vmem_limit_bytes=VMEM_LIMIT_BYTES),
        name="experts",
    )(w_tile, w_exp, w_rows, w_first, xs, wgu, bgu, wd, bd)


def _combine(x1, probs, dest, y):
    t, d = x1.shape
    per_worker = t // SC_WORKERS
    dp = d // 2
    n = SC_COMBINE_CHUNK
    nch = per_worker // n
    assert t == SC_WORKERS * nch * n and nch % 2 == 0
    mesh = plsc.VectorSubcoreMesh(core_axis_name="c", subcore_axis_name="s")
    high_half = -65536

    @functools.partial(
        pl.kernel, mesh=mesh,
        out_type=jax.ShapeDtypeStruct((t, d), F32),
        scratch_types=[pltpu.VMEM((TOP_K, per_worker), jnp.int32),
                       pltpu.VMEM((2, TOP_K, n, dp), jnp.int32),
                       pltpu.VMEM((2, n, d), F32), pltpu.VMEM((2, n, d), F32),
                       pltpu.VMEM((TOP_K, per_worker), F32),
                       pltpu.SemaphoreType.DMA((2,)), pltpu.SemaphoreType.DMA((2,))],
        compiler_params=pltpu.CompilerParams(needs_layout_passes=False),
        name="sc_combine")
    def combine_kernel(x1_hbm, p_hbm, dest_hbm, y_hbm, out_hbm,
                       idx_v, rows_v, x_v, o_v, p_v, sem_in, sem_out):
        wid = lax.axis_index("s") * SC_CORES + lax.axis_index("c")
        base = wid * per_worker
        for k in range(TOP_K):
            pltpu.sync_copy(dest_hbm.at[k, pl.ds(base, per_worker)], idx_v.at[k])
            pltpu.sync_copy(p_hbm.at[k, pl.ds(base, per_worker)], p_v.at[k])

        def loads(c, s):
            off = base + c * n
            cps = [pltpu.make_async_copy(y_hbm.at[idx_v.at[k, pl.ds(c * n, n)]], rows_v.at[s, k],
                                         sem_in.at[s])
                   for k in range(TOP_K)]
            cps.append(pltpu.make_async_copy(x1_hbm.at[pl.ds(off, n)], x_v.at[s], sem_in.at[s]))
            return cps

        def store(c, s):
            return pltpu.make_async_copy(o_v.at[s], out_hbm.at[pl.ds(base + c * n, n)],
                                         sem_out.at[s])

        for cp in loads(0, 0):
            cp.start()

        @pl.loop(0, nch, step=2)
        def _(c):
            for s in range(2):
                cc = c + s

                @pl.when(cc + 1 < nch)
                def _():
                    for cp in loads(cc + 1, 1 - s):
                        cp.start()

                for cp in loads(cc, s):
                    cp.wait()

                @pl.when(cc >= 2)
                def _():
                    store(cc - 2, s).wait()

                @pl.loop(0, n)
                def _(r):
                    tok = jnp.full((SC_LANES,), cc * n + r, jnp.int32)
                    w = [plsc.load_gather(p_v, [jnp.full((SC_LANES,), k, jnp.int32), tok])
                         for k in range(TOP_K)]
                    for g in range(dp // SC_LANES):
                        lo_sl = pl.ds(g * SC_LANES, SC_LANES)
                        hi_sl = pl.ds(dp + g * SC_LANES, SC_LANES)
                        lo = x_v[s, r, lo_sl]
                        hi = x_v[s, r, hi_sl]
                        for k in range(TOP_K):
                            word = rows_v[s, k, r, lo_sl]
                            lo = lo + w[k] * plsc.bitcast(word << 16, F32)
                            hi = hi + w[k] * plsc.bitcast(word & high_half, F32)
                        o_v[s, r, lo_sl] = lo
                        o_v[s, r, hi_sl] = hi

                store(cc, s).start()

        store(nch - 2, 0).wait()
        store(nch - 1, 1).wait()

    return combine_kernel(x1, probs, dest, y)


def _layer(x, norm1_g, w_in, q_norm_g, k_norm_g, rel_bias, sg_ln_g, sg_ln_b, sg_w, sg_b,
           w_branch_a, w_branch_b, w_out, norm2_g, router_w, router_b,
           w_gate_up, b_gate_up, w_down, b_down):
    bsz, seq, d = x.shape
    t = bsz * seq
    assert d == D_MODEL and seq % SEQ_TILE == 0 and SEQ_TILE % Q_BLOCK == 0
    assert t % RANK_TILE == 0 and (t * TOP_K) % ROW_TILE == 0
    row = lambda v: v.reshape(1, -1).astype(F32)
    scale = 1.0 / math.sqrt(HEAD_DIM)
    sgb = jnp.repeat(sg_b.T.astype(F32), SG_WIDTH // SG_GROUPS, axis=1)

    rw = router_w.T.astype(F32)
    rw_hi = rw.astype(BF16)
    rw_lo = (rw - rw_hi.astype(F32)).astype(BF16)
    mixer_consts = (
        row(norm1_g), w_in.astype(BF16),
        row(jnp.tile(q_norm_g, N_HEADS)) * scale, row(jnp.tile(k_norm_g, N_HEADS)),
        _attention_bias(rel_bias), row(sg_ln_g), row(sg_ln_b),
        sg_w.astype(F32), sgb, w_branch_a.astype(BF16), w_branch_b.astype(BF16),
        w_out.astype(BF16), row(norm2_g), jnp.concatenate([rw_hi, rw_lo], axis=0),
        router_b.reshape(N_EXPERTS, 1).astype(F32))
    expert_params = (w_gate_up.astype(F32), b_gate_up.reshape(N_EXPERTS, 1, -1).astype(F32),
                     w_down.astype(F32), b_down.reshape(N_EXPERTS, 1, -1).astype(F32))

    x1, h2, top_i, probs, counts = _mixer(x, mixer_consts)
    dest = _rank(top_i, counts)
    n_tiles = t * TOP_K // ROW_TILE + N_EXPERTS
    plan = _plan(counts[:, 0], n_tiles)
    xs = _scatter(h2.reshape(t, d // 2), dest, n_tiles * ROW_TILE)
    y = _experts(xs, plan, *expert_params)
    return _combine(x1.reshape(t, d), probs, dest, y).reshape(bsz, seq, d)


def kernel(x, norm1_g, w_in, q_norm_g, k_norm_g, rel_bias, sg_ln_g, sg_ln_b, sg_w, sg_b,
           w_branch_a, w_branch_b, w_out, norm2_g, router_w, router_b,
           w_gate_up, b_gate_up, w_down, b_down):
    depth = norm1_g.shape[0]
    for l in range(depth):
        x = _layer(x, norm1_g[l], w_in[l], q_norm_g[l], k_norm_g[l], rel_bias[l], sg_ln_g[l],
                   sg_ln_b[l], sg_w[l], sg_b[l], w_branch_a[l], w_branch_b[l], w_out[l],
                   norm2_g[l], router_w[l], router_b[l], w_gate_up[l], b_gate_up[l],
                   w_down[l], b_down[l])
    return x
```
